```python
import math
import jax, jax.numpy as jnp
from jax import lax
import numpy as np

D_MODEL = 2048
BATCH = 8
SEQ = 8192
DEPTH = 4

ATTN_WIDTH = D_MODEL // 2
HEAD_DIM = 64
N_ATTN_HEADS = ATTN_WIDTH // HEAD_DIM
ATTN_PATTERNS = ((128, 1), (512, 4), (2048, 16))
SSM_WIDTH = D_MODEL - ATTN_WIDTH
SSM_GROUP = 16
SSM_STATE = 64
N_SSM_GROUPS = SSM_WIDTH // SSM_GROUP
IN_WIDTH = 3 * ATTN_WIDTH + SSM_WIDTH
D_FF = 4 * D_MODEL
DEEPNORM_ALPHA = (2 * DEPTH) ** 0.25
DEEPNORM_BETA = (8 * DEPTH) ** -0.25
LN_EPS = 1e-5
NEG_BIG = -1e30
STEP_MIN = 1e-3
STEP_MAX = 1e-1

kernel_name = 'hymba_s5_longnet_deepnorm_encoder'


def _layer_norm(x, g, b):
    xf = x.astype(jnp.float32)
    mu = xf.mean(-1, keepdims=True)
    var = jnp.square(xf - mu).mean(-1, keepdims=True)
    y = (xf - mu) * lax.rsqrt(var + LN_EPS) * g.astype(jnp.float32) + b.astype(jnp.float32)
    return y.astype(x.dtype)


def _rms_norm(x, g):
    xf = x.astype(jnp.float32)
    y = xf * lax.rsqrt(jnp.square(xf).mean(-1, keepdims=True) + LN_EPS) * g.astype(jnp.float32)
    return y.astype(x.dtype)


def _alibi_slopes(n_heads):
    return jnp.exp2(-8.0 * jnp.arange(1, n_heads + 1, dtype=jnp.float32) / n_heads)


def _dilated_band_attention(q, k, v, window, dilation):
    bsz, nh, seq, hd = q.shape
    half = window // (2 * dilation)
    length = seq // dilation
    nb = -(-length // half)
    lp = nb * half

    def strided(t):
        return t.reshape(bsz, nh, length, dilation, hd).transpose(0, 1, 3, 2, 4)

    qs, ks, vs = strided(q), strided(k), strided(v)
    qb = jnp.pad(qs, ((0, 0), (0, 0), (0, 0), (0, lp - length), (0, 0))).reshape(
        bsz, nh, dilation, nb, half, hd)

    def band(t):
        tp = jnp.pad(t, ((0, 0), (0, 0), (0, 0), (half, lp - length + half), (0, 0))).reshape(
            bsz, nh, dilation, nb + 2, half, hd)
        return jnp.concatenate([tp[:, :, :, 0:nb], tp[:, :, :, 1:nb + 1], tp[:, :, :, 2:nb + 2]], axis=-2)

    kb, vb = band(ks), band(vs)
    q_pos = jnp.arange(nb)[:, None] * half + jnp.arange(half)[None, :]
    k_pos = jnp.arange(nb)[:, None] * half - half + jnp.arange(3 * half)[None, :]
    rel = jnp.abs(k_pos[:, None, :] - q_pos[:, :, None])
    valid = (rel <= half) & (k_pos[:, None, :] >= 0) & (k_pos[:, None, :] < length)
    slopes = _alibi_slopes(nh)
    bias = -slopes[:, None, None, None, None] * (rel * dilation).astype(jnp.float32)

    scores = jnp.einsum('bhrnqd,bhrnkd->bhrnqk', qb, kb).astype(jnp.float32) * (hd ** -0.5) + bias
    scores = jnp.where(valid, scores, NEG_BIG)
    m = scores.max(-1, keepdims=True)
    p = jnp.exp(scores - m)
    den = p.sum(-1)
    out = jnp.einsum('bhrnqk,bhrnkd->bhrnqd', p, vb.astype(jnp.float32)) / den[..., None]
    lse = m[..., 0] + jnp.log(den)
    out = out.reshape(bsz, nh, dilation, lp, hd)[:, :, :, :length].transpose(0, 1, 3, 2, 4)
    lse = lse.reshape(bsz, nh, dilation, lp)[..., :length].transpose(0, 1, 3, 2)
    return out.reshape(bsz, nh, seq, hd), lse.reshape(bsz, nh, seq)


def _dilated_mixture_attention(q, k, v):
    outs, lses = [], []
    for window, dilation in ATTN_PATTERNS:
        o, l = _dilated_band_attention(q, k, v, window, dilation)
        outs.append(o)
        lses.append(l)
    w = jax.nn.softmax(jnp.stack(lses), axis=0)
    return jnp.einsum('pbhs,pbhsd->bhsd', w, jnp.stack(outs))


def _complex_affine_combine(e1, e2):
    a1r, a1i, b1r, b1i = e1
    a2r, a2i, b2r, b2i = e2
    return (a1r * a2r - a1i * a2i,
            a1r * a2i + a1i * a2r,
            a2r * b1r - a2i * b1i + b2r,
            a2r * b1i + a2i * b1r + b2i)


def _s5_bidirectional(u, lam_re, lam_im, log_step, b_re, b_im, c_re, c_im, d_skip):
    bsz, seq, _ = u.shape
    f32 = jnp.float32
    ug = u.reshape(bsz, seq, N_SSM_GROUPS, SSM_GROUP).astype(f32)
    y = ug * d_skip.reshape(N_SSM_GROUPS, SSM_GROUP).astype(f32)
    for direction in range(2):
        step = jnp.exp(log_step[direction].astype(f32))[:, None]
        lr, li = lam_re[direction].astype(f32), lam_im[direction].astype(f32)
        mag = jnp.exp(lr * step)
        a_re, a_im = mag * jnp.cos(li * step), mag * jnp.sin(li * step)
        den = lr * lr + li * li
        coef_re = ((a_re - 1.0) * lr + a_im * li) / den
        coef_im = (a_im * lr - (a_re - 1.0) * li) / den
        br, bi = b_re[direction].astype(f32), b_im[direction].astype(f32)
        bb_re = coef_re[..., None] * br - coef_im[..., None] * bi
        bb_im = coef_re[..., None] * bi + coef_im[..., None] * br
        bu_re = jnp.einsum('bsgc,gpc->bsgp', ug, bb_re)
        bu_im = jnp.einsum('bsgc,gpc->bsgp', ug, bb_im)
        shape = bu_re.shape
        elems = (jnp.broadcast_to(a_re, shape), jnp.broadcast_to(a_im, shape), bu_re, bu_im)
        _, _, x_re, x_im = lax.associative_scan(_complex_affine_combine, elems,
                                                reverse=(direction == 1), axis=1)
        y = (y + jnp.einsum('bsgp,gcp->bsgc', x_re, c_re[direction].astype(f32))
             - jnp.einsum('bsgp,gcp->bsgc', x_im, c_im[direction].astype(f32)))
    return y.reshape(bsz, seq, SSM_WIDTH).astype(u.dtype)


def _hybrid_mixer(h, w_in, lam_re, lam_im, log_step, b_re, b_im, c_re, c_im, d_skip,
                  w_glu, b_glu, g_attn, g_ssm, w_out):
    bsz, seq, _ = h.shape
    proj = h @ w_in
    q = proj[..., :ATTN_WIDTH]
    k = proj[..., ATTN_WIDTH:2 * ATTN_WIDTH]
    v = proj[..., 2 * ATTN_WIDTH:3 * ATTN_WIDTH]
    u = proj[..., 3 * ATTN_WIDTH:]

    def heads(t):
        return t.reshape(bsz, seq, N_ATTN_HEADS, HEAD_DIM).transpose(0, 2, 1, 3)

    attn = _dilated_mixture_attention(heads(q), heads(k), heads(v))
    attn = attn.transpose(0, 2, 1, 3).reshape(bsz, seq, ATTN_WIDTH).astype(h.dtype)

    ssm = jax.nn.gelu(_s5_bidirectional(u, lam_re, lam_im, log_step, b_re, b_im, c_re, c_im, d_skip))
    ssm = ssm * jax.nn.sigmoid(ssm @ w_glu + b_glu)

    merged = jnp.concatenate([_rms_norm(attn, g_attn), _rms_norm(ssm, g_ssm)], axis=-1)
    return merged @ w_out


def _fwd_setup_inputs(seed: int = 0) -> dict:
    key = jax.random.key(seed)
    ks = jax.random.split(key, 24)
    f32 = jnp.float32

    def nrm(k, shape, s):
        return s * jax.random.normal(k, shape, f32)

    lam_shape = (DEPTH, 2, N_SSM_GROUPS, SSM_STATE)
    return {
        'x': nrm(ks[0], (BATCH, SEQ, D_MODEL), 1.0),
        'c': nrm(ks[1], (BATCH, D_MODEL), 1.0),
        'w_ada': nrm(ks[2], (DEPTH, D_MODEL, 6 * D_MODEL), 0.1 * D_MODEL ** -0.5),
        'b_ada': nrm(ks[3], (DEPTH, 6 * D_MODEL), 0.01),
        'w_in': nrm(ks[4], (DEPTH, D_MODEL, IN_WIDTH), D_MODEL ** -0.5),
        'ssm_lam_re': -0.5 * jnp.exp(nrm(ks[5], lam_shape, 0.05)),
        'ssm_lam_im': jnp.pi * jnp.arange(SSM_STATE, dtype=f32) + nrm(ks[6], lam_shape, 0.01),
        'ssm_log_step': jax.random.uniform(ks[7], (DEPTH, 2, N_SSM_GROUPS), f32,
                                           math.log(STEP_MIN), math.log(STEP_MAX)),
        'ssm_b_re': nrm(ks[8], (DEPTH, 2, N_SSM_GROUPS, SSM_STATE, SSM_GROUP), 0.5),
        'ssm_b_im': nrm(ks[9], (DEPTH, 2, N_SSM_GROUPS, SSM_STATE, SSM_GROUP), 0.5),
        'ssm_c_re': nrm(ks[10], (DEPTH, 2, N_SSM_GROUPS, SSM_GROUP, SSM_STATE), (2 * SSM_STATE) ** -0.5),
        'ssm_c_im': nrm(ks[11], (DEPTH, 2, N_SSM_GROUPS, SSM_GROUP, SSM_STATE), (2 * SSM_STATE) ** -0.5),
        'ssm_d': nrm(ks[12], (DEPTH, SSM_WIDTH), 1.0),
        'w_glu': nrm(ks[13], (DEPTH, SSM_WIDTH, SSM_WIDTH), SSM_WIDTH ** -0.5),
        'b_glu': nrm(ks[14], (DEPTH, SSM_WIDTH), 0.01),
        'g_attn': 1.0 + nrm(ks[15], (DEPTH, ATTN_WIDTH), 0.02),
        'g_ssm': 1.0 + nrm(ks[16], (DEPTH, SSM_WIDTH), 0.02),
        'w_out': nrm(ks[17], (DEPTH, D_MODEL, D_MODEL), DEEPNORM_BETA * D_MODEL ** -0.5),
        'ln1_g': 1.0 + nrm(ks[18], (DEPTH, D_MODEL), 0.02),
        'ln1_b': nrm(ks[19], (DEPTH, D_MODEL), 0.02),
        'w_mlp1': nrm(ks[20], (DEPTH, D_MODEL, D_FF), D_MODEL ** -0.5),
        'w_mlp2': nrm(ks[21], (DEPTH, D_FF, D_MODEL), DEEPNORM_BETA * D_FF ** -0.5),
        'ln2_g': 1.0 + nrm(ks[22], (DEPTH, D_MODEL), 0.02),
        'ln2_b': nrm(ks[23], (DEPTH, D_MODEL), 0.02),
    }


def _fwd_reference(x, c, w_ada, b_ada, w_in, ssm_lam_re, ssm_lam_im, ssm_log_step, ssm_b_re, ssm_b_im,
              ssm_c_re, ssm_c_im, ssm_d, w_glu, b_glu, g_attn, g_ssm, w_out, ln1_g, ln1_b,
              w_mlp1, w_mlp2, ln2_g, ln2_b):
    cond = jax.nn.silu(c)
    for layer in range(DEPTH):
        mods = cond @ w_ada[layer] + b_ada[layer]
        sh1, sc1, g1, sh2, sc2, g2 = jnp.split(mods[:, None, :], 6, axis=-1)
        h = x * (1.0 + sc1) + sh1
        mix = _hybrid_mixer(h, w_in[layer], ssm_lam_re[layer], ssm_lam_im[layer], ssm_log_step[layer],
                            ssm_b_re[layer], ssm_b_im[layer], ssm_c_re[layer], ssm_c_im[layer],
                            ssm_d[layer], w_glu[layer], b_glu[layer], g_attn[layer], g_ssm[layer],
                            w_out[layer])
        x = _layer_norm(DEEPNORM_ALPHA * x + (1.0 + g1) * mix, ln1_g[layer], ln1_b[layer])
        h = x * (1.0 + sc2) + sh2
        ff = jnp.square(jax.nn.relu(h @ w_mlp1[layer])) @ w_mlp2[layer]
        x = _layer_norm(DEEPNORM_ALPHA * x + (1.0 + g2) * ff, ln2_g[layer], ln2_b[layer])
    return x


import jax as _jax
import jax.numpy as _jnp

TWIN_FORMAT = 'train_step'
FWD_PARAMS = ['x', 'c', 'w_ada', 'b_ada', 'w_in', 'ssm_lam_re', 'ssm_lam_im', 'ssm_log_step', 'ssm_b_re', 'ssm_b_im', 'ssm_c_re', 'ssm_c_im', 'ssm_d', 'w_glu', 'b_glu', 'g_attn', 'g_ssm', 'w_out', 'ln1_g', 'ln1_b', 'w_mlp1', 'w_mlp2', 'ln2_g', 'ln2_b']
TWIN_WEIGHTS = ['w_ada', 'b_ada', 'w_in', 'ssm_lam_re', 'ssm_lam_im', 'ssm_log_step', 'ssm_b_re', 'ssm_b_im', 'ssm_c_re', 'ssm_c_im', 'ssm_d', 'w_glu', 'b_glu', 'g_attn', 'g_ssm', 'w_out', 'ln1_g', 'ln1_b', 'w_mlp1', 'w_mlp2', 'ln2_g', 'ln2_b']
TWIN_DIFF_INPUT = 'x'
TWIN_INPUTS = ['x', 'c', 'w_ada', 'b_ada', 'w_in', 'ssm_lam_re', 'ssm_lam_im', 'ssm_log_step', 'ssm_b_re', 'ssm_b_im', 'ssm_c_re', 'ssm_c_im', 'ssm_d', 'w_glu', 'b_glu', 'g_attn', 'g_ssm', 'w_out', 'ln1_g', 'ln1_b', 'w_mlp1', 'w_mlp2', 'ln2_g', 'ln2_b', 'loss_target', 'm_w_ada', 'm_b_ada', 'm_w_in', 'm_ssm_lam_re', 'm_ssm_lam_im', 'm_ssm_log_step', 'm_ssm_b_re', 'm_ssm_b_im', 'm_ssm_c_re', 'm_ssm_c_im', 'm_ssm_d', 'm_w_glu', 'm_b_glu', 'm_g_attn', 'm_g_ssm', 'm_w_out', 'm_ln1_g', 'm_ln1_b', 'm_w_mlp1', 'm_w_mlp2', 'm_ln2_g', 'm_ln2_b', 'v_w_ada', 'v_b_ada', 'v_w_in', 'v_ssm_lam_re', 'v_ssm_lam_im', 'v_ssm_log_step', 'v_ssm_b_re', 'v_ssm_b_im', 'v_ssm_c_re', 'v_ssm_c_im', 'v_ssm_d', 'v_w_glu', 'v_b_glu', 'v_g_attn', 'v_g_ssm', 'v_w_out', 'v_ln1_g', 'v_ln1_b', 'v_w_mlp1', 'v_w_mlp2', 'v_ln2_g', 'v_ln2_b']
TWIN_OUTPUTS = ['loss', 'grad_x', 'grad_w_ada', 'grad_b_ada', 'grad_w_in', 'grad_ssm_lam_re', 'grad_ssm_lam_im', 'grad_ssm_log_step', 'grad_ssm_b_re', 'grad_ssm_b_im', 'grad_ssm_c_re', 'grad_ssm_c_im', 'grad_ssm_d', 'grad_w_glu', 'grad_b_glu', 'grad_g_attn', 'grad_g_ssm', 'grad_w_out', 'grad_ln1_g', 'grad_ln1_b', 'grad_w_mlp1', 'grad_w_mlp2', 'grad_ln2_g', 'grad_ln2_b', 'delta_w_ada', 'delta_b_ada', 'delta_w_in', 'delta_ssm_lam_re', 'delta_ssm_lam_im', 'delta_ssm_log_step', 'delta_ssm_b_re', 'delta_ssm_b_im', 'delta_ssm_c_re', 'delta_ssm_c_im', 'delta_ssm_d', 'delta_w_glu', 'delta_b_glu', 'delta_g_attn', 'delta_g_ssm', 'delta_w_out', 'delta_ln1_g', 'delta_ln1_b', 'delta_w_mlp1', 'delta_w_mlp2', 'delta_ln2_g', 'delta_ln2_b', 'new_m_w_ada', 'new_m_b_ada', 'new_m_w_in', 'new_m_ssm_lam_re', 'new_m_ssm_lam_im', 'new_m_ssm_log_step', 'new_m_ssm_b_re', 'new_m_ssm_b_im', 'new_m_ssm_c_re', 'new_m_ssm_c_im', 'new_m_ssm_d', 'new_m_w_glu', 'new_m_b_glu', 'new_m_g_attn', 'new_m_g_ssm', 'new_m_w_out', 'new_m_ln1_g', 'new_m_ln1_b', 'new_m_w_mlp1', 'new_m_w_mlp2', 'new_m_ln2_g', 'new_m_ln2_b', 'new_v_w_ada', 'new_v_b_ada', 'new_v_w_in', 'new_v_ssm_lam_re', 'new_v_ssm_lam_im', 'new_v_ssm_log_step', 'new_v_ssm_b_re', 'new_v_ssm_b_im', 'new_v_ssm_c_re', 'new_v_ssm_c_im', 'new_v_ssm_d', 'new_v_w_glu', 'new_v_b_glu', 'new_v_g_attn', 'new_v_g_ssm', 'new_v_w_out', 'new_v_ln1_g', 'new_v_ln1_b', 'new_v_w_mlp1', 'new_v_w_mlp2', 'new_v_ln2_g', 'new_v_ln2_b']
TWIN_LEAF_KINDS = {'loss': 'loss', 'grad_x': 'grad_x', 'grad_w_ada': 'grad_w', 'grad_b_ada': 'grad_w', 'grad_w_in': 'grad_w', 'grad_ssm_lam_re': 'grad_w', 'grad_ssm_lam_im': 'grad_w', 'grad_ssm_log_step': 'grad_w', 'grad_ssm_b_re': 'grad_w', 'grad_ssm_b_im': 'grad_w', 'grad_ssm_c_re': 'grad_w', 'grad_ssm_c_im': 'grad_w', 'grad_ssm_d': 'grad_w', 'grad_w_glu': 'grad_w', 'grad_b_glu': 'grad_w', 'grad_g_attn': 'grad_w', 'grad_g_ssm': 'grad_w', 'grad_w_out': 'grad_w', 'grad_ln1_g': 'grad_w', 'grad_ln1_b': 'grad_w', 'grad_w_mlp1': 'grad_w', 'grad_w_mlp2': 'grad_w', 'grad_ln2_g': 'grad_w', 'grad_ln2_b': 'grad_w', 'delta_w_ada': 'delta_w', 'delta_b_ada': 'delta_w', 'delta_w_in': 'delta_w', 'delta_ssm_lam_re': 'delta_w', 'delta_ssm_lam_im': 'delta_w', 'delta_ssm_log_step': 'delta_w', 'delta_ssm_b_re': 'delta_w', 'delta_ssm_b_im': 'delta_w', 'delta_ssm_c_re': 'delta_w', 'delta_ssm_c_im': 'delta_w', 'delta_ssm_d': 'delta_w', 'delta_w_glu': 'delta_w', 'delta_b_glu': 'delta_w', 'delta_g_attn': 'delta_w', 'delta_g_ssm': 'delta_w', 'delta_w_out': 'delta_w', 'delta_ln1_g': 'delta_w', 'delta_ln1_b': 'delta_w', 'delta_w_mlp1': 'delta_w', 'delta_w_mlp2': 'delta_w', 'delta_ln2_g': 'delta_w', 'delta_ln2_b': 'delta_w', 'new_m_w_ada': 'new_m', 'new_m_b_ada': 'new_m', 'new_m_w_in': 'new_m', 'new_m_ssm_lam_re': 'new_m', 'new_m_ssm_lam_im': 'new_m', 'new_m_ssm_log_step': 'new_m', 'new_m_ssm_b_re': 'new_m', 'new_m_ssm_b_im': 'new_m', 'new_m_ssm_c_re': 'new_m', 'new_m_ssm_c_im': 'new_m', 'new_m_ssm_d': 'new_m', 'new_m_w_glu': 'new_m', 'new_m_b_glu': 'new_m', 'new_m_g_attn': 'new_m', 'new_m_g_ssm': 'new_m', 'new_m_w_out': 'new_m', 'new_m_ln1_g': 'new_m', 'new_m_ln1_b': 'new_m', 'new_m_w_mlp1': 'new_m', 'new_m_w_mlp2': 'new_m', 'new_m_ln2_g': 'new_m', 'new_m_ln2_b': 'new_m', 'new_v_w_ada': 'new_v', 'new_v_b_ada': 'new_v', 'new_v_w_in': 'new_v', 'new_v_ssm_lam_re': 'new_v', 'new_v_ssm_lam_im': 'new_v', 'new_v_ssm_log_step': 'new_v', 'new_v_ssm_b_re': 'new_v', 'new_v_ssm_b_im': 'new_v', 'new_v_ssm_c_re': 'new_v', 'new_v_ssm_c_im': 'new_v', 'new_v_ssm_d': 'new_v', 'new_v_w_glu': 'new_v', 'new_v_b_glu': 'new_v', 'new_v_g_attn': 'new_v', 'new_v_g_ssm': 'new_v', 'new_v_w_out': 'new_v', 'new_v_ln1_g': 'new_v', 'new_v_ln1_b': 'new_v', 'new_v_w_mlp1': 'new_v', 'new_v_w_mlp2': 'new_v', 'new_v_ln2_g': 'new_v', 'new_v_ln2_b': 'new_v'}


def _forward(args):
    return _fwd_reference(*[args[k] for k in FWD_PARAMS])


def _output_shape():
    def fwd():
        inp = _fwd_setup_inputs(0)
        return _fwd_reference(*[inp[k] for k in FWD_PARAMS])
    out = _jax.eval_shape(fwd)
    return out.shape, out.dtype

N_MICROBATCH = 1
ADAM_LR = 0.001
ADAM_B1 = 0.9
ADAM_B2 = 0.999
ADAM_EPS = 1e-08
ADAM_WD = 0.01
ADAM_STEP = 10
PER_EXAMPLE_BATCH_AXIS = {'x': 0, 'c': 0, 'loss_target': 0}
SHARED_INPUTS = []
_WEIGHT_DTYPES = {'w_ada': _jnp.float32, 'b_ada': _jnp.float32, 'w_in': _jnp.float32, 'ssm_lam_re': _jnp.float32, 'ssm_lam_im': _jnp.float32, 'ssm_log_step': _jnp.float32, 'ssm_b_re': _jnp.float32, 'ssm_b_im': _jnp.float32, 'ssm_c_re': _jnp.float32, 'ssm_c_im': _jnp.float32, 'ssm_d': _jnp.float32, 'w_glu': _jnp.float32, 'b_glu': _jnp.float32, 'g_attn': _jnp.float32, 'g_ssm': _jnp.float32, 'w_out': _jnp.float32, 'ln1_g': _jnp.float32, 'ln1_b': _jnp.float32, 'w_mlp1': _jnp.float32, 'w_mlp2': _jnp.float32, 'ln2_g': _jnp.float32, 'ln2_b': _jnp.float32}
MOMENT_SCALE = {'w_ada': 4.347028e-02, 'b_ada': 1.483779e-01, 'w_in': 3.897002e-02, 'ssm_lam_re': 9.161251e-03, 'ssm_lam_im': 1.159235e-02, 'ssm_log_step': 3.311910e+00, 'ssm_b_re': 1.836983e-03, 'ssm_b_im': 1.768813e-03, 'ssm_c_re': 1.017406e-02, 'ssm_c_im': 1.050876e-02, 'ssm_d': 6.645096e-02, 'w_glu': 1.207978e-02, 'b_glu': 2.718212e-02, 'g_attn': 6.883838e-02, 'g_ssm': 6.227391e-02, 'w_out': 1.615882e-01, 'ln1_g': 8.406134e-01, 'ln1_b': 6.567847e-01, 'w_mlp1': 2.405391e-02, 'w_mlp2': 1.838367e-01, 'ln2_g': 1.622790e+01, 'ln2_b': 5.006802e+00}


def _to_microbatches(a, axis):
    t = _jnp.moveaxis(a, axis, 0)
    t = t.reshape((N_MICROBATCH, t.shape[0] // N_MICROBATCH) + t.shape[1:])
    return _jnp.moveaxis(t, 1, axis + 1)


def setup_inputs(seed: int = 0) -> dict:
    inp = _fwd_setup_inputs(seed)
    key = _jax.random.fold_in(_jax.random.key(seed), 7919)
    shape, _ = _output_shape()
    out = dict(inp)
    out["loss_target"] = _jax.random.normal(_jax.random.fold_in(key, 0), shape, _jnp.float32)
    for i, name in enumerate(TWIN_WEIGHTS):
        w = inp[name].astype(_jnp.float32)
        if MOMENT_SCALE is None:
            s = _jnp.sqrt(_jnp.mean(_jnp.square(w)) + 1e-30)
        else:
            s = MOMENT_SCALE[name]
        km, kv = _jax.random.split(_jax.random.fold_in(key, i + 1))
        out[name] = w
        out["m_" + name] = s * _jax.random.normal(km, w.shape, _jnp.float32)
        out["v_" + name] = (s * s) * _jax.random.uniform(kv, w.shape, _jnp.float32, 0.5, 1.5)
    if N_MICROBATCH > 1:
        for name, axis in PER_EXAMPLE_BATCH_AXIS.items():
            out[name] = _to_microbatches(out[name], axis)
    return {'x': out['x'], 'c': out['c'], 'w_ada': out['w_ada'], 'b_ada': out['b_ada'], 'w_in': out['w_in'], 'ssm_lam_re': out['ssm_lam_re'], 'ssm_lam_im': out['ssm_lam_im'], 'ssm_log_step': out['ssm_log_step'], 'ssm_b_re': out['ssm_b_re'], 'ssm_b_im': out['ssm_b_im'], 'ssm_c_re': out['ssm_c_re'], 'ssm_c_im': out['ssm_c_im'], 'ssm_d': out['ssm_d'], 'w_glu': out['w_glu'], 'b_glu': out['b_glu'], 'g_attn': out['g_attn'], 'g_ssm': out['g_ssm'], 'w_out': out['w_out'], 'ln1_g': out['ln1_g'], 'ln1_b': out['ln1_b'], 'w_mlp1': out['w_mlp1'], 'w_mlp2': out['w_mlp2'], 'ln2_g': out['ln2_g'], 'ln2_b': out['ln2_b'], 'loss_target': out['loss_target'], 'm_w_ada': out['m_w_ada'], 'm_b_ada': out['m_b_ada'], 'm_w_in': out['m_w_in'], 'm_ssm_lam_re': out['m_ssm_lam_re'], 'm_ssm_lam_im': out['m_ssm_lam_im'], 'm_ssm_log_step': out['m_ssm_log_step'], 'm_ssm_b_re': out['m_ssm_b_re'], 'm_ssm_b_im': out['m_ssm_b_im'], 'm_ssm_c_re': out['m_ssm_c_re'], 'm_ssm_c_im': out['m_ssm_c_im'], 'm_ssm_d': out['m_ssm_d'], 'm_w_glu': out['m_w_glu'], 'm_b_glu': out['m_b_glu'], 'm_g_attn': out['m_g_attn'], 'm_g_ssm': out['m_g_ssm'], 'm_w_out': out['m_w_out'], 'm_ln1_g': out['m_ln1_g'], 'm_ln1_b': out['m_ln1_b'], 'm_w_mlp1': out['m_w_mlp1'], 'm_w_mlp2': out['m_w_mlp2'], 'm_ln2_g': out['m_ln2_g'], 'm_ln2_b': out['m_ln2_b'], 'v_w_ada': out['v_w_ada'], 'v_b_ada': out['v_b_ada'], 'v_w_in': out['v_w_in'], 'v_ssm_lam_re': out['v_ssm_lam_re'], 'v_ssm_lam_im': out['v_ssm_lam_im'], 'v_ssm_log_step': out['v_ssm_log_step'], 'v_ssm_b_re': out['v_ssm_b_re'], 'v_ssm_b_im': out['v_ssm_b_im'], 'v_ssm_c_re': out['v_ssm_c_re'], 'v_ssm_c_im': out['v_ssm_c_im'], 'v_ssm_d': out['v_ssm_d'], 'v_w_glu': out['v_w_glu'], 'v_b_glu': out['v_b_glu'], 'v_g_attn': out['v_g_attn'], 'v_g_ssm': out['v_g_ssm'], 'v_w_out': out['v_w_out'], 'v_ln1_g': out['v_ln1_g'], 'v_ln1_b': out['v_ln1_b'], 'v_w_mlp1': out['v_w_mlp1'], 'v_w_mlp2': out['v_w_mlp2'], 'v_ln2_g': out['v_ln2_g'], 'v_ln2_b': out['v_ln2_b']}


def _loss(weights, diff, rest, loss_target):
    with _jax.named_scope("forward"):
        args = {**rest, TWIN_DIFF_INPUT: diff, **{k: w.astype(_WEIGHT_DTYPES[k]) for k, w in weights.items()}}
        y = _forward(args)
    with _jax.named_scope("loss_head"):
        err = _jnp.square(y.astype(_jnp.float32) - loss_target)
        return 0.5 * _jnp.sum(_jnp.mean(err, axis=-1)) if err.ndim else 0.5 * err


def _adamw(w, g, m, v):
    m = ADAM_B1 * m + (1.0 - ADAM_B1) * g
    v = ADAM_B2 * v + (1.0 - ADAM_B2) * _jnp.square(g)
    m_hat = m / (1.0 - ADAM_B1 ** ADAM_STEP)
    v_hat = v / (1.0 - ADAM_B2 ** ADAM_STEP)
    delta = -ADAM_LR * (m_hat / (_jnp.sqrt(v_hat) + ADAM_EPS) + ADAM_WD * w)
    return delta, m, v


def reference(x, c, w_ada, b_ada, w_in, ssm_lam_re, ssm_lam_im, ssm_log_step, ssm_b_re, ssm_b_im, ssm_c_re, ssm_c_im, ssm_d, w_glu, b_glu, g_attn, g_ssm, w_out, ln1_g, ln1_b, w_mlp1, w_mlp2, ln2_g, ln2_b, loss_target, m_w_ada, m_b_ada, m_w_in, m_ssm_lam_re, m_ssm_lam_im, m_ssm_log_step, m_ssm_b_re, m_ssm_b_im, m_ssm_c_re, m_ssm_c_im, m_ssm_d, m_w_glu, m_b_glu, m_g_attn, m_g_ssm, m_w_out, m_ln1_g, m_ln1_b, m_w_mlp1, m_w_mlp2, m_ln2_g, m_ln2_b, v_w_ada, v_b_ada, v_w_in, v_ssm_lam_re, v_ssm_lam_im, v_ssm_log_step, v_ssm_b_re, v_ssm_b_im, v_ssm_c_re, v_ssm_c_im, v_ssm_d, v_w_glu, v_b_glu, v_g_attn, v_g_ssm, v_w_out, v_ln1_g, v_ln1_b, v_w_mlp1, v_w_mlp2, v_ln2_g, v_ln2_b):
    given = dict(x=x, c=c, w_ada=w_ada, b_ada=b_ada, w_in=w_in, ssm_lam_re=ssm_lam_re, ssm_lam_im=ssm_lam_im, ssm_log_step=ssm_log_step, ssm_b_re=ssm_b_re, ssm_b_im=ssm_b_im, ssm_c_re=ssm_c_re, ssm_c_im=ssm_c_im, ssm_d=ssm_d, w_glu=w_glu, b_glu=b_glu, g_attn=g_attn, g_ssm=g_ssm, w_out=w_out, ln1_g=ln1_g, ln1_b=ln1_b, w_mlp1=w_mlp1, w_mlp2=w_mlp2, ln2_g=ln2_g, ln2_b=ln2_b, loss_target=loss_target, m_w_ada=m_w_ada, m_b_ada=m_b_ada, m_w_in=m_w_in, m_ssm_lam_re=m_ssm_lam_re, m_ssm_lam_im=m_ssm_lam_im, m_ssm_log_step=m_ssm_log_step, m_ssm_b_re=m_ssm_b_re, m_ssm_b_im=m_ssm_b_im, m_ssm_c_re=m_ssm_c_re, m_ssm_c_im=m_ssm_c_im, m_ssm_d=m_ssm_d, m_w_glu=m_w_glu, m_b_glu=m_b_glu, m_g_attn=m_g_attn, m_g_ssm=m_g_ssm, m_w_out=m_w_out, m_ln1_g=m_ln1_g, m_ln1_b=m_ln1_b, m_w_mlp1=m_w_mlp1, m_w_mlp2=m_w_mlp2, m_ln2_g=m_ln2_g, m_ln2_b=m_ln2_b, v_w_ada=v_w_ada, v_b_ada=v_b_ada, v_w_in=v_w_in, v_ssm_lam_re=v_ssm_lam_re, v_ssm_lam_im=v_ssm_lam_im, v_ssm_log_step=v_ssm_log_step, v_ssm_b_re=v_ssm_b_re, v_ssm_b_im=v_ssm_b_im, v_ssm_c_re=v_ssm_c_re, v_ssm_c_im=v_ssm_c_im, v_ssm_d=v_ssm_d, v_w_glu=v_w_glu, v_b_glu=v_b_glu, v_g_attn=v_g_attn, v_g_ssm=v_g_ssm, v_w_out=v_w_out, v_ln1_g=v_ln1_g, v_ln1_b=v_ln1_b, v_w_mlp1=v_w_mlp1, v_w_mlp2=v_w_mlp2, v_ln2_g=v_ln2_g, v_ln2_b=v_ln2_b)
    weights = {n: given[n] for n in TWIN_WEIGHTS}
    shared = {n: given[n] for n in SHARED_INPUTS}
    per_example = {n: given[n] for n in ['x', 'c']}
    grad_fn = _jax.value_and_grad(_loss, argnums=(0, 1))

    def one_microbatch(ex, loss_target):
        ex = dict(ex)
        diff = ex.pop(TWIN_DIFF_INPUT)
        return grad_fn(weights, diff, {**shared, **ex}, loss_target)

    if N_MICROBATCH == 1:
        loss, (grad_w, grad_x) = one_microbatch(per_example, given["loss_target"])
    else:
        def body(carry, xs):
            loss_sum, grad_sum = carry
            l_k, (gw_k, gx_k) = one_microbatch(xs[0], xs[1])
            with _jax.named_scope("update"):
                return (loss_sum + l_k, _jax.tree.map(_jnp.add, grad_sum, gw_k)), gx_k

        init = (_jnp.zeros((), _jnp.float32), _jax.tree.map(_jnp.zeros_like, weights))
        (loss, grad_w), grad_x = _jax.lax.scan(body, init, (per_example, given["loss_target"]))
    with _jax.named_scope("update"):
        delta_w, new_m, new_v = {}, {}, {}
        for n in TWIN_WEIGHTS:
            delta_w[n], new_m[n], new_v[n] = _adamw(weights[n], grad_w[n], given["m_" + n], given["v_" + n])
    return (loss, grad_x, *[grad_w[n] for n in TWIN_WEIGHTS], *[delta_w[n] for n in TWIN_WEIGHTS],
            *[new_m[n] for n in TWIN_WEIGHTS], *[new_v[n] for n in TWIN_WEIGHTS])
```

```python
import functools
import math

import jax
import jax.numpy as jnp
from jax import lax
from jax.experimental import pallas as pl
from jax.experimental.pallas import tpu as pltpu

F32 = jnp.float32
BF16 = jnp.bfloat16
MESH = pl.DeviceIdType.MESH
ANY = pl.BlockSpec(memory_space=pl.ANY)

HEAD_DIM = 64
SSM_GROUP = 16
SSM_STATE = 64
GROUPS_PER_BLOCK = 8
ATTN_PATTERNS = ((128, 1), (512, 4), (2048, 16))
LN_EPS = 1e-5
NEG_BIG = -1e30
ADAM_LR, ADAM_B1, ADAM_B2, ADAM_EPS, ADAM_WD, ADAM_STEP = 0.001, 0.9, 0.999, 1e-08, 0.01, 10
VMEM_LIMIT_V7X = 56 * 1024 * 1024
N_DEV = 8

WEIGHTS = ['w_ada', 'b_ada', 'w_in', 'ssm_lam_re', 'ssm_lam_im', 'ssm_log_step', 'ssm_b_re', 'ssm_b_im',
           'ssm_c_re', 'ssm_c_im', 'ssm_d', 'w_glu', 'b_glu', 'g_attn', 'g_ssm', 'w_out', 'ln1_g', 'ln1_b',
           'w_mlp1', 'w_mlp2', 'ln2_g', 'ln2_b']
BIG = ['w_in', 'w_glu', 'w_out', 'w_mlp1', 'w_mlp2']
SMALL = [n for n in WEIGHTS if n not in BIG and n != 'w_ada']


def _params(n_grid):
    return pltpu.CompilerParams(dimension_semantics=("arbitrary",) * n_grid, vmem_limit_bytes=VMEM_LIMIT_V7X)


def _tile(n, cap):
    t = 1 << (max(1, min(n, cap)).bit_length() - 1)
    while n % t:
        t //= 2
    return t


def _operand_spec(stack, transposed, tr, tc, nrb, ncb, pick):
    def index(i, j, k):
        r, c = pick(i, j, k)
        s = None
        if stack == 'r':
            s, r = r // nrb, r % nrb
        elif stack == 'c':
            s, c = c // ncb, c % ncb
        idx = (c, r) if transposed else (r, c)
        return idx if s is None else (s,) + idx
    blk = (tc, tr) if transposed else (tr, tc)
    return pl.BlockSpec(blk if stack is None else (None,) + blk, index)


def _mm(a, b, *, M, N, K, name, out_dtype, ta=False, tb=False, a_st=None, b_st=None, o_st=None, ns=1,
        a_tf=None, epi=None, extra=None, bias=None, cap_m=1024, cap_n=1024, cap_k=1024):
    pm = M // ns if 'm' in (a_st, o_st) else M
    pn = N // ns if 'n' in (b_st, o_st) else N
    pk = K // ns if 'k' in (a_st, b_st) else K
    tm, tn, tk = _tile(pm, cap_m), _tile(pn, cap_n), _tile(pk, cap_k)
    gm, gn, gk = M // tm, N // tn, K // tk
    nmb, nnb, nkb = pm // tm, pn // tn, pk // tk
    a_spec = _operand_spec({None: None, 'm': 'r', 'k': 'c'}[a_st], ta, tm, tk, nmb, nkb, lambda i, j, k: (i, k))
    b_spec = _operand_spec({None: None, 'k': 'r', 'n': 'c'}[b_st], tb, tk, tn, nkb, nnb, lambda i, j, k: (k, j))
    o_spec = _operand_spec({None: None, 'm': 'r', 'n': 'c'}[o_st], False, tm, tn, nmb, nnb, lambda i, j, k: (i, j))
    dn = (((0 if ta else 1,), (1 if tb else 0,)), ((), ()))
    ins, specs = [a, b], [a_spec, b_spec]
    if extra is not None:
        ins.append(extra)
        specs.append(o_spec)
    if bias is not None:
        ins.append(bias)
        specs.append(pl.BlockSpec((1, tn), lambda i, j, k: (0, j)))
    n_in = len(ins)
    if o_st is None:
        o_shape = (M, N)
    elif o_st == 'm':
        o_shape = (ns, pm, N)
    else:
        o_shape = (ns, M, pn)

    def body(*refs):
        a_ref, b_ref = refs[0], refs[1]
        o_ref, acc = refs[n_in], refs[n_in + 1]
        k = pl.program_id(2)

        @pl.when(k == 0)
        def _():
            acc[...] = jnp.zeros_like(acc)

        av = a_ref[...]
        if a_tf is not None:
            av = a_tf(av.astype(F32))
        acc[...] += lax.dot_general(av.astype(BF16), b_ref[...].astype(BF16), dn, preferred_element_type=F32)

        @pl.when(k == gk - 1)
        def _():
            r = acc[...]
            pos = 2
            if extra is not None:
                r = epi(r, refs[pos][...].astype(F32))
                pos += 1
            elif epi is not None:
                r = epi(r)
            if bias is not None:
                r = r + refs[pos][...]
            o_ref[...] = r.astype(o_ref.dtype)

    return pl.pallas_call(
        body, grid=(gm, gn, gk), in_specs=specs, out_specs=o_spec,
        out_shape=jax.ShapeDtypeStruct(o_shape, out_dtype),
        scratch_shapes=[pltpu.VMEM((tm, tn), F32)], compiler_params=_params(3), name=name)(*ins)


def _relu2(v):
    r = jnp.maximum(v, 0.0)
    return r * r


def _silu(v):
    return v / (1.0 + jnp.exp(-v))


def _rb(tr, w):
    return pl.BlockSpec((tr, w), lambda i: (i, 0))


def _pb(r, w):
    return pl.BlockSpec((r, w), lambda i: (0, 0))


def _row_call(body, n_rows, tr, ins, in_specs, outs, out_specs, name):
    return pl.pallas_call(body, grid=(n_rows // tr,), in_specs=in_specs, out_specs=out_specs,
                          out_shape=outs, compiler_params=_params(1), name=name)(*ins)


def _accumulate(ref, value):
    @pl.when(pl.program_id(0) == 0)
    def _():
        ref[...] = jnp.zeros_like(ref)
    ref[...] += jnp.sum(value, axis=0, keepdims=True)


def _modulate(x, m6, row_shift, row_scale, name):
    S, D = x.shape
    tr = _tile(S, 256)

    def body(x_ref, m_ref, h_ref):
        sh = m_ref[row_shift:row_shift + 1, :]
        sc = m_ref[row_scale:row_scale + 1, :]
        h_ref[...] = (x_ref[...] * (1.0 + sc) + sh).astype(BF16)

    return _row_call(body, S, tr, [x, m6], [_rb(tr, D), _pb(6, D)],
                     jax.ShapeDtypeStruct((S, D), BF16), _rb(tr, D), name)


def _gelu(v):
    c = math.sqrt(2.0 / math.pi)
    return 0.5 * v * (1.0 + jnp.tanh(c * (v + 0.044715 * v * v * v)))


def _gelu_grad(v):
    c = math.sqrt(2.0 / math.pi)
    t = jnp.tanh(c * (v + 0.044715 * v * v * v))
    return 0.5 * (1.0 + t) + 0.5 * v * (1.0 - t * t) * c * (1.0 + 3.0 * 0.044715 * v * v)


def _ssm_act(u, y2, dskip):
    S, W = u.shape
    tr = _tile(S, 256)

    def body(u_ref, y0_ref, y1_ref, d_ref, pre_ref, g_ref):
        pre = u_ref[...] * d_ref[...] + y0_ref[...] + y1_ref[...]
        pre_ref[...] = pre
        g_ref[...] = _gelu(pre)

    ysp = lambda d: pl.BlockSpec((None, tr, W), lambda i: (d, i, 0))
    return _row_call(body, S, tr, [u, y2, y2, dskip], [_rb(tr, W), ysp(0), ysp(1), _pb(1, W)],
                     [jax.ShapeDtypeStruct((S, W), F32)] * 2, [_rb(tr, W)] * 2, "ssm_act")


def _merge(attn, g, z, b_glu, g_attn, g_ssm):
    S, W = attn.shape
    tr = _tile(S, 256)

    def body(a_ref, g_ref, z_ref, b_ref, ga_ref, gs_ref, o_ref):
        a = a_ref[...]
        ra = lax.rsqrt(jnp.mean(a * a, axis=-1, keepdims=True) + LN_EPS)
        o_ref[:, 0:W] = (a * ra * ga_ref[...]).astype(BF16)
        s = g_ref[...] / (1.0 + jnp.exp(-(z_ref[...] + b_ref[...])))
        rs = lax.rsqrt(jnp.mean(s * s, axis=-1, keepdims=True) + LN_EPS)
        o_ref[:, W:2 * W] = (s * rs * gs_ref[...]).astype(BF16)

    return _row_call(body, S, tr, [attn, g, z, b_glu, g_attn, g_ssm],
                     [_rb(tr, W)] * 3 + [_pb(1, W)] * 3,
                     jax.ShapeDtypeStruct((S, 2 * W), BF16), _rb(tr, 2 * W), "merge")


def _post_ln(x, branch, m6, row_gate, ln_g, ln_b, alpha, name):
    S, D = x.shape
    tr = _tile(S, 256)

    def body(x_ref, br_ref, m_ref, g_ref, b_ref, o_ref):
        gate = m_ref[row_gate:row_gate + 1, :]
        s = alpha * x_ref[...] + (1.0 + gate) * br_ref[...]
        mu = jnp.mean(s, axis=-1, keepdims=True)
        d = s - mu
        var = jnp.mean(d * d, axis=-1, keepdims=True)
        o_ref[...] = d * lax.rsqrt(var + LN_EPS) * g_ref[...] + b_ref[...]

    return _row_call(body, S, tr, [x, branch, m6, ln_g, ln_b],
                     [_rb(tr, D), _rb(tr, D), _pb(6, D), _pb(1, D), _pb(1, D)],
                     jax.ShapeDtypeStruct((S, D), F32), _rb(tr, D), name)


def _loss_head(x, target):
    S, D = x.shape
    tr = _tile(S, 256)

    def body(x_ref, t_ref, dx_ref, acc_ref):
        e = x_ref[...] - t_ref[...]
        dx_ref[...] = e * (1.0 / D)
        _accumulate(acc_ref, e * e)

    return _row_call(body, S, tr, [x, target], [_rb(tr, D)] * 2,
                     [jax.ShapeDtypeStruct((S, D), F32), jax.ShapeDtypeStruct((1, D), F32)],
                     [_rb(tr, D), _pb(1, D)], "loss_head")


def _ln_bwd(dxo, x_in, branch, m6, row_gate, ln_g, alpha, name):
    S, D = dxo.shape
    tr = _tile(S, 256)

    def body(dxo_ref, x_ref, br_ref, m_ref, g_ref, ds_ref, dbr_ref, dg_ref, db_ref, dgate_ref):
        gate = m_ref[row_gate:row_gate + 1, :]
        br = br_ref[...]
        s = alpha * x_ref[...] + (1.0 + gate) * br
        mu = jnp.mean(s, axis=-1, keepdims=True)
        d = s - mu
        var = jnp.mean(d * d, axis=-1, keepdims=True)
        rstd = lax.rsqrt(var + LN_EPS)
        xhat = d * rstd
        dxo = dxo_ref[...]
        dxh = dxo * g_ref[...]
        ds = rstd * (dxh - jnp.mean(dxh, axis=-1, keepdims=True)
                     - xhat * jnp.mean(dxh * xhat, axis=-1, keepdims=True))
        ds_ref[...] = ds
        dbr_ref[...] = ((1.0 + gate) * ds).astype(BF16)
        _accumulate(dg_ref, dxo * xhat)
        _accumulate(db_ref, dxo)
        _accumulate(dgate_ref, ds * br)

    vec = jax.ShapeDtypeStruct((1, D), F32)
    return _row_call(body, S, tr, [dxo, x_in, branch, m6, ln_g],
                     [_rb(tr, D)] * 3 + [_pb(6, D), _pb(1, D)],
                     [jax.ShapeDtypeStruct((S, D), F32), jax.ShapeDtypeStruct((S, D), BF16), vec, vec, vec],
                     [_rb(tr, D), _rb(tr, D), _pb(1, D), _pb(1, D), _pb(1, D)], name)


def _mod_bwd(ds, dh, x_in, m6, row_scale, alpha, name):
    S, D = ds.shape
    tr = _tile(S, 256)

    def body(ds_ref, dh_ref, x_ref, m_ref, dx_ref, dsc_ref, dsh_ref):
        sc = m_ref[row_scale:row_scale + 1, :]
        dh = dh_ref[...]
        dx_ref[...] = alpha * ds_ref[...] + dh * (1.0 + sc)
        _accumulate(dsc_ref, dh * x_ref[...])
        _accumulate(dsh_ref, dh)

    vec = jax.ShapeDtypeStruct((1, D), F32)
    return _row_call(body, S, tr, [ds, dh, x_in, m6], [_rb(tr, D)] * 3 + [_pb(6, D)],
                     [jax.ShapeDtypeStruct((S, D), F32), vec, vec],
                     [_rb(tr, D), _pb(1, D), _pb(1, D)], name)


def _merge_bwd(dmerged, attn, g, z, b_glu, g_attn, g_ssm):
    S, W = attn.shape
    tr = _tile(S, 256)

    def rms_bwd(x, gamma, dy):
        r = lax.rsqrt(jnp.mean(x * x, axis=-1, keepdims=True) + LN_EPS)
        gdy = gamma * dy
        dx = gdy * r - x * (r * r * r) * jnp.mean(gdy * x, axis=-1, keepdims=True)
        return dx, dy * x * r

    def body(dm_ref, a_ref, g_ref, z_ref, b_ref, ga_ref, gs_ref, da_ref, dgp_ref, dz_ref, dga_ref, dgs_ref, db_ref):
        da, dga = rms_bwd(a_ref[...], ga_ref[...], dm_ref[:, 0:W])
        da_ref[...] = da
        _accumulate(dga_ref, dga)
        gv = g_ref[...]
        sig = 1.0 / (1.0 + jnp.exp(-(z_ref[...] + b_ref[...])))
        dssm, dgs = rms_bwd(gv * sig, gs_ref[...], dm_ref[:, W:2 * W])
        _accumulate(dgs_ref, dgs)
        dgp_ref[...] = dssm * sig
        dz = dssm * gv * sig * (1.0 - sig)
        dz_ref[...] = dz.astype(BF16)
        _accumulate(db_ref, dz)

    vec = jax.ShapeDtypeStruct((1, W), F32)
    mat = jax.ShapeDtypeStruct((S, W), F32)
    return _row_call(body, S, tr, [dmerged, attn, g, z, b_glu, g_attn, g_ssm],
                     [_rb(tr, 2 * W)] + [_rb(tr, W)] * 3 + [_pb(1, W)] * 3,
                     [mat, mat, jax.ShapeDtypeStruct((S, W), BF16), vec, vec, vec],
                     [_rb(tr, W)] * 3 + [_pb(1, W)] * 3, "merge_bwd")


def _act_bwd(dgp, dgl, pre, u, dskip):
    S, W = pre.shape
    tr = _tile(S, 256)

    def body(a_ref, b_ref, pre_ref, u_ref, d_ref, dy_ref, du_ref, dd_ref):
        dy = (a_ref[...] + b_ref[...]) * _gelu_grad(pre_ref[...])
        dy_ref[...] = dy
        du_ref[...] = dy * d_ref[...]
        _accumulate(dd_ref, dy * u_ref[...])

    mat = jax.ShapeDtypeStruct((S, W), F32)
    return _row_call(body, S, tr, [dgp, dgl, pre, u, dskip], [_rb(tr, W)] * 4 + [_pb(1, W)],
                     [mat, mat, jax.ShapeDtypeStruct((1, W), F32)], [_rb(tr, W)] * 2 + [_pb(1, W)], "act_bwd")


def _dproj(dq, dkT, dvT, du_skip, du2, reach):
    S, W = dq.shape
    tr = _tile(S, 256)
    lanes = 2 * HEAD_DIM
    n_pairs, nkb = W // lanes, tr // lanes
    assert tr % lanes == 0 and reach % tr == 0

    def body(dq_ref, dk_ref, dv_ref, a_ref, b_ref, c_ref, o_ref):
        o_ref[0] = dq_ref[...].astype(BF16)
        for h in range(n_pairs):
            for b in range(nkb):
                rows, cols = slice(b * lanes, (b + 1) * lanes), slice(h * lanes, (h + 1) * lanes)
                o_ref[1, rows, cols] = dk_ref[h, b].T.astype(BF16)
                o_ref[2, rows, cols] = dv_ref[h, b].T.astype(BF16)
        o_ref[3] = (a_ref[...] + b_ref[...] + c_ref[...]).astype(BF16)

    tsp = pl.BlockSpec((n_pairs, nkb, lanes, lanes), lambda i: (0, i + reach // tr, 0, 0))
    ysp = lambda d: pl.BlockSpec((None, tr, W), lambda i: (d, i, 0))
    return _row_call(body, S, tr, [dq, dkT, dvT, du_skip, du2, du2],
                     [_rb(tr, W), tsp, tsp, _rb(tr, W), ysp(0), ysp(1)],
                     jax.ShapeDtypeStruct((4, S, W), BF16), pl.BlockSpec((4, tr, W), lambda i: (0, i, 0)), "dproj")


def _attn_reach():
    return max(w // 2 for w, _ in ATTN_PATTERNS)


def _bias_table(tq, width, reach, head):
    i = lax.broadcasted_iota(jnp.int32, (tq, width), 0)
    j = lax.broadcasted_iota(jnp.int32, (tq, width), 1)
    ad = jnp.abs(j - i - reach)
    mult = jnp.zeros((tq, width), jnp.int32)
    for window, dil in ATTN_PATTERNS:
        assert dil & (dil - 1) == 0
        mult += ((jnp.bitwise_and(ad, dil - 1) == 0) & (ad <= window // 2)).astype(jnp.int32)
    logm = jnp.zeros((tq, width), F32)
    for n in range(2, len(ATTN_PATTERNS) + 1):
        logm = jnp.where(mult == n, math.log(n), logm)
    return logm, ad.astype(F32), mult > 0


def _attn_common(n_heads, q_ref, k_ref, v_ref, bias_ref, tq, reach, seq):
    width = tq + 2 * reach
    hp, qi = pl.program_id(0), pl.program_id(1)

    @pl.when(qi == 0)
    def _():
        logm, ad, ok = _bias_table(tq, width, reach, None)
        for a in range(2):
            head = (2 * hp + a + 1).astype(F32)
            slope = jnp.exp(jnp.full((tq, width), -8.0 * math.log(2.0) / n_heads, F32) * head)
            bias_ref[a] = jnp.where(ok, logm - slope * ad, NEG_BIG)

    start = pl.multiple_of(qi * tq, tq)
    kw = k_ref[pl.ds(start, width), :]
    vw = v_ref[pl.ds(start, width), :]
    kpos = qi * tq - reach + lax.broadcasted_iota(jnp.int32, (1, width), 1)
    vbias = jnp.where((kpos >= 0) & (kpos < seq), 0.0, NEG_BIG).astype(F32)
    lane = lax.broadcasted_iota(jnp.int32, (1, 2 * HEAD_DIM), 1)
    masks = [lane < HEAD_DIM, lane >= HEAD_DIM]
    return start, kw, vw, vbias, masks


_NT = (((1,), (1,)), ((), ()))


def _attention_fwd(qkv, kp, vp, n_heads, tq):
    _, S, W = qkv.shape
    reach = _attn_reach()
    width = tq + 2 * reach
    Sp = S + 2 * reach
    scale = HEAD_DIM ** -0.5
    lanes = 2 * HEAD_DIM

    def body(q_ref, k_ref, v_ref, o_ref, lse_ref, bias_ref):
        _, kw, vw, vbias, masks = _attn_common(n_heads, q_ref, k_ref, v_ref, bias_ref, tq, reach, S)
        q = q_ref[...]
        out = jnp.zeros((tq, lanes), F32)
        lse = jnp.zeros((tq, lanes), F32)
        for a in range(2):
            qa = jnp.where(masks[a], q, jnp.zeros_like(q))
            s = lax.dot_general(qa, kw, _NT, preferred_element_type=F32) * scale + bias_ref[a] + vbias
            m = jnp.max(s, axis=-1, keepdims=True)
            p = jnp.exp(s - m)
            l = jnp.sum(p, axis=-1, keepdims=True)
            o = jnp.dot(p.astype(BF16), vw, preferred_element_type=F32) / l
            out = jnp.where(masks[a], o, out)
            lse = jnp.where(masks[a], m + jnp.log(l), lse)
        o_ref[...] = out
        lse_ref[...] = lse

    qsp = pl.BlockSpec((None, tq, lanes), lambda h, i: (0, i, h))
    ksp = pl.BlockSpec((Sp, lanes), lambda h, i: (0, h))
    osp = pl.BlockSpec((tq, lanes), lambda h, i: (i, h))
    mat = jax.ShapeDtypeStruct((S, W), F32)
    return pl.pallas_call(body, grid=(W // lanes, S // tq), in_specs=[qsp, ksp, ksp], out_specs=[osp, osp],
                          out_shape=[mat, mat], scratch_shapes=[pltpu.VMEM((2, tq, width), F32)],
                          compiler_params=_params(2), name="attn_fwd")(qkv, kp, vp)


def _attention_bwd(qkv, kp, vp, out, lse, dout, n_heads, tq):
    _, S, W = qkv.shape
    reach = _attn_reach()
    width = tq + 2 * reach
    Sp = S + 2 * reach
    scale = HEAD_DIM ** -0.5
    lanes = 2 * HEAD_DIM

    def body(q_ref, k_ref, v_ref, o_ref, lse_ref, do_ref, dq_ref, dk_ref, dv_ref, bias_ref):
        start, kw, vw, vbias, masks = _attn_common(n_heads, q_ref, k_ref, v_ref, bias_ref, tq, reach, S)

        @pl.when(pl.program_id(1) == 0)
        def _():
            dk_ref[...] = jnp.zeros_like(dk_ref)
            dv_ref[...] = jnp.zeros_like(dv_ref)

        q = q_ref[...]
        do = do_ref[...]
        prod = do * o_ref[...]
        lse_all = lse_ref[...]
        dq = jnp.zeros((tq, lanes), F32)
        dkt = jnp.zeros((lanes, width), F32)
        dvt = jnp.zeros((lanes, width), F32)
        for a in range(2):
            qa = jnp.where(masks[a], q, jnp.zeros_like(q))
            doa = jnp.where(masks[a], do, 0.0)
            s = lax.dot_general(qa, kw, _NT, preferred_element_type=F32) * scale + bias_ref[a] + vbias
            lse_a = lse_all[:, a * HEAD_DIM:a * HEAD_DIM + 1]
            p = jnp.exp(s - lse_a)
            delta = jnp.sum(jnp.where(masks[a], prod, 0.0), axis=-1, keepdims=True)
            dp = lax.dot_general(doa.astype(BF16), vw, _NT, preferred_element_type=F32)
            ds = (p * (dp - delta) * scale).astype(BF16)
            dvt += jnp.dot(doa.T.astype(BF16), p.astype(BF16), preferred_element_type=F32)
            dkt += jnp.dot(qa.astype(F32).T.astype(BF16), ds, preferred_element_type=F32)
            dq = jnp.where(masks[a], jnp.dot(ds, kw, preferred_element_type=F32), dq)
        dq_ref[...] = dq
        first = pl.program_id(1) * (tq // lanes)
        for j in range(width // lanes):
            dk_ref[first + j] += dkt[:, j * lanes:(j + 1) * lanes]
            dv_ref[first + j] += dvt[:, j * lanes:(j + 1) * lanes]

    assert tq % lanes == 0 and reach % lanes == 0
    qsp = pl.BlockSpec((None, tq, lanes), lambda h, i: (0, i, h))
    ksp = pl.BlockSpec((Sp, lanes), lambda h, i: (0, h))
    osp = pl.BlockSpec((tq, lanes), lambda h, i: (i, h))
    tsp = pl.BlockSpec((None, Sp // lanes, lanes, lanes), lambda h, i: (h, 0, 0, 0))
    tiles = jax.ShapeDtypeStruct((W // lanes, Sp // lanes, lanes, lanes), F32)
    return pl.pallas_call(body, grid=(W // lanes, S // tq), in_specs=[qsp, ksp, ksp, osp, osp, osp],
                          out_specs=[osp, tsp, tsp],
                          out_shape=[jax.ShapeDtypeStruct((S, W), F32), tiles, tiles],
                          scratch_shapes=[pltpu.VMEM((2, tq, width), F32)],
                          compiler_params=_params(2), name="attn_bwd")(qkv, kp, vp, out, lse, dout)


SB = GROUPS_PER_BLOCK * SSM_STATE
CB = GROUPS_PER_BLOCK * SSM_GROUP


def _s5_specs(nb, T, W, N, NK):
    blk = lambda d, i: jnp.where(d == 0, i, nb - 1 - i)
    usp = pl.BlockSpec((T, W), lambda d, i: (blk(d, i), 0))
    ysp = pl.BlockSpec((None, T, W), lambda d, i: (d, blk(d, i), 0))
    asp = pl.BlockSpec((None, 2, N), lambda d, i: (d, 0, 0))
    wsp = pl.BlockSpec((None, NK, CB, SB), lambda d, i: (d, 0, 0, 0))
    csp = pl.BlockSpec((None, None, 2, N), lambda d, i: (d, blk(d, i), 0, 0))
    return usp, ysp, asp, wsp, csp


def _s5_fwd(u, a, b_re, b_im, c_re, c_im, T):
    S, W = u.shape
    NK = W // CB
    N = NK * SB
    nb = S // T
    usp, ysp, asp, wsp, csp = _s5_specs(nb, T, W, N, NK)

    def body(u_ref, a_ref, bre_ref, bim_ref, cre_ref, cim_ref, y_ref, car_ref, wr, wi, st):
        d, i = pl.program_id(0), pl.program_id(1)

        @pl.when(i == 0)
        def _():
            st[...] = jnp.zeros_like(st)

        car_ref[...] = st[0:2, :]
        for k in range(NK):
            ls = slice(k * SB, (k + 1) * SB)
            uk = u_ref[:, k * CB:(k + 1) * CB].astype(BF16)
            wr[...] = jnp.dot(uk, bre_ref[k], preferred_element_type=F32)
            wi[...] = jnp.dot(uk, bim_ref[k], preferred_element_type=F32)
            ar, ai = a_ref[0:1, ls], a_ref[1:2, ls]

            def step(tt, carry):
                xr, xi = carry
                t = jnp.where(d == 0, tt, T - 1 - tt)
                nr = ar * xr - ai * xi + wr[pl.ds(t, 1), :]
                ni = ar * xi + ai * xr + wi[pl.ds(t, 1), :]
                wr[pl.ds(t, 1), :] = nr
                wi[pl.ds(t, 1), :] = ni
                return nr, ni

            xr, xi = lax.fori_loop(0, T, step, (st[0:1, ls], st[1:2, ls]))
            st[0:1, ls] = xr
            st[1:2, ls] = xi
            y_ref[:, k * CB:(k + 1) * CB] = (
                lax.dot_general(wr[...].astype(BF16), cre_ref[k], _NT, preferred_element_type=F32)
                - lax.dot_general(wi[...].astype(BF16), cim_ref[k], _NT, preferred_element_type=F32))

    return pl.pallas_call(
        body, grid=(2, nb), in_specs=[usp, asp, wsp, wsp, wsp, wsp], out_specs=[ysp, csp],
        out_shape=[jax.ShapeDtypeStruct((2, S, W), F32), jax.ShapeDtypeStruct((2, nb, 2, N), F32)],
        scratch_shapes=[pltpu.VMEM((T, SB), F32), pltpu.VMEM((T, SB), F32), pltpu.VMEM((8, N), F32)],
        compiler_params=_params(2), name="s5_fwd")(u, a, b_re, b_im, c_re, c_im)


def _s5_bwd(u, dy, a, b_re, b_im, c_re, c_im, carries, T):
    S, W = u.shape
    NK = W // CB
    N = NK * SB
    nb = S // T
    rblk = lambda d, i: jnp.where(d == 0, nb - 1 - i, i)
    usp = pl.BlockSpec((T, W), lambda d, i: (rblk(d, i), 0))
    ysp = pl.BlockSpec((None, T, W), lambda d, i: (d, rblk(d, i), 0))
    asp = pl.BlockSpec((None, 2, N), lambda d, i: (d, 0, 0))
    wsp = pl.BlockSpec((None, NK, CB, SB), lambda d, i: (d, 0, 0, 0))
    csp = pl.BlockSpec((None, None, 2, N), lambda d, i: (d, rblk(d, i), 0, 0))
    H = 8

    def body(u_ref, dy_ref, a_ref, bre_ref, bim_ref, cre_ref, cim_ref, car_ref,
             du_ref, dbre_ref, dbim_ref, dcre_ref, dcim_ref, da_ref, wr, wi, gr, gi, lam):
        d, i = pl.program_id(0), pl.program_id(1)

        @pl.when(i == 0)
        def _():
            lam[...] = jnp.zeros_like(lam)
            for r in (dbre_ref, dbim_ref, dcre_ref, dcim_ref, da_ref):
                r[...] = jnp.zeros_like(r)

        sgn = jnp.where(d == 0, 1, -1)
        for k in range(NK):
            ls = slice(k * SB, (k + 1) * SB)
            u32 = u_ref[:, k * CB:(k + 1) * CB]
            uk = u32.astype(BF16)
            dyk = dy_ref[:, k * CB:(k + 1) * CB]
            dyb = dyk.astype(BF16)
            wr[H:H + T, :] = jnp.dot(uk, bre_ref[k], preferred_element_type=F32)
            wi[H:H + T, :] = jnp.dot(uk, bim_ref[k], preferred_element_type=F32)
            ar, ai = a_ref[0:1, ls], a_ref[1:2, ls]
            x0r, x0i = car_ref[0:1, ls], car_ref[1:2, ls]
            wr[H - 1:H, :] = x0r
            wi[H - 1:H, :] = x0i
            wr[H + T:H + T + 1, :] = x0r
            wi[H + T:H + T + 1, :] = x0i

            def fstep(tt, carry):
                xr, xi = carry
                t = H + jnp.where(d == 0, tt, T - 1 - tt)
                nr = ar * xr - ai * xi + wr[pl.ds(t, 1), :]
                ni = ar * xi + ai * xr + wi[pl.ds(t, 1), :]
                wr[pl.ds(t, 1), :] = nr
                wi[pl.ds(t, 1), :] = ni
                return nr, ni

            lax.fori_loop(0, T, fstep, (x0r, x0i))

            gr[...] = jnp.dot(dyb, cre_ref[k], preferred_element_type=F32)
            gi[...] = -jnp.dot(dyb, cim_ref[k], preferred_element_type=F32)

            def bstep(tt, carry):
                lr, li, dar, dai = carry
                tl = jnp.where(d == 0, T - 1 - tt, tt)
                nr = gr[pl.ds(tl, 1), :] + ar * lr + ai * li
                ni = gi[pl.ds(tl, 1), :] + ar * li - ai * lr
                gr[pl.ds(tl, 1), :] = nr
                gi[pl.ds(tl, 1), :] = ni
                xpr = wr[pl.ds(H + tl - sgn, 1), :]
                xpi = wi[pl.ds(H + tl - sgn, 1), :]
                dar = dar + nr * xpr + ni * xpi
                dai = dai + ni * xpr - nr * xpi
                return nr, ni, dar, dai

            zero = jnp.zeros((1, SB), F32)
            lr, li, dar, dai = lax.fori_loop(0, T, bstep, (lam[0:1, ls], lam[1:2, ls], zero, zero))
            lam[0:1, ls] = lr
            lam[1:2, ls] = li
            da_ref[0:1, ls] += dar
            da_ref[1:2, ls] += dai

            lrb, lib = gr[...].astype(BF16), gi[...].astype(BF16)
            du_ref[:, k * CB:(k + 1) * CB] = (
                lax.dot_general(lrb, bre_ref[k], _NT, preferred_element_type=F32)
                + lax.dot_general(lib, bim_ref[k], _NT, preferred_element_type=F32))
            ut = u32.T.astype(BF16)
            dbre_ref[k] += jnp.dot(ut, lrb, preferred_element_type=F32)
            dbim_ref[k] += jnp.dot(ut, lib, preferred_element_type=F32)
            dyt = dyk.T.astype(BF16)
            dcre_ref[k] += jnp.dot(dyt, wr[H:H + T, :].astype(BF16), preferred_element_type=F32)
            dcim_ref[k] += jnp.dot(dyt, wi[H:H + T, :].astype(BF16), preferred_element_type=F32)

    blk = jax.ShapeDtypeStruct((2, NK, CB, SB), F32)
    return pl.pallas_call(
        body, grid=(2, nb), in_specs=[usp, usp, asp, wsp, wsp, wsp, wsp, csp],
        out_specs=[ysp, wsp, wsp, wsp, wsp, asp],
        out_shape=[jax.ShapeDtypeStruct((2, S, W), F32), blk, blk, blk, blk, jax.ShapeDtypeStruct((2, 2, N), F32)],
        scratch_shapes=[pltpu.VMEM((T + 16, SB), F32), pltpu.VMEM((T + 16, SB), F32),
                        pltpu.VMEM((T, SB), F32), pltpu.VMEM((T, SB), F32), pltpu.VMEM((8, N), F32)],
        compiler_params=_params(2), name="s5_bwd")(u, dy, a, b_re, b_im, c_re, c_im, carries)


def _s5_discretize(lam_re, lam_im, log_step, b_re, b_im):
    step = jnp.exp(log_step)[..., None]
    mag = jnp.exp(lam_re * step)
    a_re, a_im = mag * jnp.cos(lam_im * step), mag * jnp.sin(lam_im * step)
    den = lam_re * lam_re + lam_im * lam_im
    coef_re = ((a_re - 1.0) * lam_re + a_im * lam_im) / den
    coef_im = (a_im * lam_re - (a_re - 1.0) * lam_im) / den
    bb_re = coef_re[..., None] * b_re - coef_im[..., None] * b_im
    bb_im = coef_re[..., None] * b_im + coef_im[..., None] * b_re
    return a_re, a_im, bb_re, bb_im


def _to_blocks(w_gcp):
    two, G, C, P = w_gcp.shape
    nk = G // GROUPS_PER_BLOCK
    x = w_gcp.reshape(two, nk, GROUPS_PER_BLOCK, C, P)
    eye = jnp.eye(GROUPS_PER_BLOCK, dtype=w_gcp.dtype)
    return jnp.einsum('dkgcp,gh->dkgchp', x, eye).reshape(two, nk, GROUPS_PER_BLOCK * C, GROUPS_PER_BLOCK * P)


def _from_blocks(blk):
    two, nk, cb, sb = blk.shape
    C, P = cb // GROUPS_PER_BLOCK, sb // GROUPS_PER_BLOCK
    x = blk.reshape(two, nk, GROUPS_PER_BLOCK, C, GROUPS_PER_BLOCK, P)
    eye = jnp.eye(GROUPS_PER_BLOCK, dtype=blk.dtype)
    return jnp.einsum('dkgchp,gh->dkgcp', x, eye).reshape(two, nk * GROUPS_PER_BLOCK, C, P)


def _me():
    return lax.axis_index("x"), lax.axis_index("y"), lax.axis_index("c")


def _peer(k):
    x, y, c = _me()
    return (1 - x if k & 4 else x, 1 - y if k & 2 else y, 1 - c if k & 1 else c)


def _logical(dev):
    return 4 * dev[0] + 2 * dev[1] + dev[2]


def _exchange(name, ks, src_of, dst_of, out_shape, ins, local=None):
    n = len(ks)
    n_in = len(ins)

    def body(*refs):
        in_refs, out_ref = refs[:n_in], refs[n_in]
        send, recv, lsem = refs[n_in + 1:]
        copies = [pltpu.make_async_remote_copy(
            src_ref=src_of(k, in_refs, out_ref), dst_ref=dst_of(k, in_refs, out_ref), send_sem=send.at[j],
            recv_sem=recv.at[j], device_id=_peer(k), device_id_type=MESH) for j, k in enumerate(ks)]
        mine = None
        if local is not None:
            mine = pltpu.make_async_copy(*local(in_refs, out_ref), lsem)
            mine.start()
        for cp in copies:
            cp.start()
        for cp in copies:
            cp.wait_recv()
        for cp in copies:
            cp.wait_send()
        if mine is not None:
            mine.wait()

    return pl.pallas_call(
        body, in_specs=[ANY] * n_in, out_specs=ANY, out_shape=out_shape,
        scratch_shapes=[pltpu.SemaphoreType.DMA((n,)), pltpu.SemaphoreType.DMA((n,)), pltpu.SemaphoreType.DMA],
        name=name)(*ins)


def _all_gather8(v, name):
    out = jax.ShapeDtypeStruct((N_DEV,) + v.shape, v.dtype)
    slot = lambda k, ins, o: o.at[_logical(_me())]
    return _exchange(name, list(range(1, 8)), lambda k, ins, o: ins[0], slot, out, [v],
                     local=lambda ins, o: (ins[0], o.at[_logical(_me())]))


def _all_gather_xy(v, name):
    out = jax.ShapeDtypeStruct((4,) + v.shape, v.dtype)
    mine = lambda: 2 * lax.axis_index("x") + lax.axis_index("y")
    return _exchange(name, [2, 4, 6], lambda k, ins, o: ins[0], lambda k, ins, o: o.at[mine()], out, [v],
                     local=lambda ins, o: (ins[0], o.at[mine()]))


def _scatter8(g, name):
    two, n, four, R, C = g.shape
    out = jax.ShapeDtypeStruct((N_DEV, n, R, C), g.dtype)

    def piece(dev, ins):
        return ins[0].at[dev[2], :, 2 * dev[0] + dev[1]]

    slot = lambda k, ins, o: o.at[_logical(_me())]
    return _exchange(name, list(range(1, 8)), lambda k, ins, o: piece(_peer(k), ins), slot, out, [g],
                     local=lambda ins, o: (piece(_me(), ins), o.at[_logical(_me())]))


def _swap_c(v, name):
    out = jax.ShapeDtypeStruct((2,) + v.shape, v.dtype)
    slot = lambda: lax.axis_index("c")
    return _exchange(name, [1], lambda k, ins, o: ins[0], lambda k, ins, o: o.at[slot()], out, [v],
                     local=lambda ins, o: (ins[0], o.at[slot()]))


def _sum_leading(v, name):
    n, R, C = v.shape
    tr = _tile(R, max(8, (1 << 19) // (C * n)))

    def body(v_ref, o_ref):
        acc = v_ref[0].astype(F32)
        for s in range(1, n):
            acc = acc + v_ref[s].astype(F32)
        o_ref[...] = acc

    return pl.pallas_call(body, grid=(R // tr,), in_specs=[pl.BlockSpec((n, tr, C), lambda i: (0, i, 0))],
                          out_specs=pl.BlockSpec((tr, C), lambda i: (i, 0)),
                          out_shape=jax.ShapeDtypeStruct((R, C), F32), compiler_params=_params(1), name=name)(v)


def _adamw(w, g, m, v, name):
    R, C = w.shape
    tr = _tile(R, max(8, (1 << 18) // C))
    c1 = 1.0 - ADAM_B1 ** ADAM_STEP
    c2 = 1.0 - ADAM_B2 ** ADAM_STEP

    def body(w_ref, g_ref, m_ref, v_ref, d_ref, nm_ref, nv_ref):
        gv = g_ref[...]
        nm = ADAM_B1 * m_ref[...] + (1.0 - ADAM_B1) * gv
        nv = ADAM_B2 * v_ref[...] + (1.0 - ADAM_B2) * (gv * gv)
        nm_ref[...] = nm
        nv_ref[...] = nv
        d_ref[...] = -ADAM_LR * ((nm / c1) / (jnp.sqrt(nv / c2) + ADAM_EPS) + ADAM_WD * w_ref[...])

    sp = pl.BlockSpec((tr, C), lambda i: (i, 0))
    return pl.pallas_call(body, grid=(R // tr,), in_specs=[sp] * 4, out_specs=[sp] * 3,
                          out_shape=[jax.ShapeDtypeStruct((R, C), F32)] * 3, compiler_params=_params(1),
                          name=name)(w, g, m, v)


def _forward_layer(x, lw, alpha, n_heads, tq, ts):
    S, D = x.shape
    W = D // 2
    reach = _attn_reach()
    m6 = lw['m6']
    gw = {n: _all_gather_xy(lw[n].astype(BF16), "gather_" + n) for n in BIG}
    FF = gw['w_mlp1'].shape[2] * 4

    h = _modulate(x, m6, 0, 1, "modulate1")
    qkv = _mm(h, gw['w_in'], M=S, N=3 * W, K=D, b_st='n', o_st='n', ns=3, out_dtype=BF16, name="proj_qkv")
    u = _mm(h, gw['w_in'][3], M=S, N=W, K=D, out_dtype=F32, name="proj_u")
    pad = ((reach, reach), (0, 0))
    kp, vp = jnp.pad(qkv[1], pad), jnp.pad(qkv[2], pad)
    attn, lse = _attention_fwd(qkv, kp, vp, n_heads, tq)
    y2, carries = _s5_fwd(u, lw['a'], lw['b_re'], lw['b_im'], lw['c_re'], lw['c_im'], ts)
    pre, g = _ssm_act(u, y2, lw['ssm_d'])
    z = _mm(g, gw['w_glu'].reshape(W, W), M=S, N=W, K=W, out_dtype=F32, name="glu")
    merged = _merge(attn, g, z, lw['b_glu'], lw['g_attn'], lw['g_ssm'])
    mix = _mm(merged, gw['w_out'].reshape(D, D), M=S, N=D, K=D, out_dtype=F32, name="out_proj")
    x1 = _post_ln(x, mix, m6, 2, lw['ln1_g'], lw['ln1_b'], alpha, "post_ln1")
    h2 = _modulate(x1, m6, 3, 4, "modulate2")
    a4 = _mm(h2, gw['w_mlp1'], M=S, N=FF, K=D, b_st='n', o_st='n', ns=4, out_dtype=F32, name="mlp1")
    ff = _mm(a4, gw['w_mlp2'], M=S, N=D, K=FF, a_st='k', b_st='k', ns=4, a_tf=_relu2, out_dtype=F32, name="mlp2")
    x2 = _post_ln(x1, ff, m6, 5, lw['ln2_g'], lw['ln2_b'], alpha, "post_ln2")
    saved = dict(x=x, h=h, qkv=qkv, u=u, attn=attn, lse=lse, carries=carries, pre=pre, g=g, z=z, merged=merged,
                 mix=mix, x1=x1, h2=h2, a4=a4, ff=ff, gw=gw)
    return x2, saved


def _backward_layer(dx2, lw, sv, alpha, n_heads, tq, ts):
    S, D = dx2.shape
    W = D // 2
    reach = _attn_reach()
    m6, gw = lw['m6'], sv['gw']
    FF = gw['w_mlp1'].shape[2] * 4
    gr = {}

    ds2, dff, gr['ln2_g'], gr['ln2_b'], d_g2 = _ln_bwd(dx2, sv['x1'], sv['ff'], m6, 5, lw['ln2_g'], alpha, "ln2_bwd")
    da4 = _mm(dff, gw['w_mlp2'], M=S, N=FF, K=D, tb=True, b_st='n', o_st='n', ns=4, extra=sv['a4'],
              epi=lambda acc, a: 2.0 * jnp.maximum(a, 0.0) * acc, out_dtype=BF16, name="mlp2_bwd_x")
    gr['w_mlp2'] = _mm(sv['a4'], dff, M=FF, N=D, K=S, ta=True, a_st='m', o_st='m', ns=4, a_tf=_relu2,
                       out_dtype=BF16, name="mlp2_bwd_w")
    dh2 = _mm(da4, gw['w_mlp1'], M=S, N=D, K=FF, tb=True, a_st='k', b_st='k', ns=4, out_dtype=F32, name="mlp1_bwd_x")
    gr['w_mlp1'] = _mm(sv['h2'], da4, M=D, N=FF, K=S, ta=True, b_st='n', o_st='n', ns=4, out_dtype=BF16,
                       name="mlp1_bwd_w")
    dx1, d_sc2, d_sh2 = _mod_bwd(ds2, dh2, sv['x1'], m6, 4, alpha, "mod2_bwd")

    ds1, dmix, gr['ln1_g'], gr['ln1_b'], d_g1 = _ln_bwd(dx1, sv['x'], sv['mix'], m6, 2, lw['ln1_g'], alpha, "ln1_bwd")
    dmerged = _mm(dmix, gw['w_out'].reshape(D, D), M=S, N=D, K=D, tb=True, out_dtype=F32, name="out_proj_bwd_x")
    gr['w_out'] = _mm(sv['merged'], dmix, M=D, N=D, K=S, ta=True, out_dtype=BF16, name="out_proj_bwd_w")
    dattn, dgp, dz, gr['g_attn'], gr['g_ssm'], gr['b_glu'] = _merge_bwd(
        dmerged, sv['attn'], sv['g'], sv['z'], lw['b_glu'], lw['g_attn'], lw['g_ssm'])
    dgl = _mm(dz, gw['w_glu'].reshape(W, W), M=S, N=W, K=W, tb=True, out_dtype=F32, name="glu_bwd_x")
    gr['w_glu'] = _mm(sv['g'], dz, M=W, N=W, K=S, ta=True, out_dtype=BF16, name="glu_bwd_w")
    dy, du_skip, gr['ssm_d'] = _act_bwd(dgp, dgl, sv['pre'], sv['u'], lw['ssm_d'])
    du2, gr['bb_re'], gr['bb_im'], gr['cc_re'], gr['cc_im'], gr['a'] = _s5_bwd(
        sv['u'], dy, lw['a'], lw['b_re'], lw['b_im'], lw['c_re'], lw['c_im'], sv['carries'], ts)
    pad = ((reach, reach), (0, 0))
    kp, vp = jnp.pad(sv['qkv'][1], pad), jnp.pad(sv['qkv'][2], pad)
    dq, dkT, dvT = _attention_bwd(sv['qkv'], kp, vp, sv['attn'], sv['lse'], dattn, n_heads, tq)
    dproj = _dproj(dq, dkT, dvT, du_skip, du2, reach)
    dh = _mm(dproj, gw['w_in'], M=S, N=D, K=4 * W, tb=True, a_st='k', b_st='k', ns=4, out_dtype=F32, name="proj_bwd_x")
    gr['w_in'] = _mm(sv['h'], dproj, M=D, N=4 * W, K=S, ta=True, b_st='n', o_st='n', ns=4, out_dtype=BF16,
                     name="proj_bwd_w")
    dx, d_sc1, d_sh1 = _mod_bwd(ds1, dh, sv['x'], m6, 1, alpha, "mod1_bwd")
    gr['m6'] = jnp.concatenate([d_sh1, d_sc1, d_g1, d_sh2, d_sc2, d_g2], axis=0)
    return dx, gr


def _pack(arrays, width):
    flat = jnp.concatenate([a.reshape(-1) for a in arrays])
    n = flat.shape[0]
    rows = -(-n // (8 * width)) * 8
    return jnp.pad(flat, (0, rows * width - n)).reshape(rows, width), n


def _unpack(packed, like):
    flat = packed.reshape(-1)
    out, pos = [], 0
    for a in like:
        out.append(flat[pos:pos + a.size].reshape(a.shape))
        pos += a.size
    return out


def kernel(x, c, w_ada, b_ada, w_in, ssm_lam_re, ssm_lam_im, ssm_log_step, ssm_b_re, ssm_b_im, ssm_c_re, ssm_c_im, ssm_d, w_glu, b_glu, g_attn, g_ssm, w_out, ln1_g, ln1_b, w_mlp1, w_mlp2, ln2_g, ln2_b, loss_target, m_w_ada, m_b_ada, m_w_in, m_ssm_lam_re, m_ssm_lam_im, m_ssm_log_step, m_ssm_b_re, m_ssm_b_im, m_ssm_c_re, m_ssm_c_im, m_ssm_d, m_w_glu, m_b_glu, m_g_attn, m_g_ssm, m_w_out, m_ln1_g, m_ln1_b, m_w_mlp1, m_w_mlp2, m_ln2_g, m_ln2_b, v_w_ada, v_b_ada, v_w_in, v_ssm_lam_re, v_ssm_lam_im, v_ssm_log_step, v_ssm_b_re, v_ssm_b_im, v_ssm_c_re, v_ssm_c_im, v_ssm_d, v_w_glu, v_b_glu, v_g_attn, v_g_ssm, v_w_out, v_ln1_g, v_ln1_b, v_w_mlp1, v_w_mlp2, v_ln2_g, v_ln2_b):
    args = locals()
    w = {n: args[n] for n in WEIGHTS}
    mom = {n: args['m_' + n] for n in WEIGHTS}
    var = {n: args['v_' + n] for n in WEIGHTS}
    L, D, ada_cols = w_ada.shape
    S = x.shape[1]
    W = D // 2
    n_heads = W // HEAD_DIM
    alpha = (2 * L) ** 0.25
    tq = _tile(S, 128)
    ts = _tile(S, 128)
    xi, yi, ci = _me()
    shard = 2 * xi + yi
    me = 4 * xi + 2 * yi + ci

    c_all = _all_gather8(c, "gather_c").reshape(N_DEV, D)
    c16 = jnp.pad(c_all, ((0, 16 - N_DEV), (0, 0)))
    b_sh = lax.dynamic_slice_in_dim(b_ada, shard * ada_cols, ada_cols, axis=1)
    mods_sh = jnp.stack([_mm(c16, w_ada[l], M=16, N=ada_cols, K=D, a_tf=_silu, bias=b_sh[l:l + 1], out_dtype=F32,
                             name="ada") for l in range(L)])
    mods_all = _all_gather8(mods_sh, "gather_mods")
    mods = jnp.concatenate([mods_all[2 * j] for j in range(4)], axis=-1)
    m6_all = lax.dynamic_index_in_dim(mods, me, axis=1, keepdims=False).reshape(L, 6, D)

    (a_re, a_im, bb_re, bb_im), disc_vjp = jax.vjp(
        _s5_discretize, ssm_lam_re, ssm_lam_im, ssm_log_step, ssm_b_re, ssm_b_im)
    N = a_re.shape[2] * a_re.shape[3]
    per_layer = dict(
        m6=m6_all,
        a=jnp.stack([a_re.reshape(L, 2, N), a_im.reshape(L, 2, N)], axis=2),
        b_re=jax.vmap(_to_blocks)(jnp.swapaxes(bb_re, -1, -2)).astype(BF16),
        b_im=jax.vmap(_to_blocks)(jnp.swapaxes(bb_im, -1, -2)).astype(BF16),
        c_re=jax.vmap(_to_blocks)(ssm_c_re).astype(BF16),
        c_im=jax.vmap(_to_blocks)(ssm_c_im).astype(BF16),
        ssm_d=ssm_d[:, None], b_glu=b_glu[:, None], g_attn=g_attn[:, None], g_ssm=g_ssm[:, None],
        ln1_g=ln1_g[:, None], ln1_b=ln1_b[:, None], ln2_g=ln2_g[:, None], ln2_b=ln2_b[:, None],
        w_in=w_in, w_glu=w_glu, w_out=w_out, w_mlp1=w_mlp1, w_mlp2=w_mlp2)

    def fwd(xc, lw):
        return _forward_layer(xc, lw, alpha, n_heads, tq, ts)

    x_out, saved = lax.scan(fwd, x[0], per_layer)
    dx, sq = _loss_head(x_out, loss_target[0])
    loss = lax.psum(0.5 * jnp.sum(sq) / D, ("x", "y", "c"))

    def bwd(dxc, xs):
        lw, sv = xs
        return _backward_layer(dxc, lw, sv, alpha, n_heads, tq, ts)

    dx, gr = lax.scan(bwd, dx, (per_layer, saved), reverse=True)

    grads = {}
    for n in BIG:
        R, C = w[n].shape[1:]
        g = gr[n].reshape(2, L // 2, 4, R, C)
        land = _scatter8(g, "scatter_" + n)
        part = _sum_leading(land.reshape(N_DEV, (L // 2) * R, C), "reduce_" + n)
        grads[n] = _swap_c(part, "swap_" + n).reshape(w[n].shape)

    dm_all = _all_gather8(gr['m6'].reshape(L, 6 * D), "gather_dmods")
    grads['b_ada'] = _sum_leading(jnp.pad(dm_all, ((0, 0), (0, 8 - L), (0, 0))), "reduce_b_ada")[:L]
    dm16 = jnp.pad(lax.dynamic_slice_in_dim(dm_all, shard * ada_cols, ada_cols, axis=2), ((0, 16 - N_DEV), (0, 0), (0, 0)))
    grads['w_ada'] = jnp.stack([_mm(c16, dm16[:, l], M=D, N=ada_cols, K=16, ta=True, a_tf=_silu, out_dtype=F32,
                                    name="ada_bwd_w") for l in range(L)])

    small_names = ['a', 'bb_re', 'bb_im', 'cc_re', 'cc_im', 'ssm_d', 'b_glu', 'g_attn', 'g_ssm',
                   'ln1_g', 'ln1_b', 'ln2_g', 'ln2_b']
    for n in ['bb_re', 'bb_im', 'cc_re', 'cc_im']:
        gr[n] = jax.vmap(_from_blocks)(gr[n])
    packed, _ = _pack([gr[n] for n in small_names], 1024)
    total = _sum_leading(_all_gather8(packed, "gather_small"), "reduce_small")
    red = dict(zip(small_names, _unpack(total, [gr[n] for n in small_names])))
    G, P = ssm_lam_re.shape[2], ssm_lam_re.shape[3]
    d_a = red['a'].reshape(L, 2, 2, G, P)
    (grads['ssm_lam_re'], grads['ssm_lam_im'], grads['ssm_log_step'], grads['ssm_b_re'],
     grads['ssm_b_im']) = disc_vjp((d_a[:, :, 0], d_a[:, :, 1], jnp.swapaxes(red['bb_re'], -1, -2),
                                    jnp.swapaxes(red['bb_im'], -1, -2)))
    grads['ssm_c_re'] = red['cc_re']
    grads['ssm_c_im'] = -red['cc_im']
    for n in ['ssm_d', 'b_glu', 'g_attn', 'g_ssm', 'ln1_g', 'ln1_b', 'ln2_g', 'ln2_b']:
        grads[n] = red[n].reshape(w[n].shape)

    delta, new_m, new_v = {}, {}, {}
    for n in ['w_ada'] + BIG:
        two_d = lambda t: t.reshape(-1, t.shape[-1])
        d_, m_, v_ = _adamw(two_d(w[n]), two_d(grads[n]), two_d(mom[n]), two_d(var[n]), "adamw_" + n)
        delta[n], new_m[n], new_v[n] = d_.reshape(w[n].shape), m_.reshape(w[n].shape), v_.reshape(w[n].shape)
    pk = lambda d: _pack([d[n] for n in SMALL], 1024)[0]
    d_, m_, v_ = _adamw(pk(w), pk(grads), pk(mom), pk(var), "adamw_small")
    like = [w[n] for n in SMALL]
    for n, dn, mn, vn in zip(SMALL, _unpack(d_, like), _unpack(m_, like), _unpack(v_, like)):
        delta[n], new_m[n], new_v[n] = dn, mn, vn

    return (loss, dx[None], *[grads[n] for n in WEIGHTS], *[delta[n] for n in WEIGHTS],
            *[new_m[n] for n in WEIGHTS], *[new_v[n] for n in WEIGHTS])
```

```python
import functools
import math

import jax
import jax.numpy as jnp
from jax import lax
from jax.experimental import pallas as pl
from jax.experimental.pallas import tpu as pltpu

F32 = jnp.float32
BF16 = jnp.bfloat16
MESH = pl.DeviceIdType.MESH
ANY = pl.BlockSpec(memory_space=pl.ANY)

HEAD_DIM = 64
SSM_GROUP = 16
SSM_STATE = 64
GROUPS_PER_BLOCK = 8
ATTN_PATTERNS = ((128, 1), (512, 4), (2048, 16))
LN_EPS = 1e-5
NEG_BIG = -1e30
ADAM_LR, ADAM_B1, ADAM_B2, ADAM_EPS, ADAM_WD, ADAM_STEP = 0.001, 0.9, 0.999, 1e-08, 0.01, 10
VMEM_LIMIT_V7X = 56 * 1024 * 1024
N_DEV = 8

WEIGHTS = ['w_ada', 'b_ada', 'w_in', 'ssm_lam_re', 'ssm_lam_im', 'ssm_log_step', 'ssm_b_re', 'ssm_b_im',
           'ssm_c_re', 'ssm_c_im', 'ssm_d', 'w_glu', 'b_glu', 'g_attn', 'g_ssm', 'w_out', 'ln1_g', 'ln1_b',
           'w_mlp1', 'w_mlp2', 'ln2_g', 'ln2_b']
BIG = ['w_in', 'w_glu', 'w_out', 'w_mlp1', 'w_mlp2']
SMALL = [n for n in WEIGHTS if n not in BIG and n != 'w_ada']


def _params(n_grid):
    return pltpu.CompilerParams(dimension_semantics=("arbitrary",) * n_grid, vmem_limit_bytes=VMEM_LIMIT_V7X)


def _tile(n, cap):
    t = 1 << (max(1, min(n, cap)).bit_length() - 1)
    while n % t:
        t //= 2
    return t


def _operand_spec(stack, transposed, tr, tc, nrb, ncb, pick):
    def index(i, j, k):
        r, c = pick(i, j, k)
        s = None
        if stack == 'r':
            s, r = r // nrb, r % nrb
        elif stack == 'c':
            s, c = c // ncb, c % ncb
        idx = (c, r) if transposed else (r, c)
        return idx if s is None else (s,) + idx
    blk = (tc, tr) if transposed else (tr, tc)
    return pl.BlockSpec(blk if stack is None else (None,) + blk, index)


def _mm(a, b, *, M, N, K, name, out_dtype, ta=False, tb=False, a_st=None, b_st=None, o_st=None, ns=1,
        a_tf=None, epi=None, extra=None, bias=None, cap_m=1024, cap_n=1024, cap_k=1024):
    pm = M // ns if 'm' in (a_st, o_st) else M
    pn = N // ns if 'n' in (b_st, o_st) else N
    pk = K // ns if 'k' in (a_st, b_st) else K
    tm, tn, tk = _tile(pm, cap_m), _tile(pn, cap_n), _tile(pk, cap_k)
    gm, gn, gk = M // tm, N // tn, K // tk
    nmb, nnb, nkb = pm // tm, pn // tn, pk // tk
    a_spec = _operand_spec({None: None, 'm': 'r', 'k': 'c'}[a_st], ta, tm, tk, nmb, nkb, lambda i, j, k: (i, k))
    b_spec = _operand_spec({None: None, 'k': 'r', 'n': 'c'}[b_st], tb, tk, tn, nkb, nnb, lambda i, j, k: (k, j))
    o_spec = _operand_spec({None: None, 'm': 'r', 'n': 'c'}[o_st], False, tm, tn, nmb, nnb, lambda i, j, k: (i, j))
    dn = (((0 if ta else 1,), (1 if tb else 0,)), ((), ()))
    ins, specs = [a, b], [a_spec, b_spec]
    if extra is not None:
        ins.append(extra)
        specs.append(o_spec)
    if bias is not None:
        ins.append(bias)
        specs.append(pl.BlockSpec((1, tn), lambda i, j, k: (0, j)))
    n_in = len(ins)
    if o_st is None:
        o_shape = (M, N)
    elif o_st == 'm':
        o_shape = (ns, pm, N)
    else:
        o_shape = (ns, M, pn)

    def body(*refs):
        a_ref, b_ref = refs[0], refs[1]
        o_ref, acc = refs[n_in], refs[n_in + 1]
        k = pl.program_id(2)

        @pl.when(k == 0)
        def _():
            acc[...] = jnp.zeros_like(acc)

        av = a_ref[...]
        if a_tf is not None:
            av = a_tf(av.astype(F32))
        acc[...] += lax.dot_general(av.astype(BF16), b_ref[...].astype(BF16), dn, preferred_element_type=F32)

        @pl.when(k == gk - 1)
        def _():
            r = acc[...]
            pos = 2
            if extra is not None:
                r = epi(r, refs[pos][...].astype(F32))
                pos += 1
            elif epi is not None:
                r = epi(r)
            if bias is not None:
                r = r + refs[pos][...]
            o_ref[...] = r.astype(o_ref.dtype)

    return pl.pallas_call(
        body, grid=(gm, gn, gk), in_specs=specs, out_specs=o_spec,
        out_shape=jax.ShapeDtypeStruct(o_shape, out_dtype),
        scratch_shapes=[pltpu.VMEM((tm, tn), F32)], compiler_params=_params(3), name=name)(*ins)


def _relu2(v):
    r = jnp.maximum(v, 0.0)
    return r * r


def _silu(v):
    return v / (1.0 + jnp.exp(-v))


def _rb(tr, w):
    return pl.BlockSpec((tr, w), lambda i: (i, 0))


def _pb(r, w):
    return pl.BlockSpec((r, w), lambda i: (0, 0))


def _row_call(body, n_rows, tr, ins, in_specs, outs, out_specs, name):
    return pl.pallas_call(body, grid=(n_rows // tr,), in_specs=in_specs, out_specs=out_specs,
                          out_shape=outs, compiler_params=_params(1), name=name)(*ins)


def _accumulate(ref, value):
    @pl.when(pl.program_id(0) == 0)
    def _():
        ref[...] = jnp.zeros_like(ref)
    ref[...] += jnp.sum(value, axis=0, keepdims=True)


def _modulate(x, m6, row_shift, row_scale, name):
    S, D = x.shape
    tr = _tile(S, 256)

    def body(x_ref, m_ref, h_ref):
        sh = m_ref[row_shift:row_shift + 1, :]
        sc = m_ref[row_scale:row_scale + 1, :]
        h_ref[...] = (x_ref[...] * (1.0 + sc) + sh).astype(BF16)

    return _row_call(body, S, tr, [x, m6], [_rb(tr, D), _pb(6, D)],
                     jax.ShapeDtypeStruct((S, D), BF16), _rb(tr, D), name)


def _gelu(v):
    c = math.sqrt(2.0 / math.pi)
    return 0.5 * v * (1.0 + jnp.tanh(c * (v + 0.044715 * v * v * v)))


def _gelu_grad(v):
    c = math.sqrt(2.0 / math.pi)
    t = jnp.tanh(c * (v + 0.044715 * v * v * v))
    return 0.5 * (1.0 + t) + 0.5 * v * (1.0 - t * t) * c * (1.0 + 3.0 * 0.044715 * v * v)


def _ssm_act(u, y0, y1, dskip):
    S, W = u.shape
    tr = _tile(S, 256)

    def body(u_ref, y0_ref, y1_ref, d_ref, pre_ref, g_ref):
        pre = u_ref[...] * d_ref[...] + y0_ref[...] + y1_ref[...]
        pre_ref[...] = pre
        g_ref[...] = _gelu(pre)

    return _row_call(body, S, tr, [u, y0, y1, dskip], [_rb(tr, W)] * 3 + [_pb(1, W)],
                     [jax.ShapeDtypeStruct((S, W), F32)] * 2, [_rb(tr, W)] * 2, "ssm_act")


def _merge(attn, g, z, b_glu, g_attn, g_ssm):
    S, W = attn.shape
    tr = _tile(S, 256)

    def body(a_ref, g_ref, z_ref, b_ref, ga_ref, gs_ref, o_ref):
        a = a_ref[...]
        ra = lax.rsqrt(jnp.mean(a * a, axis=-1, keepdims=True) + LN_EPS)
        o_ref[:, 0:W] = (a * ra * ga_ref[...]).astype(BF16)
        s = g_ref[...] / (1.0 + jnp.exp(-(z_ref[...] + b_ref[...])))
        rs = lax.rsqrt(jnp.mean(s * s, axis=-1, keepdims=True) + LN_EPS)
        o_ref[:, W:2 * W] = (s * rs * gs_ref[...]).astype(BF16)

    return _row_call(body, S, tr, [attn, g, z, b_glu, g_attn, g_ssm],
                     [_rb(tr, W)] * 3 + [_pb(1, W)] * 3,
                     jax.ShapeDtypeStruct((S, 2 * W), BF16), _rb(tr, 2 * W), "merge")


def _post_ln(x, branch, m6, row_gate, ln_g, ln_b, alpha, name):
    S, D = x.shape
    tr = _tile(S, 256)

    def body(x_ref, br_ref, m_ref, g_ref, b_ref, o_ref):
        gate = m_ref[row_gate:row_gate + 1, :]
        s = alpha * x_ref[...] + (1.0 + gate) * br_ref[...]
        mu = jnp.mean(s, axis=-1, keepdims=True)
        d = s - mu
        var = jnp.mean(d * d, axis=-1, keepdims=True)
        o_ref[...] = d * lax.rsqrt(var + LN_EPS) * g_ref[...] + b_ref[...]

    return _row_call(body, S, tr, [x, branch, m6, ln_g, ln_b],
                     [_rb(tr, D), _rb(tr, D), _pb(6, D), _pb(1, D), _pb(1, D)],
                     jax.ShapeDtypeStruct((S, D), F32), _rb(tr, D), name)


def _loss_head(x, target):
    S, D = x.shape
    tr = _tile(S, 256)

    def body(x_ref, t_ref, dx_ref, acc_ref):
        e = x_ref[...] - t_ref[...]
        dx_ref[...] = e * (1.0 / D)
        _accumulate(acc_ref, e * e)

    return _row_call(body, S, tr, [x, target], [_rb(tr, D)] * 2,
                     [jax.ShapeDtypeStruct((S, D), F32), jax.ShapeDtypeStruct((1, D), F32)],
                     [_rb(tr, D), _pb(1, D)], "loss_head")


def _ln_bwd(dxo, x_in, branch, m6, row_gate, ln_g, alpha, name):
    S, D = dxo.shape
    tr = _tile(S, 256)

    def body(dxo_ref, x_ref, br_ref, m_ref, g_ref, ds_ref, dbr_ref, dg_ref, db_ref, dgate_ref):
        gate = m_ref[row_gate:row_gate + 1, :]
        br = br_ref[...]
        s = alpha * x_ref[...] + (1.0 + gate) * br
        mu = jnp.mean(s, axis=-1, keepdims=True)
        d = s - mu
        var = jnp.mean(d * d, axis=-1, keepdims=True)
        rstd = lax.rsqrt(var + LN_EPS)
        xhat = d * rstd
        dxo = dxo_ref[...]
        dxh = dxo * g_ref[...]
        ds = rstd * (dxh - jnp.mean(dxh, axis=-1, keepdims=True)
                     - xhat * jnp.mean(dxh * xhat, axis=-1, keepdims=True))
        ds_ref[...] = ds
        dbr_ref[...] = ((1.0 + gate) * ds).astype(BF16)
        _accumulate(dg_ref, dxo * xhat)
        _accumulate(db_ref, dxo)
        _accumulate(dgate_ref, ds * br)

    vec = jax.ShapeDtypeStruct((1, D), F32)
    return _row_call(body, S, tr, [dxo, x_in, branch, m6, ln_g],
                     [_rb(tr, D)] * 3 + [_pb(6, D), _pb(1, D)],
                     [jax.ShapeDtypeStruct((S, D), F32), jax.ShapeDtypeStruct((S, D), BF16), vec, vec, vec],
                     [_rb(tr, D), _rb(tr, D), _pb(1, D), _pb(1, D), _pb(1, D)], name)


def _mod_bwd(ds, dh, x_in, m6, row_scale, alpha, name):
    S, D = ds.shape
    tr = _tile(S, 256)

    def body(ds_ref, dh_ref, x_ref, m_ref, dx_ref, dsc_ref, dsh_ref):
        sc = m_ref[row_scale:row_scale + 1, :]
        dh = dh_ref[...]
        dx_ref[...] = alpha * ds_ref[...] + dh * (1.0 + sc)
        _accumulate(dsc_ref, dh * x_ref[...])
        _accumulate(dsh_ref, dh)

    vec = jax.ShapeDtypeStruct((1, D), F32)
    return _row_call(body, S, tr, [ds, dh, x_in, m6], [_rb(tr, D)] * 3 + [_pb(6, D)],
                     [jax.ShapeDtypeStruct((S, D), F32), vec, vec],
                     [_rb(tr, D), _pb(1, D), _pb(1, D)], name)


def _merge_bwd(dmerged, attn, g, z, b_glu, g_attn, g_ssm):
    S, W = attn.shape
    tr = _tile(S, 256)

    def rms_bwd(x, gamma, dy):
        r = lax.rsqrt(jnp.mean(x * x, axis=-1, keepdims=True) + LN_EPS)
        gdy = gamma * dy
        dx = gdy * r - x * (r * r * r) * jnp.mean(gdy * x, axis=-1, keepdims=True)
        return dx, dy * x * r

    def body(dm_ref, a_ref, g_ref, z_ref, b_ref, ga_ref, gs_ref, da_ref, dgp_ref, dz_ref, dga_ref, dgs_ref, db_ref):
        da, dga = rms_bwd(a_ref[...], ga_ref[...], dm_ref[:, 0:W])
        da_ref[...] = da
        _accumulate(dga_ref, dga)
        gv = g_ref[...]
        sig = 1.0 / (1.0 + jnp.exp(-(z_ref[...] + b_ref[...])))
        dssm, dgs = rms_bwd(gv * sig, gs_ref[...], dm_ref[:, W:2 * W])
        _accumulate(dgs_ref, dgs)
        dgp_ref[...] = dssm * sig
        dz = dssm * gv * sig * (1.0 - sig)
        dz_ref[...] = dz.astype(BF16)
        _accumulate(db_ref, dz)

    vec = jax.ShapeDtypeStruct((1, W), F32)
    mat = jax.ShapeDtypeStruct((S, W), F32)
    return _row_call(body, S, tr, [dmerged, attn, g, z, b_glu, g_attn, g_ssm],
                     [_rb(tr, 2 * W)] + [_rb(tr, W)] * 3 + [_pb(1, W)] * 3,
                     [mat, mat, jax.ShapeDtypeStruct((S, W), BF16), vec, vec, vec],
                     [_rb(tr, W)] * 3 + [_pb(1, W)] * 3, "merge_bwd")


def _act_bwd(dgp, dgl, pre, u, dskip):
    S, W = pre.shape
    tr = _tile(S, 256)

    def body(a_ref, b_ref, pre_ref, u_ref, d_ref, dy_ref, du_ref, dd_ref):
        dy = (a_ref[...] + b_ref[...]) * _gelu_grad(pre_ref[...])
        dy_ref[...] = dy
        du_ref[...] = dy * d_ref[...]
        _accumulate(dd_ref, dy * u_ref[...])

    mat = jax.ShapeDtypeStruct((S, W), F32)
    return _row_call(body, S, tr, [dgp, dgl, pre, u, dskip], [_rb(tr, W)] * 4 + [_pb(1, W)],
                     [mat, mat, jax.ShapeDtypeStruct((1, W), F32)], [_rb(tr, W)] * 2 + [_pb(1, W)], "act_bwd")


def _dproj(dq, dkT, dvT, du_skip, du0, du1, reach):
    S, W = dq.shape
    tr = _tile(S, 256)
    lanes = 2 * HEAD_DIM
    n_pairs, nkb = W // lanes, tr // lanes
    assert tr % lanes == 0 and reach % tr == 0

    def body(dq_ref, dk_ref, dv_ref, a_ref, b_ref, c_ref, o_ref):
        o_ref[0] = dq_ref[...].astype(BF16)
        for h in range(n_pairs):
            for b in range(nkb):
                rows, cols = slice(b * lanes, (b + 1) * lanes), slice(h * lanes, (h + 1) * lanes)
                o_ref[1, rows, cols] = dk_ref[h, b].T.astype(BF16)
                o_ref[2, rows, cols] = dv_ref[h, b].T.astype(BF16)
        o_ref[3] = (a_ref[...] + b_ref[...] + c_ref[...]).astype(BF16)

    tsp = pl.BlockSpec((n_pairs, nkb, lanes, lanes), lambda i: (0, i + reach // tr, 0, 0))
    return _row_call(body, S, tr, [dq, dkT, dvT, du_skip, du0, du1],
                     [_rb(tr, W), tsp, tsp, _rb(tr, W), _rb(tr, W), _rb(tr, W)],
                     jax.ShapeDtypeStruct((4, S, W), BF16), pl.BlockSpec((4, tr, W), lambda i: (0, i, 0)), "dproj")


def _attn_reach():
    return max(w // 2 for w, _ in ATTN_PATTERNS)


def _bias_table(tq, width, reach, head):
    i = lax.broadcasted_iota(jnp.int32, (tq, width), 0)
    j = lax.broadcasted_iota(jnp.int32, (tq, width), 1)
    ad = jnp.abs(j - i - reach)
    mult = jnp.zeros((tq, width), jnp.int32)
    for window, dil in ATTN_PATTERNS:
        assert dil & (dil - 1) == 0
        mult += ((jnp.bitwise_and(ad, dil - 1) == 0) & (ad <= window // 2)).astype(jnp.int32)
    logm = jnp.zeros((tq, width), F32)
    for n in range(2, len(ATTN_PATTERNS) + 1):
        logm = jnp.where(mult == n, math.log(n), logm)
    return logm, ad.astype(F32), mult > 0


def _attn_common(n_heads, q_ref, k_ref, v_ref, bias_ref, tq, reach, seq):
    width = tq + 2 * reach
    hp, qi = pl.program_id(0), pl.program_id(1)

    @pl.when(qi == 0)
    def _():
        logm, ad, ok = _bias_table(tq, width, reach, None)
        for a in range(2):
            head = (2 * hp + a + 1).astype(F32)
            slope = jnp.exp(jnp.full((tq, width), -8.0 * math.log(2.0) / n_heads, F32) * head)
            bias_ref[a] = jnp.where(ok, logm - slope * ad, NEG_BIG)

    start = pl.multiple_of(qi * tq, tq)
    kw = k_ref[pl.ds(start, width), :]
    vw = v_ref[pl.ds(start, width), :]
    kpos = qi * tq - reach + lax.broadcasted_iota(jnp.int32, (1, width), 1)
    vbias = jnp.where((kpos >= 0) & (kpos < seq), 0.0, NEG_BIG).astype(F32)
    lane = lax.broadcasted_iota(jnp.int32, (1, 2 * HEAD_DIM), 1)
    masks = [lane < HEAD_DIM, lane >= HEAD_DIM]
    return start, kw, vw, vbias, masks


_NT = (((1,), (1,)), ((), ()))


def _attention_fwd(qkv, kp, vp, n_heads, tq):
    _, S, W = qkv.shape
    reach = _attn_reach()
    width = tq + 2 * reach
    Sp = S + 2 * reach
    scale = HEAD_DIM ** -0.5
    lanes = 2 * HEAD_DIM

    def body(q_ref, k_ref, v_ref, o_ref, lse_ref, bias_ref):
        _, kw, vw, vbias, masks = _attn_common(n_heads, q_ref, k_ref, v_ref, bias_ref, tq, reach, S)
        q = q_ref[...]
        out = jnp.zeros((tq, lanes), F32)
        lse = jnp.zeros((tq, lanes), F32)
        for a in range(2):
            qa = jnp.where(masks[a], q, jnp.zeros_like(q))
            s = lax.dot_general(qa, kw, _NT, preferred_element_type=F32) * scale + bias_ref[a] + vbias
            m = jnp.max(s, axis=-1, keepdims=True)
            p = jnp.exp(s - m)
            l = jnp.sum(p, axis=-1, keepdims=True)
            o = jnp.dot(p.astype(BF16), vw, preferred_element_type=F32) / l
            out = jnp.where(masks[a], o, out)
            lse = jnp.where(masks[a], m + jnp.log(l), lse)
        o_ref[...] = out
        lse_ref[...] = lse

    qsp = pl.BlockSpec((None, tq, lanes), lambda h, i: (0, i, h))
    ksp = pl.BlockSpec((Sp, lanes), lambda h, i: (0, h))
    osp = pl.BlockSpec((tq, lanes), lambda h, i: (i, h))
    mat = jax.ShapeDtypeStruct((S, W), F32)
    return pl.pallas_call(body, grid=(W // lanes, S // tq), in_specs=[qsp, ksp, ksp], out_specs=[osp, osp],
                          out_shape=[mat, mat], scratch_shapes=[pltpu.VMEM((2, tq, width), F32)],
                          compiler_params=_params(2), name="attn_fwd")(qkv, kp, vp)


def _attention_bwd(qkv, kp, vp, out, lse, dout, n_heads, tq):
    _, S, W = qkv.shape
    reach = _attn_reach()
    width = tq + 2 * reach
    Sp = S + 2 * reach
    scale = HEAD_DIM ** -0.5
    lanes = 2 * HEAD_DIM

    def body(q_ref, k_ref, v_ref, o_ref, lse_ref, do_ref, dq_ref, dk_ref, dv_ref, bias_ref):
        start, kw, vw, vbias, masks = _attn_common(n_heads, q_ref, k_ref, v_ref, bias_ref, tq, reach, S)

        @pl.when(pl.program_id(1) == 0)
        def _():
            dk_ref[...] = jnp.zeros_like(dk_ref)
            dv_ref[...] = jnp.zeros_like(dv_ref)

        q = q_ref[...]
        do = do_ref[...]
        prod = do * o_ref[...]
        lse_all = lse_ref[...]
        dq = jnp.zeros((tq, lanes), F32)
        dkt = jnp.zeros((lanes, width), F32)
        dvt = jnp.zeros((lanes, width), F32)
        for a in range(2):
            qa = jnp.where(masks[a], q, jnp.zeros_like(q))
            doa = jnp.where(masks[a], do, 0.0)
            s = lax.dot_general(qa, kw, _NT, preferred_element_type=F32) * scale + bias_ref[a] + vbias
            lse_a = lse_all[:, a * HEAD_DIM:a * HEAD_DIM + 1]
            p = jnp.exp(s - lse_a)
            delta = jnp.sum(jnp.where(masks[a], prod, 0.0), axis=-1, keepdims=True)
            dp = lax.dot_general(doa.astype(BF16), vw, _NT, preferred_element_type=F32)
            ds = (p * (dp - delta) * scale).astype(BF16)
            dvt += jnp.dot(doa.T.astype(BF16), p.astype(BF16), preferred_element_type=F32)
            dkt += jnp.dot(qa.astype(F32).T.astype(BF16), ds, preferred_element_type=F32)
            dq = jnp.where(masks[a], jnp.dot(ds, kw, preferred_element_type=F32), dq)
        dq_ref[...] = dq
        first = pl.program_id(1) * (tq // lanes)
        for j in range(width // lanes):
            dk_ref[first + j] += dkt[:, j * lanes:(j + 1) * lanes]
            dv_ref[first + j] += dvt[:, j * lanes:(j + 1) * lanes]

    assert tq % lanes == 0 and reach % lanes == 0
    qsp = pl.BlockSpec((None, tq, lanes), lambda h, i: (0, i, h))
    ksp = pl.BlockSpec((Sp, lanes), lambda h, i: (0, h))
    osp = pl.BlockSpec((tq, lanes), lambda h, i: (i, h))
    tsp = pl.BlockSpec((None, Sp // lanes, lanes, lanes), lambda h, i: (h, 0, 0, 0))
    tiles = jax.ShapeDtypeStruct((W // lanes, Sp // lanes, lanes, lanes), F32)
    return pl.pallas_call(body, grid=(W // lanes, S // tq), in_specs=[qsp, ksp, ksp, osp, osp, osp],
                          out_specs=[osp, tsp, tsp],
                          out_shape=[jax.ShapeDtypeStruct((S, W), F32), tiles, tiles],
                          scratch_shapes=[pltpu.VMEM((2, tq, width), F32)],
                          compiler_params=_params(2), name="attn_bwd")(qkv, kp, vp, out, lse, dout)


SB = GROUPS_PER_BLOCK * SSM_STATE
CB = GROUPS_PER_BLOCK * SSM_GROUP
RUNS = 8
LANES = 128
NQ = SB // LANES


def _powers(pw, a_r, a_i, n):
    p_r, p_i = a_r, a_i
    for j in range(n):
        pw[0, j:j + 1, :] = p_r
        pw[1, j:j + 1, :] = p_i
        p_r, p_i = p_r * a_r - p_i * a_i, p_r * a_i + p_i * a_r


def _run_scan(br, bi, head, T, a_r, a_i, pw, e_r, e_i, ent, down, conj):
    n = T // RUNS
    sg = -1.0 if conj else 1.0
    rows = lambda j: pl.ds(head + j, RUNS, stride=n)
    out_r, out_i = [], []
    for q in range(NQ):
        ls = slice(q * LANES, (q + 1) * LANES)
        A_r = jnp.broadcast_to(a_r[:, ls], (RUNS, LANES))
        A_i = jnp.broadcast_to(sg * a_i[:, ls], (RUNS, LANES))
        x_r = x_i = None
        for j in (range(n - 1, -1, -1) if down else range(n)):
            b_r, b_i = br[q, rows(j), :], bi[q, rows(j), :]
            if x_r is None:
                x_r, x_i = b_r, b_i
            else:
                x_r, x_i = A_r * x_r - A_i * x_i + b_r, A_r * x_i + A_i * x_r + b_i
                br[q, rows(j), :] = x_r
                bi[q, rows(j), :] = x_i
        al_r, al_i = pw[0, n - 1:n, ls], sg * pw[1, n - 1:n, ls]
        s_r, s_i = e_r[:, ls], e_i[:, ls]
        for c in (range(RUNS - 1, -1, -1) if down else range(RUNS)):
            ent[0, c:c + 1, ls] = s_r
            ent[1, c:c + 1, ls] = s_i
            s_r, s_i = (al_r * s_r - al_i * s_i + x_r[c:c + 1, :], al_r * s_i + al_i * s_r + x_i[c:c + 1, :])
        E_r, E_i = ent[0, :, ls], ent[1, :, ls]
        for j in range(n):
            p = n - 1 - j if down else j
            p_r, p_i = pw[0, p:p + 1, ls], sg * pw[1, p:p + 1, ls]
            v_r, v_i = br[q, rows(j), :], bi[q, rows(j), :]
            br[q, rows(j), :] = v_r + p_r * E_r - p_i * E_i
            bi[q, rows(j), :] = v_i + p_r * E_i + p_i * E_r
        out_r.append(s_r)
        out_i.append(s_i)
    return jnp.concatenate(out_r, axis=1), jnp.concatenate(out_i, axis=1)


def _to_tiles(buf, head, T, value):
    for q in range(NQ):
        buf[q, head:head + T, :] = value[:, q * LANES:(q + 1) * LANES]


def _from_tiles(buf, head, T):
    return jnp.concatenate([buf[q, head:head + T, :] for q in range(NQ)], axis=1)


def _s5_specs(nb, T, rev, adjoint):
    flip = rev != adjoint
    blk = (lambda i: nb - 1 - i) if flip else (lambda i: i)
    usp = pl.BlockSpec((T, CB), lambda k, i: (blk(i), k))
    asp = pl.BlockSpec((None, 2, SB), lambda k, i: (k, 0, 0))
    wsp = pl.BlockSpec((None, CB, SB), lambda k, i: (k, 0, 0))
    csp = pl.BlockSpec((None, None, 2, SB), lambda k, i: (k, blk(i), 0, 0))
    return usp, asp, wsp, csp


def _s5_fwd(u, a, b_re, b_im, c_re, c_im, T, rev):
    S, W = u.shape
    NK = W // CB
    nb = S // T
    usp, asp, wsp, csp = _s5_specs(nb, T, rev, False)

    def body(u_ref, a_ref, bre_ref, bim_ref, cre_ref, cim_ref, y_ref, car_ref, wr, wi, st, pw, ent):
        a_r, a_i = a_ref[0:1, :], a_ref[1:2, :]

        @pl.when(pl.program_id(1) == 0)
        def _():
            st[...] = jnp.zeros_like(st)
            _powers(pw, a_r, a_i, T // RUNS)

        car_ref[...] = st[0:2, :]
        ub = u_ref[...].astype(BF16)
        _to_tiles(wr, 0, T, jnp.dot(ub, bre_ref[...], preferred_element_type=F32))
        _to_tiles(wi, 0, T, jnp.dot(ub, bim_ref[...], preferred_element_type=F32))
        s_r, s_i = _run_scan(wr, wi, 0, T, a_r, a_i, pw, st[0:1, :], st[1:2, :], ent, rev, False)
        st[0:1, :] = s_r
        st[1:2, :] = s_i
        x_r, x_i = _from_tiles(wr, 0, T).astype(BF16), _from_tiles(wi, 0, T).astype(BF16)
        y_ref[...] = (lax.dot_general(x_r, cre_ref[...], _NT, preferred_element_type=F32)
                      - lax.dot_general(x_i, cim_ref[...], _NT, preferred_element_type=F32))

    return pl.pallas_call(
        body, grid=(NK, nb), in_specs=[usp, asp, wsp, wsp, wsp, wsp], out_specs=[usp, csp],
        out_shape=[jax.ShapeDtypeStruct((S, W), F32), jax.ShapeDtypeStruct((NK, nb, 2, SB), F32)],
        scratch_shapes=[pltpu.VMEM((NQ, T, LANES), F32), pltpu.VMEM((NQ, T, LANES), F32), pltpu.VMEM((8, SB), F32),
                        pltpu.VMEM((2, T // RUNS, SB), F32), pltpu.VMEM((2, RUNS, SB), F32)],
        compiler_params=_params(2), name="s5_fwd_rev" if rev else "s5_fwd")(u, a, b_re, b_im, c_re, c_im)


def _s5_bwd(u, dy, a, b_re, b_im, c_re, c_im, carries, T, rev):
    S, W = u.shape
    NK = W // CB
    nb = S // T
    usp, asp, wsp, csp = _s5_specs(nb, T, rev, True)
    H = 8
    n = T // RUNS

    def body(u_ref, dy_ref, a_ref, bre_ref, bim_ref, cre_ref, cim_ref, car_ref,
             du_ref, dbre_ref, dbim_ref, dcre_ref, dcim_ref, da_ref, wr, wi, gr, gi, lam, pw, ent):
        a_r, a_i = a_ref[0:1, :], a_ref[1:2, :]

        @pl.when(pl.program_id(1) == 0)
        def _():
            lam[...] = jnp.zeros_like(lam)
            for r in (dbre_ref, dbim_ref, dcre_ref, dcim_ref, da_ref):
                r[...] = jnp.zeros_like(r)
            _powers(pw, a_r, a_i, n)

        u32 = u_ref[...]
        ub = u32.astype(BF16)
        dyk = dy_ref[...]
        dyb = dyk.astype(BF16)
        _to_tiles(wr, H, T, jnp.dot(ub, bre_ref[...], preferred_element_type=F32))
        _to_tiles(wi, H, T, jnp.dot(ub, bim_ref[...], preferred_element_type=F32))
        x0r, x0i = car_ref[0:1, :], car_ref[1:2, :]
        for row in (H - 1, H + T):
            _to_tiles(wr, row, 1, x0r)
            _to_tiles(wi, row, 1, x0i)
        _run_scan(wr, wi, H, T, a_r, a_i, pw, x0r, x0i, ent, rev, False)

        _to_tiles(gr, 0, T, jnp.dot(dyb, cre_ref[...], preferred_element_type=F32))
        _to_tiles(gi, 0, T, -jnp.dot(dyb, cim_ref[...], preferred_element_type=F32))
        l_r, l_i = _run_scan(gr, gi, 0, T, a_r, a_i, pw, lam[0:1, :], lam[1:2, :], ent, not rev, True)
        lam[0:1, :] = l_r
        lam[1:2, :] = l_i

        prev = H + (1 if rev else -1)
        for q in range(NQ):
            acc_r = jnp.zeros((RUNS, LANES), F32)
            acc_i = jnp.zeros((RUNS, LANES), F32)
            for j in range(n):
                g_r, g_i = gr[q, pl.ds(j, RUNS, stride=n), :], gi[q, pl.ds(j, RUNS, stride=n), :]
                p_r, p_i = wr[q, pl.ds(prev + j, RUNS, stride=n), :], wi[q, pl.ds(prev + j, RUNS, stride=n), :]
                acc_r += g_r * p_r + g_i * p_i
                acc_i += g_i * p_r - g_r * p_i
            ls = slice(q * LANES, (q + 1) * LANES)
            da_ref[0:1, ls] += jnp.sum(acc_r, axis=0, keepdims=True)
            da_ref[1:2, ls] += jnp.sum(acc_i, axis=0, keepdims=True)

        lrb, lib = _from_tiles(gr, 0, T).astype(BF16), _from_tiles(gi, 0, T).astype(BF16)
        du_ref[...] = (lax.dot_general(lrb, bre_ref[...], _NT, preferred_element_type=F32)
                       + lax.dot_general(lib, bim_ref[...], _NT, preferred_element_type=F32))
        ut = u32.T.astype(BF16)
        dbre_ref[...] += jnp.dot(ut, lrb, preferred_element_type=F32)
        dbim_ref[...] += jnp.dot(ut, lib, preferred_element_type=F32)
        dyt = dyk.T.astype(BF16)
        dcre_ref[...] += jnp.dot(dyt, _from_tiles(wr, H, T).astype(BF16), preferred_element_type=F32)
        dcim_ref[...] += jnp.dot(dyt, _from_tiles(wi, H, T).astype(BF16), preferred_element_type=F32)

    blk = jax.ShapeDtypeStruct((NK, CB, SB), F32)
    return pl.pallas_call(
        body, grid=(NK, nb), in_specs=[usp, usp, asp, wsp, wsp, wsp, wsp, csp],
        out_specs=[usp, wsp, wsp, wsp, wsp, asp],
        out_shape=[jax.ShapeDtypeStruct((S, W), F32), blk, blk, blk, blk, jax.ShapeDtypeStruct((NK, 2, SB), F32)],
        scratch_shapes=[pltpu.VMEM((NQ, T + 16, LANES), F32), pltpu.VMEM((NQ, T + 16, LANES), F32),
                        pltpu.VMEM((NQ, T, LANES), F32), pltpu.VMEM((NQ, T, LANES), F32), pltpu.VMEM((8, SB), F32),
                        pltpu.VMEM((2, n, SB), F32), pltpu.VMEM((2, RUNS, SB), F32)],
        compiler_params=_params(2), name="s5_bwd_rev" if rev else "s5_bwd")(
            u, dy, a, b_re, b_im, c_re, c_im, carries)


def _s5_discretize(lam_re, lam_im, log_step, b_re, b_im):
    step = jnp.exp(log_step)[..., None]
    mag = jnp.exp(lam_re * step)
    a_re, a_im = mag * jnp.cos(lam_im * step), mag * jnp.sin(lam_im * step)
    den = lam_re * lam_re + lam_im * lam_im
    coef_re = ((a_re - 1.0) * lam_re + a_im * lam_im) / den
    coef_im = (a_im * lam_re - (a_re - 1.0) * lam_im) / den
    bb_re = coef_re[..., None] * b_re - coef_im[..., None] * b_im
    bb_im = coef_re[..., None] * b_im + coef_im[..., None] * b_re
    return a_re, a_im, bb_re, bb_im


def _to_blocks(w_gcp):
    two, G, C, P = w_gcp.shape
    nk = G // GROUPS_PER_BLOCK
    x = w_gcp.reshape(two, nk, GROUPS_PER_BLOCK, C, P)
    eye = jnp.eye(GROUPS_PER_BLOCK, dtype=w_gcp.dtype)
    return jnp.einsum('dkgcp,gh->dkgchp', x, eye).reshape(two, nk, GROUPS_PER_BLOCK * C, GROUPS_PER_BLOCK * P)


def _from_blocks(blk):
    two, nk, cb, sb = blk.shape
    C, P = cb // GROUPS_PER_BLOCK, sb // GROUPS_PER_BLOCK
    x = blk.reshape(two, nk, GROUPS_PER_BLOCK, C, GROUPS_PER_BLOCK, P)
    eye = jnp.eye(GROUPS_PER_BLOCK, dtype=blk.dtype)
    return jnp.einsum('dkgchp,gh->dkgcp', x, eye).reshape(two, nk * GROUPS_PER_BLOCK, C, P)


def _me():
    return lax.axis_index("x"), lax.axis_index("y"), lax.axis_index("c")


def _peer(k):
    x, y, c = _me()
    return (1 - x if k & 4 else x, 1 - y if k & 2 else y, 1 - c if k & 1 else c)


def _logical(dev):
    return 4 * dev[0] + 2 * dev[1] + dev[2]


def _exchange(name, ks, src_of, dst_of, out_shape, ins, local=None, split=1):
    n = len(ks) * split
    n_in = len(ins)

    def body(*refs):
        in_refs, out_ref = refs[:n_in], refs[n_in]
        send, recv, lsem = refs[n_in + 1:]
        copies = []
        for k in ks:
            src, dst = src_of(k, in_refs, out_ref), dst_of(k, in_refs, out_ref)
            rows = src.shape[0] // split
            for q in range(split):
                part = pl.ds(q * rows, rows)
                j = len(copies)
                copies.append(pltpu.make_async_remote_copy(
                    src_ref=src if split == 1 else src.at[part], dst_ref=dst if split == 1 else dst.at[part],
                    send_sem=send.at[j], recv_sem=recv.at[j], device_id=_peer(k), device_id_type=MESH))
        mine = None
        if local is not None:
            mine = pltpu.make_async_copy(*local(in_refs, out_ref), lsem)
            mine.start()
        for cp in copies:
            cp.start()
        for cp in copies:
            cp.wait_recv()
        for cp in copies:
            cp.wait_send()
        if mine is not None:
            mine.wait()

    return pl.pallas_call(
        body, in_specs=[ANY] * n_in, out_specs=ANY, out_shape=out_shape,
        scratch_shapes=[pltpu.SemaphoreType.DMA((n,)), pltpu.SemaphoreType.DMA((n,)), pltpu.SemaphoreType.DMA],
        name=name)(*ins)


def _all_gather8(v, name):
    out = jax.ShapeDtypeStruct((N_DEV,) + v.shape, v.dtype)
    slot = lambda k, ins, o: o.at[_logical(_me())]
    return _exchange(name, list(range(1, 8)), lambda k, ins, o: ins[0], slot, out, [v],
                     local=lambda ins, o: (ins[0], o.at[_logical(_me())]))


def _all_gather_xy(v, name):
    R, C = v.shape
    half = lambda ins: ins[0].at[pl.ds(lax.axis_index("c") * (R // 2), R // 2)]
    chip = lambda: 2 * lax.axis_index("x") + lax.axis_index("y")
    halves = _exchange(name, [2, 4, 6], lambda k, ins, o: half(ins), lambda k, ins, o: o.at[chip()],
                       jax.ShapeDtypeStruct((4, R // 2, C), v.dtype), [v],
                       local=lambda ins, o: (half(ins), o.at[chip()]))
    core = lambda o: o.at[:, lax.axis_index("c")]
    both = _exchange(name + "_join", [1], lambda k, ins, o: ins[0], lambda k, ins, o: core(o),
                     jax.ShapeDtypeStruct((4, 2, R // 2, C), v.dtype), [halves],
                     local=lambda ins, o: (ins[0], core(o)), split=4)
    return both.reshape(4, R, C)


def _scatter8(g, name):
    two, n, four, R, C = g.shape
    out = jax.ShapeDtypeStruct((N_DEV, n, R, C), g.dtype)

    def piece(dev, ins):
        return ins[0].at[dev[2], :, 2 * dev[0] + dev[1]]

    slot = lambda k, ins, o: o.at[_logical(_me())]
    return _exchange(name, list(range(1, 8)), lambda k, ins, o: piece(_peer(k), ins), slot, out, [g],
                     local=lambda ins, o: (piece(_me(), ins), o.at[_logical(_me())]))


def _swap_c(v, name):
    out = jax.ShapeDtypeStruct((2,) + v.shape, v.dtype)
    slot = lambda: lax.axis_index("c")
    return _exchange(name, [1], lambda k, ins, o: ins[0], lambda k, ins, o: o.at[slot()], out, [v],
                     local=lambda ins, o: (ins[0], o.at[slot()]), split=8)


def _sum_leading(v, name):
    n, R, C = v.shape
    tr = _tile(R, max(8, (1 << 19) // (C * n)))

    def body(v_ref, o_ref):
        acc = v_ref[0].astype(F32)
        for s in range(1, n):
            acc = acc + v_ref[s].astype(F32)
        o_ref[...] = acc

    return pl.pallas_call(body, grid=(R // tr,), in_specs=[pl.BlockSpec((n, tr, C), lambda i: (0, i, 0))],
                          out_specs=pl.BlockSpec((tr, C), lambda i: (i, 0)),
                          out_shape=jax.ShapeDtypeStruct((R, C), F32), compiler_params=_params(1), name=name)(v)


def _adamw(w, g, m, v, name):
    R, C = w.shape
    tr = _tile(R, max(8, (1 << 18) // C))
    c1 = 1.0 - ADAM_B1 ** ADAM_STEP
    c2 = 1.0 - ADAM_B2 ** ADAM_STEP

    def body(w_ref, g_ref, m_ref, v_ref, d_ref, nm_ref, nv_ref):
        gv = g_ref[...]
        nm = ADAM_B1 * m_ref[...] + (1.0 - ADAM_B1) * gv
        nv = ADAM_B2 * v_ref[...] + (1.0 - ADAM_B2) * (gv * gv)
        nm_ref[...] = nm
        nv_ref[...] = nv
        d_ref[...] = -ADAM_LR * ((nm / c1) / (jnp.sqrt(nv / c2) + ADAM_EPS) + ADAM_WD * w_ref[...])

    sp = pl.BlockSpec((tr, C), lambda i: (i, 0))
    return pl.pallas_call(body, grid=(R // tr,), in_specs=[sp] * 4, out_specs=[sp] * 3,
                          out_shape=[jax.ShapeDtypeStruct((R, C), F32)] * 3, compiler_params=_params(1),
                          name=name)(w, g, m, v)


def _forward_layer(x, lw, alpha, n_heads, tq, ts):
    S, D = x.shape
    W = D // 2
    reach = _attn_reach()
    m6 = lw['m6']
    gw = {n: _all_gather_xy(lw[n].astype(BF16), "gather_" + n) for n in BIG}
    FF = gw['w_mlp1'].shape[2] * 4

    h = _modulate(x, m6, 0, 1, "modulate1")
    qkv = _mm(h, gw['w_in'], M=S, N=3 * W, K=D, b_st='n', o_st='n', ns=3, out_dtype=BF16, name="proj_qkv")
    u = _mm(h, gw['w_in'][3], M=S, N=W, K=D, out_dtype=F32, name="proj_u")
    pad = ((reach, reach), (0, 0))
    kp, vp = jnp.pad(qkv[1], pad), jnp.pad(qkv[2], pad)
    attn, lse = _attention_fwd(qkv, kp, vp, n_heads, tq)
    y, carries = zip(*[_s5_fwd(u, lw['a'][d], lw['b_re'][d], lw['b_im'][d], lw['c_re'][d], lw['c_im'][d], ts,
                               rev=bool(d)) for d in range(2)])
    pre, g = _ssm_act(u, y[0], y[1], lw['ssm_d'])
    z = _mm(g, gw['w_glu'].reshape(W, W), M=S, N=W, K=W, out_dtype=F32, name="glu")
    merged = _merge(attn, g, z, lw['b_glu'], lw['g_attn'], lw['g_ssm'])
    mix = _mm(merged, gw['w_out'].reshape(D, D), M=S, N=D, K=D, out_dtype=F32, name="out_proj")
    x1 = _post_ln(x, mix, m6, 2, lw['ln1_g'], lw['ln1_b'], alpha, "post_ln1")
    h2 = _modulate(x1, m6, 3, 4, "modulate2")
    a4 = _mm(h2, gw['w_mlp1'], M=S, N=FF, K=D, b_st='n', o_st='n', ns=4, out_dtype=F32, name="mlp1")
    ff = _mm(a4, gw['w_mlp2'], M=S, N=D, K=FF, a_st='k', b_st='k', ns=4, a_tf=_relu2, out_dtype=F32, name="mlp2")
    x2 = _post_ln(x1, ff, m6, 5, lw['ln2_g'], lw['ln2_b'], alpha, "post_ln2")
    saved = dict(x=x, h=h, qkv=qkv, u=u, attn=attn, lse=lse, carries=carries, pre=pre, g=g, z=z, merged=merged,
                 mix=mix, x1=x1, h2=h2, a4=a4, ff=ff, gw=gw)
    return x2, saved


def _backward_layer(dx2, lw, sv, alpha, n_heads, tq, ts):
    S, D = dx2.shape
    W = D // 2
    reach = _attn_reach()
    m6, gw = lw['m6'], sv['gw']
    FF = gw['w_mlp1'].shape[2] * 4
    gr = {}

    ds2, dff, gr['ln2_g'], gr['ln2_b'], d_g2 = _ln_bwd(dx2, sv['x1'], sv['ff'], m6, 5, lw['ln2_g'], alpha, "ln2_bwd")
    da4 = _mm(dff, gw['w_mlp2'], M=S, N=FF, K=D, tb=True, b_st='n', o_st='n', ns=4, extra=sv['a4'],
              epi=lambda acc, a: 2.0 * jnp.maximum(a, 0.0) * acc, out_dtype=BF16, name="mlp2_bwd_x")
    gr['w_mlp2'] = _mm(sv['a4'], dff, M=FF, N=D, K=S, ta=True, a_st='m', o_st='m', ns=4, a_tf=_relu2,
                       out_dtype=BF16, name="mlp2_bwd_w")
    dh2 = _mm(da4, gw['w_mlp1'], M=S, N=D, K=FF, tb=True, a_st='k', b_st='k', ns=4, out_dtype=F32, name="mlp1_bwd_x")
    gr['w_mlp1'] = _mm(sv['h2'], da4, M=D, N=FF, K=S, ta=True, b_st='n', o_st='n', ns=4, out_dtype=BF16,
                       name="mlp1_bwd_w")
    dx1, d_sc2, d_sh2 = _mod_bwd(ds2, dh2, sv['x1'], m6, 4, alpha, "mod2_bwd")

    ds1, dmix, gr['ln1_g'], gr['ln1_b'], d_g1 = _ln_bwd(dx1, sv['x'], sv['mix'], m6, 2, lw['ln1_g'], alpha, "ln1_bwd")
    dmerged = _mm(dmix, gw['w_out'].reshape(D, D), M=S, N=D, K=D, tb=True, out_dtype=F32, name="out_proj_bwd_x")
    gr['w_out'] = _mm(sv['merged'], dmix, M=D, N=D, K=S, ta=True, out_dtype=BF16, name="out_proj_bwd_w")
    dattn, dgp, dz, gr['g_attn'], gr['g_ssm'], gr['b_glu'] = _merge_bwd(
        dmerged, sv['attn'], sv['g'], sv['z'], lw['b_glu'], lw['g_attn'], lw['g_ssm'])
    dgl = _mm(dz, gw['w_glu'].reshape(W, W), M=S, N=W, K=W, tb=True, out_dtype=F32, name="glu_bwd_x")
    gr['w_glu'] = _mm(sv['g'], dz, M=W, N=W, K=S, ta=True, out_dtype=BF16, name="glu_bwd_w")
    dy, du_skip, gr['ssm_d'] = _act_bwd(dgp, dgl, sv['pre'], sv['u'], lw['ssm_d'])
    s5 = [_s5_bwd(sv['u'], dy, lw['a'][d], lw['b_re'][d], lw['b_im'][d], lw['c_re'][d], lw['c_im'][d],
                  sv['carries'][d], ts, rev=bool(d)) for d in range(2)]
    du = [s5[d][0] for d in range(2)]
    for pos, n in enumerate(['bb_re', 'bb_im', 'cc_re', 'cc_im', 'a']):
        gr[n] = jnp.stack([s5[d][pos + 1] for d in range(2)])
    pad = ((reach, reach), (0, 0))
    kp, vp = jnp.pad(sv['qkv'][1], pad), jnp.pad(sv['qkv'][2], pad)
    dq, dkT, dvT = _attention_bwd(sv['qkv'], kp, vp, sv['attn'], sv['lse'], dattn, n_heads, tq)
    dproj = _dproj(dq, dkT, dvT, du_skip, du[0], du[1], reach)
    dh = _mm(dproj, gw['w_in'], M=S, N=D, K=4 * W, tb=True, a_st='k', b_st='k', ns=4, out_dtype=F32, name="proj_bwd_x")
    gr['w_in'] = _mm(sv['h'], dproj, M=D, N=4 * W, K=S, ta=True, b_st='n', o_st='n', ns=4, out_dtype=BF16,
                     name="proj_bwd_w")
    dx, d_sc1, d_sh1 = _mod_bwd(ds1, dh, sv['x'], m6, 1, alpha, "mod1_bwd")
    gr['m6'] = jnp.concatenate([d_sh1, d_sc1, d_g1, d_sh2, d_sc2, d_g2], axis=0)
    return dx, gr


def _pack(arrays, width):
    flat = jnp.concatenate([a.reshape(-1) for a in arrays])
    n = flat.shape[0]
    rows = -(-n // (8 * width)) * 8
    return jnp.pad(flat, (0, rows * width - n)).reshape(rows, width), n


def _unpack(packed, like):
    flat = packed.reshape(-1)
    out, pos = [], 0
    for a in like:
        out.append(flat[pos:pos + a.size].reshape(a.shape))
        pos += a.size
    return out


def kernel(x, c, w_ada, b_ada, w_in, ssm_lam_re, ssm_lam_im, ssm_log_step, ssm_b_re, ssm_b_im, ssm_c_re, ssm_c_im, ssm_d, w_glu, b_glu, g_attn, g_ssm, w_out, ln1_g, ln1_b, w_mlp1, w_mlp2, ln2_g, ln2_b, loss_target, m_w_ada, m_b_ada, m_w_in, m_ssm_lam_re, m_ssm_lam_im, m_ssm_log_step, m_ssm_b_re, m_ssm_b_im, m_ssm_c_re, m_ssm_c_im, m_ssm_d, m_w_glu, m_b_glu, m_g_attn, m_g_ssm, m_w_out, m_ln1_g, m_ln1_b, m_w_mlp1, m_w_mlp2, m_ln2_g, m_ln2_b, v_w_ada, v_b_ada, v_w_in, v_ssm_lam_re, v_ssm_lam_im, v_ssm_log_step, v_ssm_b_re, v_ssm_b_im, v_ssm_c_re, v_ssm_c_im, v_ssm_d, v_w_glu, v_b_glu, v_g_attn, v_g_ssm, v_w_out, v_ln1_g, v_ln1_b, v_w_mlp1, v_w_mlp2, v_ln2_g, v_ln2_b):
    args = locals()
    w = {n: args[n] for n in WEIGHTS}
    mom = {n: args['m_' + n] for n in WEIGHTS}
    var = {n: args['v_' + n] for n in WEIGHTS}
    L, D, ada_cols = w_ada.shape
    S = x.shape[1]
    W = D // 2
    n_heads = W // HEAD_DIM
    alpha = (2 * L) ** 0.25
    tq = _tile(S, 256)
    ts = _tile(S, 256)
    xi, yi, ci = _me()
    shard = 2 * xi + yi
    me = 4 * xi + 2 * yi + ci

    c_all = _all_gather8(c, "gather_c").reshape(N_DEV, D)
    c16 = jnp.pad(c_all, ((0, 16 - N_DEV), (0, 0)))
    b_sh = lax.dynamic_slice_in_dim(b_ada, shard * ada_cols, ada_cols, axis=1)
    mods_sh = jnp.stack([_mm(c16, w_ada[l], M=16, N=ada_cols, K=D, a_tf=_silu, bias=b_sh[l:l + 1], out_dtype=F32,
                             name="ada") for l in range(L)])
    mods_all = _all_gather8(mods_sh, "gather_mods")
    mods = jnp.concatenate([mods_all[2 * j] for j in range(4)], axis=-1)
    m6_all = lax.dynamic_index_in_dim(mods, me, axis=1, keepdims=False).reshape(L, 6, D)

    (a_re, a_im, bb_re, bb_im), disc_vjp = jax.vjp(
        _s5_discretize, ssm_lam_re, ssm_lam_im, ssm_log_step, ssm_b_re, ssm_b_im)
    N = a_re.shape[2] * a_re.shape[3]
    per_layer = dict(
        m6=m6_all,
        a=jnp.stack([a_re.reshape(L, 2, N // SB, SB), a_im.reshape(L, 2, N // SB, SB)], axis=3),
        b_re=jax.vmap(_to_blocks)(jnp.swapaxes(bb_re, -1, -2)).astype(BF16),
        b_im=jax.vmap(_to_blocks)(jnp.swapaxes(bb_im, -1, -2)).astype(BF16),
        c_re=jax.vmap(_to_blocks)(ssm_c_re).astype(BF16),
        c_im=jax.vmap(_to_blocks)(ssm_c_im).astype(BF16),
        ssm_d=ssm_d[:, None], b_glu=b_glu[:, None], g_attn=g_attn[:, None], g_ssm=g_ssm[:, None],
        ln1_g=ln1_g[:, None], ln1_b=ln1_b[:, None], ln2_g=ln2_g[:, None], ln2_b=ln2_b[:, None],
        w_in=w_in, w_glu=w_glu, w_out=w_out, w_mlp1=w_mlp1, w_mlp2=w_mlp2)

    layers = [{n: v[l] for n, v in per_layer.items()} for l in range(L)]
    xc, saved = x[0], []
    for lw in layers:
        xc, sv = _forward_layer(xc, lw, alpha, n_heads, tq, ts)
        saved.append(sv)
    dx, sq = _loss_head(xc, loss_target[0])
    loss = lax.psum(0.5 * jnp.sum(sq) / D, ("x", "y", "c"))

    per_layer_grads = [None] * L
    for l in reversed(range(L)):
        dx, per_layer_grads[l] = _backward_layer(dx, layers[l], saved[l], alpha, n_heads, tq, ts)
    gr = {n: jnp.stack([g[n] for g in per_layer_grads]) for n in per_layer_grads[0]}

    grads = {}
    for n in BIG:
        R, C = w[n].shape[1:]
        g = gr[n].reshape(2, L // 2, 4, R, C)
        land = _scatter8(g, "scatter_" + n)
        part = _sum_leading(land.reshape(N_DEV, (L // 2) * R, C), "reduce_" + n)
        grads[n] = _swap_c(part, "swap_" + n).reshape(w[n].shape)

    dm_all = _all_gather8(gr['m6'].reshape(L, 6 * D), "gather_dmods")
    grads['b_ada'] = _sum_leading(jnp.pad(dm_all, ((0, 0), (0, 8 - L), (0, 0))), "reduce_b_ada")[:L]
    dm16 = jnp.pad(lax.dynamic_slice_in_dim(dm_all, shard * ada_cols, ada_cols, axis=2), ((0, 16 - N_DEV), (0, 0), (0, 0)))
    grads['w_ada'] = jnp.stack([_mm(c16, dm16[:, l], M=D, N=ada_cols, K=16, ta=True, a_tf=_silu, out_dtype=F32,
                                    name="ada_bwd_w") for l in range(L)])

    small_names = ['a', 'bb_re', 'bb_im', 'cc_re', 'cc_im', 'ssm_d', 'b_glu', 'g_attn', 'g_ssm',
                   'ln1_g', 'ln1_b', 'ln2_g', 'ln2_b']
    for n in ['bb_re', 'bb_im', 'cc_re', 'cc_im']:
        gr[n] = jax.vmap(_from_blocks)(gr[n])
    packed, _ = _pack([gr[n] for n in small_names], 1024)
    total = _sum_leading(_all_gather8(packed, "gather_small"), "reduce_small")
    red = dict(zip(small_names, _unpack(total, [gr[n] for n in small_names])))
    G, P = ssm_lam_re.shape[2], ssm_lam_re.shape[3]
    d_a = jnp.swapaxes(red['a'], 2, 3).reshape(L, 2, 2, G, P)
    (grads['ssm_lam_re'], grads['ssm_lam_im'], grads['ssm_log_step'], grads['ssm_b_re'],
     grads['ssm_b_im']) = disc_vjp((d_a[:, :, 0], d_a[:, :, 1], jnp.swapaxes(red['bb_re'], -1, -2),
                                    jnp.swapaxes(red['bb_im'], -1, -2)))
    grads['ssm_c_re'] = red['cc_re']
    grads['ssm_c_im'] = -red['cc_im']
    for n in ['ssm_d', 'b_glu', 'g_attn', 'g_ssm', 'ln1_g', 'ln1_b', 'ln2_g', 'ln2_b']:
        grads[n] = red[n].reshape(w[n].shape)

    delta, new_m, new_v = {}, {}, {}
    for n in ['w_ada'] + BIG:
        two_d = lambda t: t.reshape(-1, t.shape[-1])
        d_, m_, v_ = _adamw(two_d(w[n]), two_d(grads[n]), two_d(mom[n]), two_d(var[n]), "adamw_" + n)
        delta[n], new_m[n], new_v[n] = d_.reshape(w[n].shape), m_.reshape(w[n].shape), v_.reshape(w[n].shape)
    pk = lambda d: _pack([d[n] for n in SMALL], 1024)[0]
    d_, m_, v_ = _adamw(pk(w), pk(grads), pk(mom), pk(var), "adamw_small")
    like = [w[n] for n in SMALL]
    for n, dn, mn, vn in zip(SMALL, _unpack(d_, like), _unpack(m_, like), _unpack(v_, like)):
        delta[n], new_m[n], new_v[n] = dn, mn, vn

    return (loss, dx[None], *[grads[n] for n in WEIGHTS], *[delta[n] for n in WEIGHTS],
            *[new_m[n] for n in WEIGHTS], *[new_v[n] for n in WEIGHTS])
```

```python
import functools
import math

import jax
import jax.numpy as jnp
from jax import lax
from jax.experimental import pallas as pl
from jax.experimental.pallas import tpu as pltpu

F32 = jnp.float32
BF16 = jnp.bfloat16
MESH = pl.DeviceIdType.MESH
ANY = pl.BlockSpec(memory_space=pl.ANY)

HEAD_DIM = 64
SSM_GROUP = 16
SSM_STATE = 64
GROUPS_PER_BLOCK = 8
ATTN_PATTERNS = ((128, 1), (512, 4), (2048, 16))
LN_EPS = 1e-5
NEG_BIG = -1e30
ADAM_LR, ADAM_B1, ADAM_B2, ADAM_EPS, ADAM_WD, ADAM_STEP = 0.001, 0.9, 0.999, 1e-08, 0.01, 10
VMEM_LIMIT_V7X = 56 * 1024 * 1024
N_DEV = 8

WEIGHTS = ['w_ada', 'b_ada', 'w_in', 'ssm_lam_re', 'ssm_lam_im', 'ssm_log_step', 'ssm_b_re', 'ssm_b_im',
           'ssm_c_re', 'ssm_c_im', 'ssm_d', 'w_glu', 'b_glu', 'g_attn', 'g_ssm', 'w_out', 'ln1_g', 'ln1_b',
           'w_mlp1', 'w_mlp2', 'ln2_g', 'ln2_b']
BIG = ['w_in', 'w_glu', 'w_out', 'w_mlp1', 'w_mlp2']
SMALL = [n for n in WEIGHTS if n not in BIG and n != 'w_ada']


def _params(n_grid):
    return pltpu.CompilerParams(dimension_semantics=("arbitrary",) * n_grid, vmem_limit_bytes=VMEM_LIMIT_V7X)


def _tile(n, cap):
    t = 1 << (max(1, min(n, cap)).bit_length() - 1)
    while n % t:
        t //= 2
    return t


def _operand_spec(stack, transposed, tr, tc, nrb, ncb, pick):
    def index(i, j, k):
        r, c = pick(i, j, k)
        s = None
        if stack == 'r':
            s, r = r // nrb, r % nrb
        elif stack == 'c':
            s, c = c // ncb, c % ncb
        idx = (c, r) if transposed else (r, c)
        return idx if s is None else (s,) + idx
    blk = (tc, tr) if transposed else (tr, tc)
    return pl.BlockSpec(blk if stack is None else (None,) + blk, index)


def _mm(a, b, *, M, N, K, name, out_dtype, ta=False, tb=False, a_st=None, b_st=None, o_st=None, ns=1,
        a_tf=None, epi=None, extra=None, bias=None, cap_m=1024, cap_n=1024, cap_k=1024):
    pm = M // ns if 'm' in (a_st, o_st) else M
    pn = N // ns if 'n' in (b_st, o_st) else N
    pk = K // ns if 'k' in (a_st, b_st) else K
    tm, tn, tk = _tile(pm, cap_m), _tile(pn, cap_n), _tile(pk, cap_k)
    gm, gn, gk = M // tm, N // tn, K // tk
    nmb, nnb, nkb = pm // tm, pn // tn, pk // tk
    a_spec = _operand_spec({None: None, 'm': 'r', 'k': 'c'}[a_st], ta, tm, tk, nmb, nkb, lambda i, j, k: (i, k))
    b_spec = _operand_spec({None: None, 'k': 'r', 'n': 'c'}[b_st], tb, tk, tn, nkb, nnb, lambda i, j, k: (k, j))
    o_spec = _operand_spec({None: None, 'm': 'r', 'n': 'c'}[o_st], False, tm, tn, nmb, nnb, lambda i, j, k: (i, j))
    dn = (((0 if ta else 1,), (1 if tb else 0,)), ((), ()))
    ins, specs = [a, b], [a_spec, b_spec]
    if extra is not None:
        ins.append(extra)
        specs.append(o_spec)
    if bias is not None:
        ins.append(bias)
        specs.append(pl.BlockSpec((1, tn), lambda i, j, k: (0, j)))
    n_in = len(ins)
    if o_st is None:
        o_shape = (M, N)
    elif o_st == 'm':
        o_shape = (ns, pm, N)
    else:
        o_shape = (ns, M, pn)

    def body(*refs):
        a_ref, b_ref = refs[0], refs[1]
        o_ref, acc = refs[n_in], refs[n_in + 1]
        k = pl.program_id(2)

        @pl.when(k == 0)
        def _():
            acc[...] = jnp.zeros_like(acc)

        av = a_ref[...]
        if a_tf is not None:
            av = a_tf(av.astype(F32))
        acc[...] += lax.dot_general(av.astype(BF16), b_ref[...].astype(BF16), dn, preferred_element_type=F32)

        @pl.when(k == gk - 1)
        def _():
            r = acc[...]
            pos = 2
            if extra is not None:
                r = epi(r, refs[pos][...].astype(F32))
                pos += 1
            elif epi is not None:
                r = epi(r)
            if bias is not None:
                r = r + refs[pos][...]
            o_ref[...] = r.astype(o_ref.dtype)

    return pl.pallas_call(
        body, grid=(gm, gn, gk), in_specs=specs, out_specs=o_spec,
        out_shape=jax.ShapeDtypeStruct(o_shape, out_dtype),
        scratch_shapes=[pltpu.VMEM((tm, tn), F32)], compiler_params=_params(3), name=name)(*ins)


def _relu2(v):
    r = jnp.maximum(v, 0.0)
    return r * r


def _silu(v):
    return v / (1.0 + jnp.exp(-v))


def _rb(tr, w):
    return pl.BlockSpec((tr, w), lambda i: (i, 0))


def _pb(r, w):
    return pl.BlockSpec((r, w), lambda i: (0, 0))


def _row_call(body, n_rows, tr, ins, in_specs, outs, out_specs, name):
    return pl.pallas_call(body, grid=(n_rows // tr,), in_specs=in_specs, out_specs=out_specs,
                          out_shape=outs, compiler_params=_params(1), name=name)(*ins)


def _accumulate(ref, value):
    @pl.when(pl.program_id(0) == 0)
    def _():
        ref[...] = jnp.zeros_like(ref)
    ref[...] += jnp.sum(value, axis=0, keepdims=True)


def _modulate(x, m6, row_shift, row_scale, name):
    S, D = x.shape
    tr = _tile(S, 256)

    def body(x_ref, m_ref, h_ref):
        sh = m_ref[row_shift:row_shift + 1, :]
        sc = m_ref[row_scale:row_scale + 1, :]
        h_ref[...] = (x_ref[...] * (1.0 + sc) + sh).astype(BF16)

    return _row_call(body, S, tr, [x, m6], [_rb(tr, D), _pb(6, D)],
                     jax.ShapeDtypeStruct((S, D), BF16), _rb(tr, D), name)


def _gelu(v):
    c = math.sqrt(2.0 / math.pi)
    return 0.5 * v * (1.0 + jnp.tanh(c * (v + 0.044715 * v * v * v)))


def _gelu_grad(v):
    c = math.sqrt(2.0 / math.pi)
    t = jnp.tanh(c * (v + 0.044715 * v * v * v))
    return 0.5 * (1.0 + t) + 0.5 * v * (1.0 - t * t) * c * (1.0 + 3.0 * 0.044715 * v * v)


def _ssm_act(u, y0, y1, dskip):
    S, W = u.shape
    tr = _tile(S, 256)

    def body(u_ref, y0_ref, y1_ref, d_ref, pre_ref, g_ref):
        pre = u_ref[...] * d_ref[...] + y0_ref[...] + y1_ref[...]
        pre_ref[...] = pre
        g_ref[...] = _gelu(pre)

    return _row_call(body, S, tr, [u, y0, y1, dskip], [_rb(tr, W)] * 3 + [_pb(1, W)],
                     [jax.ShapeDtypeStruct((S, W), F32)] * 2, [_rb(tr, W)] * 2, "ssm_act")


def _merge(attn, g, z, b_glu, g_attn, g_ssm):
    S, W = attn.shape
    tr = _tile(S, 256)

    def body(a_ref, g_ref, z_ref, b_ref, ga_ref, gs_ref, o_ref):
        a = a_ref[...]
        ra = lax.rsqrt(jnp.mean(a * a, axis=-1, keepdims=True) + LN_EPS)
        o_ref[:, 0:W] = (a * ra * ga_ref[...]).astype(BF16)
        s = g_ref[...] / (1.0 + jnp.exp(-(z_ref[...] + b_ref[...])))
        rs = lax.rsqrt(jnp.mean(s * s, axis=-1, keepdims=True) + LN_EPS)
        o_ref[:, W:2 * W] = (s * rs * gs_ref[...]).astype(BF16)

    return _row_call(body, S, tr, [attn, g, z, b_glu, g_attn, g_ssm],
                     [_rb(tr, W)] * 3 + [_pb(1, W)] * 3,
                     jax.ShapeDtypeStruct((S, 2 * W), BF16), _rb(tr, 2 * W), "merge")


def _post_ln(x, branch, m6, row_gate, ln_g, ln_b, alpha, name):
    S, D = x.shape
    tr = _tile(S, 256)

    def body(x_ref, br_ref, m_ref, g_ref, b_ref, o_ref):
        gate = m_ref[row_gate:row_gate + 1, :]
        s = alpha * x_ref[...] + (1.0 + gate) * br_ref[...]
        mu = jnp.mean(s, axis=-1, keepdims=True)
        d = s - mu
        var = jnp.mean(d * d, axis=-1, keepdims=True)
        o_ref[...] = d * lax.rsqrt(var + LN_EPS) * g_ref[...] + b_ref[...]

    return _row_call(body, S, tr, [x, branch, m6, ln_g, ln_b],
                     [_rb(tr, D), _rb(tr, D), _pb(6, D), _pb(1, D), _pb(1, D)],
                     jax.ShapeDtypeStruct((S, D), F32), _rb(tr, D), name)


def _loss_head(x, target):
    S, D = x.shape
    tr = _tile(S, 256)

    def body(x_ref, t_ref, dx_ref, acc_ref):
        e = x_ref[...] - t_ref[...]
        dx_ref[...] = e * (1.0 / D)
        _accumulate(acc_ref, e * e)

    return _row_call(body, S, tr, [x, target], [_rb(tr, D)] * 2,
                     [jax.ShapeDtypeStruct((S, D), F32), jax.ShapeDtypeStruct((1, D), F32)],
                     [_rb(tr, D), _pb(1, D)], "loss_head")


def _ln_bwd(dxo, x_in, branch, m6, row_gate, ln_g, alpha, name):
    S, D = dxo.shape
    tr = _tile(S, 256)

    def body(dxo_ref, x_ref, br_ref, m_ref, g_ref, ds_ref, dbr_ref, dg_ref, db_ref, dgate_ref):
        gate = m_ref[row_gate:row_gate + 1, :]
        br = br_ref[...]
        s = alpha * x_ref[...] + (1.0 + gate) * br
        mu = jnp.mean(s, axis=-1, keepdims=True)
        d = s - mu
        var = jnp.mean(d * d, axis=-1, keepdims=True)
        rstd = lax.rsqrt(var + LN_EPS)
        xhat = d * rstd
        dxo = dxo_ref[...]
        dxh = dxo * g_ref[...]
        ds = rstd * (dxh - jnp.mean(dxh, axis=-1, keepdims=True)
                     - xhat * jnp.mean(dxh * xhat, axis=-1, keepdims=True))
        ds_ref[...] = ds
        dbr_ref[...] = ((1.0 + gate) * ds).astype(BF16)
        _accumulate(dg_ref, dxo * xhat)
        _accumulate(db_ref, dxo)
        _accumulate(dgate_ref, ds * br)

    vec = jax.ShapeDtypeStruct((1, D), F32)
    return _row_call(body, S, tr, [dxo, x_in, branch, m6, ln_g],
                     [_rb(tr, D)] * 3 + [_pb(6, D), _pb(1, D)],
                     [jax.ShapeDtypeStruct((S, D), F32), jax.ShapeDtypeStruct((S, D), BF16), vec, vec, vec],
                     [_rb(tr, D), _rb(tr, D), _pb(1, D), _pb(1, D), _pb(1, D)], name)


def _mod_bwd(ds, dh, x_in, m6, row_scale, alpha, name):
    S, D = ds.shape
    tr = _tile(S, 256)

    def body(ds_ref, dh_ref, x_ref, m_ref, dx_ref, dsc_ref, dsh_ref):
        sc = m_ref[row_scale:row_scale + 1, :]
        dh = dh_ref[...]
        dx_ref[...] = alpha * ds_ref[...] + dh * (1.0 + sc)
        _accumulate(dsc_ref, dh * x_ref[...])
        _accumulate(dsh_ref, dh)

    vec = jax.ShapeDtypeStruct((1, D), F32)
    return _row_call(body, S, tr, [ds, dh, x_in, m6], [_rb(tr, D)] * 3 + [_pb(6, D)],
                     [jax.ShapeDtypeStruct((S, D), F32), vec, vec],
                     [_rb(tr, D), _pb(1, D), _pb(1, D)], name)


def _merge_bwd(dmerged, attn, g, z, b_glu, g_attn, g_ssm):
    S, W = attn.shape
    tr = _tile(S, 256)

    def rms_bwd(x, gamma, dy):
        r = lax.rsqrt(jnp.mean(x * x, axis=-1, keepdims=True) + LN_EPS)
        gdy = gamma * dy
        dx = gdy * r - x * (r * r * r) * jnp.mean(gdy * x, axis=-1, keepdims=True)
        return dx, dy * x * r

    def body(dm_ref, a_ref, g_ref, z_ref, b_ref, ga_ref, gs_ref, da_ref, dgp_ref, dz_ref, dga_ref, dgs_ref, db_ref):
        da, dga = rms_bwd(a_ref[...], ga_ref[...], dm_ref[:, 0:W])
        da_ref[...] = da
        _accumulate(dga_ref, dga)
        gv = g_ref[...]
        sig = 1.0 / (1.0 + jnp.exp(-(z_ref[...] + b_ref[...])))
        dssm, dgs = rms_bwd(gv * sig, gs_ref[...], dm_ref[:, W:2 * W])
        _accumulate(dgs_ref, dgs)
        dgp_ref[...] = dssm * sig
        dz = dssm * gv * sig * (1.0 - sig)
        dz_ref[...] = dz.astype(BF16)
        _accumulate(db_ref, dz)

    vec = jax.ShapeDtypeStruct((1, W), F32)
    mat = jax.ShapeDtypeStruct((S, W), F32)
    return _row_call(body, S, tr, [dmerged, attn, g, z, b_glu, g_attn, g_ssm],
                     [_rb(tr, 2 * W)] + [_rb(tr, W)] * 3 + [_pb(1, W)] * 3,
                     [mat, mat, jax.ShapeDtypeStruct((S, W), BF16), vec, vec, vec],
                     [_rb(tr, W)] * 3 + [_pb(1, W)] * 3, "merge_bwd")


def _act_bwd(dgp, dgl, pre, u, dskip):
    S, W = pre.shape
    tr = _tile(S, 256)

    def body(a_ref, b_ref, pre_ref, u_ref, d_ref, dy_ref, du_ref, dd_ref):
        dy = (a_ref[...] + b_ref[...]) * _gelu_grad(pre_ref[...])
        dy_ref[...] = dy
        du_ref[...] = dy * d_ref[...]
        _accumulate(dd_ref, dy * u_ref[...])

    mat = jax.ShapeDtypeStruct((S, W), F32)
    return _row_call(body, S, tr, [dgp, dgl, pre, u, dskip], [_rb(tr, W)] * 4 + [_pb(1, W)],
                     [mat, mat, jax.ShapeDtypeStruct((1, W), F32)], [_rb(tr, W)] * 2 + [_pb(1, W)], "act_bwd")


def _dproj(dq, dkT, dvT, du_skip, du0, du1, reach):
    S, W = dq.shape
    tr = _tile(S, 256)
    lanes = 2 * HEAD_DIM
    n_pairs, nkb = W // lanes, tr // lanes
    assert tr % lanes == 0 and reach % tr == 0

    def body(dq_ref, dk_ref, dv_ref, a_ref, b_ref, c_ref, o_ref):
        o_ref[0] = dq_ref[...].astype(BF16)
        for h in range(n_pairs):
            for b in range(nkb):
                rows, cols = slice(b * lanes, (b + 1) * lanes), slice(h * lanes, (h + 1) * lanes)
                o_ref[1, rows, cols] = dk_ref[h, b].T.astype(BF16)
                o_ref[2, rows, cols] = dv_ref[h, b].T.astype(BF16)
        o_ref[3] = (a_ref[...] + b_ref[...] + c_ref[...]).astype(BF16)

    tsp = pl.BlockSpec((n_pairs, nkb, lanes, lanes), lambda i: (0, i + reach // tr, 0, 0))
    return _row_call(body, S, tr, [dq, dkT, dvT, du_skip, du0, du1],
                     [_rb(tr, W), tsp, tsp, _rb(tr, W), _rb(tr, W), _rb(tr, W)],
                     jax.ShapeDtypeStruct((4, S, W), BF16), pl.BlockSpec((4, tr, W), lambda i: (0, i, 0)), "dproj")


def _attn_reach():
    return max(w // 2 for w, _ in ATTN_PATTERNS)


def _bias_table(tq, width, reach, head):
    i = lax.broadcasted_iota(jnp.int32, (tq, width), 0)
    j = lax.broadcasted_iota(jnp.int32, (tq, width), 1)
    ad = jnp.abs(j - i - reach)
    mult = jnp.zeros((tq, width), jnp.int32)
    for window, dil in ATTN_PATTERNS:
        assert dil & (dil - 1) == 0
        mult += ((jnp.bitwise_and(ad, dil - 1) == 0) & (ad <= window // 2)).astype(jnp.int32)
    logm = jnp.zeros((tq, width), F32)
    for n in range(2, len(ATTN_PATTERNS) + 1):
        logm = jnp.where(mult == n, math.log(n), logm)
    return logm, ad.astype(F32), mult > 0


def _attn_common(n_heads, q_ref, k_ref, v_ref, bias_ref, tq, reach, seq):
    width = tq + 2 * reach
    hp, qi = pl.program_id(0), pl.program_id(1)

    @pl.when(qi == 0)
    def _():
        logm, ad, ok = _bias_table(tq, width, reach, None)
        for a in range(2):
            head = (2 * hp + a + 1).astype(F32)
            slope = jnp.exp(jnp.full((tq, width), -8.0 * math.log(2.0) / n_heads, F32) * head)
            bias_ref[a] = jnp.where(ok, logm - slope * ad, NEG_BIG)

    start = pl.multiple_of(qi * tq, tq)
    kw = k_ref[pl.ds(start, width), :]
    vw = v_ref[pl.ds(start, width), :]
    kpos = qi * tq - reach + lax.broadcasted_iota(jnp.int32, (1, width), 1)
    vbias = jnp.where((kpos >= 0) & (kpos < seq), 0.0, NEG_BIG).astype(F32)
    lane = lax.broadcasted_iota(jnp.int32, (1, 2 * HEAD_DIM), 1)
    masks = [lane < HEAD_DIM, lane >= HEAD_DIM]
    return start, kw, vw, vbias, masks


_NT = (((1,), (1,)), ((), ()))


def _attention_fwd(qkv, kp, vp, n_heads, tq):
    _, S, W = qkv.shape
    reach = _attn_reach()
    width = tq + 2 * reach
    Sp = S + 2 * reach
    scale = HEAD_DIM ** -0.5
    lanes = 2 * HEAD_DIM

    def body(q_ref, k_ref, v_ref, o_ref, lse_ref, bias_ref):
        _, kw, vw, vbias, masks = _attn_common(n_heads, q_ref, k_ref, v_ref, bias_ref, tq, reach, S)
        q = q_ref[...]
        out = jnp.zeros((tq, lanes), F32)
        lse = jnp.zeros((tq, lanes), F32)
        for a in range(2):
            qa = jnp.where(masks[a], q, jnp.zeros_like(q))
            s = lax.dot_general(qa, kw, _NT, preferred_element_type=F32) * scale + bias_ref[a] + vbias
            m = jnp.max(s, axis=-1, keepdims=True)
            p = jnp.exp(s - m)
            l = jnp.sum(p, axis=-1, keepdims=True)
            o = jnp.dot(p.astype(BF16), vw, preferred_element_type=F32) / l
            out = jnp.where(masks[a], o, out)
            lse = jnp.where(masks[a], m + jnp.log(l), lse)
        o_ref[...] = out
        lse_ref[...] = lse

    qsp = pl.BlockSpec((None, tq, lanes), lambda h, i: (0, i, h))
    ksp = pl.BlockSpec((Sp, lanes), lambda h, i: (0, h))
    osp = pl.BlockSpec((tq, lanes), lambda h, i: (i, h))
    mat = jax.ShapeDtypeStruct((S, W), F32)
    return pl.pallas_call(body, grid=(W // lanes, S // tq), in_specs=[qsp, ksp, ksp], out_specs=[osp, osp],
                          out_shape=[mat, mat], scratch_shapes=[pltpu.VMEM((2, tq, width), F32)],
                          compiler_params=_params(2), name="attn_fwd")(qkv, kp, vp)


def _attention_bwd(qkv, kp, vp, out, lse, dout, n_heads, tq):
    _, S, W = qkv.shape
    reach = _attn_reach()
    width = tq + 2 * reach
    Sp = S + 2 * reach
    scale = HEAD_DIM ** -0.5
    lanes = 2 * HEAD_DIM

    def body(q_ref, k_ref, v_ref, o_ref, lse_ref, do_ref, dq_ref, dk_ref, dv_ref, bias_ref):
        start, kw, vw, vbias, masks = _attn_common(n_heads, q_ref, k_ref, v_ref, bias_ref, tq, reach, S)

        @pl.when(pl.program_id(1) == 0)
        def _():
            dk_ref[...] = jnp.zeros_like(dk_ref)
            dv_ref[...] = jnp.zeros_like(dv_ref)

        q = q_ref[...]
        do = do_ref[...]
        prod = do * o_ref[...]
        lse_all = lse_ref[...]
        dq = jnp.zeros((tq, lanes), F32)
        dkt = jnp.zeros((lanes, width), F32)
        dvt = jnp.zeros((lanes, width), F32)
        for a in range(2):
            qa = jnp.where(masks[a], q, jnp.zeros_like(q))
            doa = jnp.where(masks[a], do, 0.0)
            s = lax.dot_general(qa, kw, _NT, preferred_element_type=F32) * scale + bias_ref[a] + vbias
            lse_a = lse_all[:, a * HEAD_DIM:a * HEAD_DIM + 1]
            p = jnp.exp(s - lse_a)
            delta = jnp.sum(jnp.where(masks[a], prod, 0.0), axis=-1, keepdims=True)
            dp = lax.dot_general(doa.astype(BF16), vw, _NT, preferred_element_type=F32)
            ds = (p * (dp - delta) * scale).astype(BF16)
            dvt += jnp.dot(doa.T.astype(BF16), p.astype(BF16), preferred_element_type=F32)
            dkt += jnp.dot(qa.astype(F32).T.astype(BF16), ds, preferred_element_type=F32)
            dq = jnp.where(masks[a], jnp.dot(ds, kw, preferred_element_type=F32), dq)
        dq_ref[...] = dq
        first = pl.program_id(1) * (tq // lanes)
        for j in range(width // lanes):
            dk_ref[first + j] += dkt[:, j * lanes:(j + 1) * lanes]
            dv_ref[first + j] += dvt[:, j * lanes:(j + 1) * lanes]

    assert tq % lanes == 0 and reach % lanes == 0
    qsp = pl.BlockSpec((None, tq, lanes), lambda h, i: (0, i, h))
    ksp = pl.BlockSpec((Sp, lanes), lambda h, i: (0, h))
    osp = pl.BlockSpec((tq, lanes), lambda h, i: (i, h))
    tsp = pl.BlockSpec((None, Sp // lanes, lanes, lanes), lambda h, i: (h, 0, 0, 0))
    tiles = jax.ShapeDtypeStruct((W // lanes, Sp // lanes, lanes, lanes), F32)
    return pl.pallas_call(body, grid=(W // lanes, S // tq), in_specs=[qsp, ksp, ksp, osp, osp, osp],
                          out_specs=[osp, tsp, tsp],
                          out_shape=[jax.ShapeDtypeStruct((S, W), F32), tiles, tiles],
                          scratch_shapes=[pltpu.VMEM((2, tq, width), F32)],
                          compiler_params=_params(2), name="attn_bwd")(qkv, kp, vp, out, lse, dout)


SB = GROUPS_PER_BLOCK * SSM_STATE
CB = GROUPS_PER_BLOCK * SSM_GROUP
RUNS = 8


def _powers(pw, a_r, a_i, n):
    p_r, p_i = a_r, a_i
    for j in range(n):
        pw[0, j:j + 1, :] = p_r
        pw[1, j:j + 1, :] = p_i
        p_r, p_i = p_r * a_r - p_i * a_i, p_r * a_i + p_i * a_r


def _rows8(j):
    return slice(j * RUNS, (j + 1) * RUNS)


def _run_scan(br, bi, T, a_r, a_i, pw, e_r, e_i, ent, down, conj):
    n = T // RUNS
    sg = -1.0 if conj else 1.0
    A_r = jnp.broadcast_to(a_r, (RUNS, SB))
    A_i = jnp.broadcast_to(sg * a_i, (RUNS, SB))
    x_r = x_i = None
    for j in (range(n - 1, -1, -1) if down else range(n)):
        b_r, b_i = br[_rows8(j), :], bi[_rows8(j), :]
        if x_r is None:
            x_r, x_i = b_r, b_i
        else:
            x_r, x_i = A_r * x_r - A_i * x_i + b_r, A_r * x_i + A_i * x_r + b_i
            br[_rows8(j), :] = x_r
            bi[_rows8(j), :] = x_i
    al_r, al_i = pw[0, n - 1:n, :], sg * pw[1, n - 1:n, :]
    s_r, s_i = e_r, e_i
    for c in (range(RUNS - 1, -1, -1) if down else range(RUNS)):
        ent[0, c:c + 1, :] = s_r
        ent[1, c:c + 1, :] = s_i
        s_r, s_i = (al_r * s_r - al_i * s_i + x_r[c:c + 1, :], al_r * s_i + al_i * s_r + x_i[c:c + 1, :])
    E_r, E_i = ent[0], ent[1]
    for j in range(n):
        p = n - 1 - j if down else j
        p_r, p_i = pw[0, p:p + 1, :], sg * pw[1, p:p + 1, :]
        v_r, v_i = br[_rows8(j), :], bi[_rows8(j), :]
        br[_rows8(j), :] = v_r + p_r * E_r - p_i * E_i
        bi[_rows8(j), :] = v_i + p_r * E_i + p_i * E_r
    return s_r, s_i


def _load_run_major(ref, T):
    n = T // RUNS
    return jnp.concatenate([ref[pl.ds(j, RUNS, stride=n), :] for j in range(n)], axis=0)


def _store_row_major(ref, value, T):
    n = T // RUNS
    for j in range(n):
        ref[pl.ds(j, RUNS, stride=n), :] = value[_rows8(j), :]


def _s5_specs(nb, T, rev, adjoint):
    flip = rev != adjoint
    blk = (lambda i: nb - 1 - i) if flip else (lambda i: i)
    usp = pl.BlockSpec((T, CB), lambda k, i: (blk(i), k))
    asp = pl.BlockSpec((None, 2, SB), lambda k, i: (k, 0, 0))
    wsp = pl.BlockSpec((None, CB, SB), lambda k, i: (k, 0, 0))
    csp = pl.BlockSpec((None, None, 2, SB), lambda k, i: (k, blk(i), 0, 0))
    return usp, asp, wsp, csp


def _s5_fwd(u, a, b_re, b_im, c_re, c_im, T, rev):
    S, W = u.shape
    NK = W // CB
    nb = S // T
    usp, asp, wsp, csp = _s5_specs(nb, T, rev, False)

    def body(u_ref, a_ref, bre_ref, bim_ref, cre_ref, cim_ref, y_ref, car_ref, wr, wi, st, pw, ent):
        a_r, a_i = a_ref[0:1, :], a_ref[1:2, :]

        @pl.when(pl.program_id(1) == 0)
        def _():
            st[...] = jnp.zeros_like(st)
            _powers(pw, a_r, a_i, T // RUNS)

        car_ref[...] = st[0:2, :]
        ub = _load_run_major(u_ref, T).astype(BF16)
        wr[...] = jnp.dot(ub, bre_ref[...], preferred_element_type=F32)
        wi[...] = jnp.dot(ub, bim_ref[...], preferred_element_type=F32)
        s_r, s_i = _run_scan(wr, wi, T, a_r, a_i, pw, st[0:1, :], st[1:2, :], ent, rev, False)
        st[0:1, :] = s_r
        st[1:2, :] = s_i
        y = (lax.dot_general(wr[...].astype(BF16), cre_ref[...], _NT, preferred_element_type=F32)
             - lax.dot_general(wi[...].astype(BF16), cim_ref[...], _NT, preferred_element_type=F32))
        _store_row_major(y_ref, y, T)

    return pl.pallas_call(
        body, grid=(NK, nb), in_specs=[usp, asp, wsp, wsp, wsp, wsp], out_specs=[usp, csp],
        out_shape=[jax.ShapeDtypeStruct((S, W), F32), jax.ShapeDtypeStruct((NK, nb, 2, SB), F32)],
        scratch_shapes=[pltpu.VMEM((T, SB), F32), pltpu.VMEM((T, SB), F32), pltpu.VMEM((8, SB), F32),
                        pltpu.VMEM((2, T // RUNS, SB), F32), pltpu.VMEM((2, RUNS, SB), F32)],
        compiler_params=_params(2), name="s5_fwd_rev" if rev else "s5_fwd")(u, a, b_re, b_im, c_re, c_im)


def _s5_bwd(u, dy, a, b_re, b_im, c_re, c_im, carries, T, rev):
    S, W = u.shape
    NK = W // CB
    nb = S // T
    usp, asp, wsp, csp = _s5_specs(nb, T, rev, True)
    n = T // RUNS

    def body(u_ref, dy_ref, a_ref, bre_ref, bim_ref, cre_ref, cim_ref, car_ref,
             du_ref, dbre_ref, dbim_ref, dcre_ref, dcim_ref, da_ref, wr, wi, gr, gi, lam, pw, ent):
        a_r, a_i = a_ref[0:1, :], a_ref[1:2, :]

        @pl.when(pl.program_id(1) == 0)
        def _():
            lam[...] = jnp.zeros_like(lam)
            for r in (dbre_ref, dbim_ref, dcre_ref, dcim_ref, da_ref):
                r[...] = jnp.zeros_like(r)
            _powers(pw, a_r, a_i, n)

        u32 = _load_run_major(u_ref, T)
        ub = u32.astype(BF16)
        dyk = _load_run_major(dy_ref, T)
        dyb = dyk.astype(BF16)
        wr[...] = jnp.dot(ub, bre_ref[...], preferred_element_type=F32)
        wi[...] = jnp.dot(ub, bim_ref[...], preferred_element_type=F32)
        x0r, x0i = car_ref[0:1, :], car_ref[1:2, :]
        _run_scan(wr, wi, T, a_r, a_i, pw, x0r, x0i, ent, rev, False)

        gr[...] = jnp.dot(dyb, cre_ref[...], preferred_element_type=F32)
        gi[...] = -jnp.dot(dyb, cim_ref[...], preferred_element_type=F32)
        l_r, l_i = _run_scan(gr, gi, T, a_r, a_i, pw, lam[0:1, :], lam[1:2, :], ent, not rev, True)
        lam[0:1, :] = l_r
        lam[1:2, :] = l_i

        sub = lax.broadcasted_iota(jnp.int32, (RUNS, SB), 0)

        def before(buf, x0, j):
            if rev:
                if j < n - 1:
                    return buf[_rows8(j + 1), :]
                return jnp.where(sub == RUNS - 1, x0, pltpu.roll(buf[_rows8(0), :], RUNS - 1, 0))
            if j > 0:
                return buf[_rows8(j - 1), :]
            return jnp.where(sub == 0, x0, pltpu.roll(buf[_rows8(n - 1), :], 1, 0))

        acc_r = jnp.zeros((RUNS, SB), F32)
        acc_i = jnp.zeros((RUNS, SB), F32)
        for j in range(n):
            g_r, g_i = gr[_rows8(j), :], gi[_rows8(j), :]
            p_r, p_i = before(wr, x0r, j), before(wi, x0i, j)
            acc_r += g_r * p_r + g_i * p_i
            acc_i += g_i * p_r - g_r * p_i
        da_ref[0:1, :] += jnp.sum(acc_r, axis=0, keepdims=True)
        da_ref[1:2, :] += jnp.sum(acc_i, axis=0, keepdims=True)

        lrb, lib = gr[...].astype(BF16), gi[...].astype(BF16)
        du = (lax.dot_general(lrb, bre_ref[...], _NT, preferred_element_type=F32)
              + lax.dot_general(lib, bim_ref[...], _NT, preferred_element_type=F32))
        _store_row_major(du_ref, du, T)
        ut = u32.T.astype(BF16)
        dbre_ref[...] += jnp.dot(ut, lrb, preferred_element_type=F32)
        dbim_ref[...] += jnp.dot(ut, lib, preferred_element_type=F32)
        dyt = dyk.T.astype(BF16)
        dcre_ref[...] += jnp.dot(dyt, wr[...].astype(BF16), preferred_element_type=F32)
        dcim_ref[...] += jnp.dot(dyt, wi[...].astype(BF16), preferred_element_type=F32)

    blk = jax.ShapeDtypeStruct((NK, CB, SB), F32)
    return pl.pallas_call(
        body, grid=(NK, nb), in_specs=[usp, usp, asp, wsp, wsp, wsp, wsp, csp],
        out_specs=[usp, wsp, wsp, wsp, wsp, asp],
        out_shape=[jax.ShapeDtypeStruct((S, W), F32), blk, blk, blk, blk, jax.ShapeDtypeStruct((NK, 2, SB), F32)],
        scratch_shapes=[pltpu.VMEM((T, SB), F32), pltpu.VMEM((T, SB), F32),
                        pltpu.VMEM((T, SB), F32), pltpu.VMEM((T, SB), F32), pltpu.VMEM((8, SB), F32),
                        pltpu.VMEM((2, n, SB), F32), pltpu.VMEM((2, RUNS, SB), F32)],
        compiler_params=_params(2), name="s5_bwd_rev" if rev else "s5_bwd")(
            u, dy, a, b_re, b_im, c_re, c_im, carries)


def _s5_discretize(lam_re, lam_im, log_step, b_re, b_im):
    step = jnp.exp(log_step)[..., None]
    mag = jnp.exp(lam_re * step)
    a_re, a_im = mag * jnp.cos(lam_im * step), mag * jnp.sin(lam_im * step)
    den = lam_re * lam_re + lam_im * lam_im
    coef_re = ((a_re - 1.0) * lam_re + a_im * lam_im) / den
    coef_im = (a_im * lam_re - (a_re - 1.0) * lam_im) / den
    bb_re = coef_re[..., None] * b_re - coef_im[..., None] * b_im
    bb_im = coef_re[..., None] * b_im + coef_im[..., None] * b_re
    return a_re, a_im, bb_re, bb_im


def _to_blocks(w_gcp):
    two, G, C, P = w_gcp.shape
    nk = G // GROUPS_PER_BLOCK
    x = w_gcp.reshape(two, nk, GROUPS_PER_BLOCK, C, P)
    eye = jnp.eye(GROUPS_PER_BLOCK, dtype=w_gcp.dtype)
    return jnp.einsum('dkgcp,gh->dkgchp', x, eye).reshape(two, nk, GROUPS_PER_BLOCK * C, GROUPS_PER_BLOCK * P)


def _from_blocks(blk):
    two, nk, cb, sb = blk.shape
    C, P = cb // GROUPS_PER_BLOCK, sb // GROUPS_PER_BLOCK
    x = blk.reshape(two, nk, GROUPS_PER_BLOCK, C, GROUPS_PER_BLOCK, P)
    eye = jnp.eye(GROUPS_PER_BLOCK, dtype=blk.dtype)
    return jnp.einsum('dkgchp,gh->dkgcp', x, eye).reshape(two, nk * GROUPS_PER_BLOCK, C, P)


def _me():
    return lax.axis_index("x"), lax.axis_index("y"), lax.axis_index("c")


def _peer(k):
    x, y, c = _me()
    return (1 - x if k & 4 else x, 1 - y if k & 2 else y, 1 - c if k & 1 else c)


def _logical(dev):
    return 4 * dev[0] + 2 * dev[1] + dev[2]


def _exchange(name, ks, src_of, dst_of, out_shape, ins, split=1):
    n = len(ks) * split
    n_in = len(ins)

    def body(*refs):
        in_refs, out_ref = refs[:n_in], refs[n_in]
        send, recv = refs[n_in + 1:]
        copies = []
        for k in ks:
            src, dst = src_of(k, in_refs, out_ref), dst_of(k, in_refs, out_ref)
            rows = src.shape[0] // split
            for q in range(split):
                part = pl.ds(q * rows, rows)
                j = len(copies)
                copies.append(pltpu.make_async_remote_copy(
                    src_ref=src if split == 1 else src.at[part], dst_ref=dst if split == 1 else dst.at[part],
                    send_sem=send.at[j], recv_sem=recv.at[j], device_id=_peer(k), device_id_type=MESH))
        for cp in copies:
            cp.start()
        for cp in copies:
            cp.wait_recv()
        for cp in copies:
            cp.wait_send()

    return pl.pallas_call(
        body, in_specs=[ANY] * n_in, out_specs=ANY, out_shape=out_shape,
        scratch_shapes=[pltpu.SemaphoreType.DMA((n,)), pltpu.SemaphoreType.DMA((n,))], name=name)(*ins)


def _all_gather8(v, name):
    out = jax.ShapeDtypeStruct((N_DEV,) + v.shape, v.dtype)
    slot = lambda k, ins, o: o.at[_logical(_me())]
    got = _exchange(name, list(range(1, 8)), lambda k, ins, o: ins[0], slot, out, [v])
    return lax.dynamic_update_index_in_dim(got, v, _logical(_me()), 0)


def _all_gather_xy(v, name):
    R, C = v.shape
    half = lambda ins: ins[0].at[pl.ds(lax.axis_index("c") * (R // 2), R // 2)]
    chip = lambda: 2 * lax.axis_index("x") + lax.axis_index("y")
    halves = _exchange(name, [2, 4, 6], lambda k, ins, o: half(ins), lambda k, ins, o: o.at[chip()],
                       jax.ShapeDtypeStruct((4, R // 2, C), v.dtype), [v])
    mine = lax.dynamic_slice_in_dim(v, lax.axis_index("c") * (R // 2), R // 2, axis=0)
    halves = lax.dynamic_update_index_in_dim(halves, mine, chip(), 0)
    core = lambda o: o.at[:, lax.axis_index("c")]
    both = _exchange(name + "_join", [1], lambda k, ins, o: ins[0], lambda k, ins, o: core(o),
                     jax.ShapeDtypeStruct((4, 2, R // 2, C), v.dtype), [halves], split=4)
    return lax.dynamic_update_index_in_dim(both, halves, lax.axis_index("c"), 1).reshape(4, R, C)


def _scatter8(g, name):
    two, n, four, R, C = g.shape
    out = jax.ShapeDtypeStruct((N_DEV, n, R, C), g.dtype)

    def piece(dev, ins):
        return ins[0].at[dev[2], :, 2 * dev[0] + dev[1]]

    slot = lambda k, ins, o: o.at[_logical(_me())]
    got = _exchange(name, list(range(1, 8)), lambda k, ins, o: piece(_peer(k), ins), slot, out, [g])
    x, y, c = _me()
    mine = lax.dynamic_index_in_dim(lax.dynamic_index_in_dim(g, c, 0, False), 2 * x + y, 1, False)
    return lax.dynamic_update_index_in_dim(got, mine, _logical(_me()), 0)


def _swap_c(v, name):
    out = jax.ShapeDtypeStruct((2,) + v.shape, v.dtype)
    slot = lambda: lax.axis_index("c")
    got = _exchange(name, [1], lambda k, ins, o: ins[0], lambda k, ins, o: o.at[slot()], out, [v], split=8)
    return lax.dynamic_update_index_in_dim(got, v, slot(), 0)


def _sum_leading(v, name):
    n, R, C = v.shape
    tr = _tile(R, max(8, (1 << 19) // (C * n)))

    def body(v_ref, o_ref):
        acc = v_ref[0].astype(F32)
        for s in range(1, n):
            acc = acc + v_ref[s].astype(F32)
        o_ref[...] = acc

    return pl.pallas_call(body, grid=(R // tr,), in_specs=[pl.BlockSpec((n, tr, C), lambda i: (0, i, 0))],
                          out_specs=pl.BlockSpec((tr, C), lambda i: (i, 0)),
                          out_shape=jax.ShapeDtypeStruct((R, C), F32), compiler_params=_params(1), name=name)(v)


def _adamw(w, g, m, v, name):
    R, C = w.shape
    tr = _tile(R, max(8, (1 << 18) // C))
    c1 = 1.0 - ADAM_B1 ** ADAM_STEP
    c2 = 1.0 - ADAM_B2 ** ADAM_STEP

    def body(w_ref, g_ref, m_ref, v_ref, d_ref, nm_ref, nv_ref):
        gv = g_ref[...]
        nm = ADAM_B1 * m_ref[...] + (1.0 - ADAM_B1) * gv
        nv = ADAM_B2 * v_ref[...] + (1.0 - ADAM_B2) * (gv * gv)
        nm_ref[...] = nm
        nv_ref[...] = nv
        d_ref[...] = -ADAM_LR * ((nm / c1) / (jnp.sqrt(nv / c2) + ADAM_EPS) + ADAM_WD * w_ref[...])

    sp = pl.BlockSpec((tr, C), lambda i: (i, 0))
    return pl.pallas_call(body, grid=(R // tr,), in_specs=[sp] * 4, out_specs=[sp] * 3,
                          out_shape=[jax.ShapeDtypeStruct((R, C), F32)] * 3, compiler_params=_params(1),
                          name=name)(w, g, m, v)


def _forward_layer(x, lw, alpha, n_heads, tq, ts):
    S, D = x.shape
    W = D // 2
    reach = _attn_reach()
    m6 = lw['m6']
    gw = {n: _all_gather_xy(lw[n].astype(BF16), "gather_" + n) for n in BIG}
    FF = gw['w_mlp1'].shape[2] * 4

    h = _modulate(x, m6, 0, 1, "modulate1")
    qkv = _mm(h, gw['w_in'], M=S, N=3 * W, K=D, b_st='n', o_st='n', ns=3, out_dtype=BF16, name="proj_qkv")
    u = _mm(h, gw['w_in'][3], M=S, N=W, K=D, out_dtype=F32, name="proj_u")
    pad = ((reach, reach), (0, 0))
    kp, vp = jnp.pad(qkv[1], pad), jnp.pad(qkv[2], pad)
    attn, lse = _attention_fwd(qkv, kp, vp, n_heads, tq)
    y, carries = zip(*[_s5_fwd(u, lw['a'][d], lw['b_re'][d], lw['b_im'][d], lw['c_re'][d], lw['c_im'][d], ts,
                               rev=bool(d)) for d in range(2)])
    pre, g = _ssm_act(u, y[0], y[1], lw['ssm_d'])
    z = _mm(g, gw['w_glu'].reshape(W, W), M=S, N=W, K=W, out_dtype=F32, name="glu")
    merged = _merge(attn, g, z, lw['b_glu'], lw['g_attn'], lw['g_ssm'])
    mix = _mm(merged, gw['w_out'].reshape(D, D), M=S, N=D, K=D, out_dtype=F32, name="out_proj")
    x1 = _post_ln(x, mix, m6, 2, lw['ln1_g'], lw['ln1_b'], alpha, "post_ln1")
    h2 = _modulate(x1, m6, 3, 4, "modulate2")
    a4 = _mm(h2, gw['w_mlp1'], M=S, N=FF, K=D, b_st='n', o_st='n', ns=4, out_dtype=F32, name="mlp1")
    ff = _mm(a4, gw['w_mlp2'], M=S, N=D, K=FF, a_st='k', b_st='k', ns=4, a_tf=_relu2, out_dtype=F32, name="mlp2")
    x2 = _post_ln(x1, ff, m6, 5, lw['ln2_g'], lw['ln2_b'], alpha, "post_ln2")
    saved = dict(x=x, h=h, qkv=qkv, u=u, attn=attn, lse=lse, carries=carries, pre=pre, g=g, z=z, merged=merged,
                 mix=mix, x1=x1, h2=h2, a4=a4, ff=ff, gw=gw)
    return x2, saved


def _backward_layer(dx2, lw, sv, alpha, n_heads, tq, ts):
    S, D = dx2.shape
    W = D // 2
    reach = _attn_reach()
    m6, gw = lw['m6'], sv['gw']
    FF = gw['w_mlp1'].shape[2] * 4
    gr = {}

    ds2, dff, gr['ln2_g'], gr['ln2_b'], d_g2 = _ln_bwd(dx2, sv['x1'], sv['ff'], m6, 5, lw['ln2_g'], alpha, "ln2_bwd")
    da4 = _mm(dff, gw['w_mlp2'], M=S, N=FF, K=D, tb=True, b_st='n', o_st='n', ns=4, extra=sv['a4'],
              epi=lambda acc, a: 2.0 * jnp.maximum(a, 0.0) * acc, out_dtype=BF16, name="mlp2_bwd_x")
    gr['w_mlp2'] = _mm(sv['a4'], dff, M=FF, N=D, K=S, ta=True, a_st='m', o_st='m', ns=4, a_tf=_relu2,
                       out_dtype=BF16, name="mlp2_bwd_w")
    dh2 = _mm(da4, gw['w_mlp1'], M=S, N=D, K=FF, tb=True, a_st='k', b_st='k', ns=4, out_dtype=F32, name="mlp1_bwd_x")
    gr['w_mlp1'] = _mm(sv['h2'], da4, M=D, N=FF, K=S, ta=True, b_st='n', o_st='n', ns=4, out_dtype=BF16,
                       name="mlp1_bwd_w")
    dx1, d_sc2, d_sh2 = _mod_bwd(ds2, dh2, sv['x1'], m6, 4, alpha, "mod2_bwd")

    ds1, dmix, gr['ln1_g'], gr['ln1_b'], d_g1 = _ln_bwd(dx1, sv['x'], sv['mix'], m6, 2, lw['ln1_g'], alpha, "ln1_bwd")
    dmerged = _mm(dmix, gw['w_out'].reshape(D, D), M=S, N=D, K=D, tb=True, out_dtype=F32, name="out_proj_bwd_x")
    gr['w_out'] = _mm(sv['merged'], dmix, M=D, N=D, K=S, ta=True, out_dtype=BF16, name="out_proj_bwd_w")
    dattn, dgp, dz, gr['g_attn'], gr['g_ssm'], gr['b_glu'] = _merge_bwd(
        dmerged, sv['attn'], sv['g'], sv['z'], lw['b_glu'], lw['g_attn'], lw['g_ssm'])
    dgl = _mm(dz, gw['w_glu'].reshape(W, W), M=S, N=W, K=W, tb=True, out_dtype=F32, name="glu_bwd_x")
    gr['w_glu'] = _mm(sv['g'], dz, M=W, N=W, K=S, ta=True, out_dtype=BF16, name="glu_bwd_w")
    dy, du_skip, gr['ssm_d'] = _act_bwd(dgp, dgl, sv['pre'], sv['u'], lw['ssm_d'])
    s5 = [_s5_bwd(sv['u'], dy, lw['a'][d], lw['b_re'][d], lw['b_im'][d], lw['c_re'][d], lw['c_im'][d],
                  sv['carries'][d], ts, rev=bool(d)) for d in range(2)]
    du = [s5[d][0] for d in range(2)]
    for pos, n in enumerate(['bb_re', 'bb_im', 'cc_re', 'cc_im', 'a']):
        gr[n] = jnp.stack([s5[d][pos + 1] for d in range(2)])
    pad = ((reach, reach), (0, 0))
    kp, vp = jnp.pad(sv['qkv'][1], pad), jnp.pad(sv['qkv'][2], pad)
    dq, dkT, dvT = _attention_bwd(sv['qkv'], kp, vp, sv['attn'], sv['lse'], dattn, n_heads, tq)
    dproj = _dproj(dq, dkT, dvT, du_skip, du[0], du[1], reach)
    dh = _mm(dproj, gw['w_in'], M=S, N=D, K=4 * W, tb=True, a_st='k', b_st='k', ns=4, out_dtype=F32, name="proj_bwd_x")
    gr['w_in'] = _mm(sv['h'], dproj, M=D, N=4 * W, K=S, ta=True, b_st='n', o_st='n', ns=4, out_dtype=BF16,
                     name="proj_bwd_w")
    dx, d_sc1, d_sh1 = _mod_bwd(ds1, dh, sv['x'], m6, 1, alpha, "mod1_bwd")
    gr['m6'] = jnp.concatenate([d_sh1, d_sc1, d_g1, d_sh2, d_sc2, d_g2], axis=0)
    return dx, gr


def _pack(arrays, width):
    flat = jnp.concatenate([a.reshape(-1) for a in arrays])
    n = flat.shape[0]
    rows = -(-n // (8 * width)) * 8
    return jnp.pad(flat, (0, rows * width - n)).reshape(rows, width), n


def _unpack(packed, like):
    flat = packed.reshape(-1)
    out, pos = [], 0
    for a in like:
        out.append(flat[pos:pos + a.size].reshape(a.shape))
        pos += a.size
    return out


def kernel(x, c, w_ada, b_ada, w_in, ssm_lam_re, ssm_lam_im, ssm_log_step, ssm_b_re, ssm_b_im, ssm_c_re, ssm_c_im, ssm_d, w_glu, b_glu, g_attn, g_ssm, w_out, ln1_g, ln1_b, w_mlp1, w_mlp2, ln2_g, ln2_b, loss_target, m_w_ada, m_b_ada, m_w_in, m_ssm_lam_re, m_ssm_lam_im, m_ssm_log_step, m_ssm_b_re, m_ssm_b_im, m_ssm_c_re, m_ssm_c_im, m_ssm_d, m_w_glu, m_b_glu, m_g_attn, m_g_ssm, m_w_out, m_ln1_g, m_ln1_b, m_w_mlp1, m_w_mlp2, m_ln2_g, m_ln2_b, v_w_ada, v_b_ada, v_w_in, v_ssm_lam_re, v_ssm_lam_im, v_ssm_log_step, v_ssm_b_re, v_ssm_b_im, v_ssm_c_re, v_ssm_c_im, v_ssm_d, v_w_glu, v_b_glu, v_g_attn, v_g_ssm, v_w_out, v_ln1_g, v_ln1_b, v_w_mlp1, v_w_mlp2, v_ln2_g, v_ln2_b):
    args = locals()
    w = {n: args[n] for n in WEIGHTS}
    mom = {n: args['m_' + n] for n in WEIGHTS}
    var = {n: args['v_' + n] for n in WEIGHTS}
    L, D, ada_cols = w_ada.shape
    S = x.shape[1]
    W = D // 2
    n_heads = W // HEAD_DIM
    alpha = (2 * L) ** 0.25
    tq = _tile(S, 256)
    ts = _tile(S, 256)
    xi, yi, ci = _me()
    shard = 2 * xi + yi
    me = 4 * xi + 2 * yi + ci

    c_all = _all_gather8(c, "gather_c").reshape(N_DEV, D)
    c16 = jnp.pad(c_all, ((0, 16 - N_DEV), (0, 0)))
    b_sh = lax.dynamic_slice_in_dim(b_ada, shard * ada_cols, ada_cols, axis=1)
    mods_sh = jnp.stack([_mm(c16, w_ada[l], M=16, N=ada_cols, K=D, a_tf=_silu, bias=b_sh[l:l + 1], out_dtype=F32,
                             name="ada") for l in range(L)])
    mods_all = _all_gather8(mods_sh, "gather_mods")
    mods = jnp.concatenate([mods_all[2 * j] for j in range(4)], axis=-1)
    m6_all = lax.dynamic_index_in_dim(mods, me, axis=1, keepdims=False).reshape(L, 6, D)

    (a_re, a_im, bb_re, bb_im), disc_vjp = jax.vjp(
        _s5_discretize, ssm_lam_re, ssm_lam_im, ssm_log_step, ssm_b_re, ssm_b_im)
    N = a_re.shape[2] * a_re.shape[3]
    per_layer = dict(
        m6=m6_all,
        a=jnp.stack([a_re.reshape(L, 2, N // SB, SB), a_im.reshape(L, 2, N // SB, SB)], axis=3),
        b_re=jax.vmap(_to_blocks)(jnp.swapaxes(bb_re, -1, -2)).astype(BF16),
        b_im=jax.vmap(_to_blocks)(jnp.swapaxes(bb_im, -1, -2)).astype(BF16),
        c_re=jax.vmap(_to_blocks)(ssm_c_re).astype(BF16),
        c_im=jax.vmap(_to_blocks)(ssm_c_im).astype(BF16),
        ssm_d=ssm_d[:, None], b_glu=b_glu[:, None], g_attn=g_attn[:, None], g_ssm=g_ssm[:, None],
        ln1_g=ln1_g[:, None], ln1_b=ln1_b[:, None], ln2_g=ln2_g[:, None], ln2_b=ln2_b[:, None],
        w_in=w_in, w_glu=w_glu, w_out=w_out, w_mlp1=w_mlp1, w_mlp2=w_mlp2)

    layers = [{n: v[l] for n, v in per_layer.items()} for l in range(L)]
    xc, saved = x[0], []
    for lw in layers:
        xc, sv = _forward_layer(xc, lw, alpha, n_heads, tq, ts)
        saved.append(sv)
    dx, sq = _loss_head(xc, loss_target[0])
    loss = lax.psum(0.5 * jnp.sum(sq) / D, ("x", "y", "c"))

    per_layer_grads = [None] * L
    for l in reversed(range(L)):
        dx, per_layer_grads[l] = _backward_layer(dx, layers[l], saved[l], alpha, n_heads, tq, ts)
    gr = {n: jnp.stack([g[n] for g in per_layer_grads]) for n in per_layer_grads[0]}

    grads = {}
    for n in BIG:
        R, C = w[n].shape[1:]
        g = gr[n].reshape(2, L // 2, 4, R, C)
        land = _scatter8(g, "scatter_" + n)
        part = _sum_leading(land.reshape(N_DEV, (L // 2) * R, C), "reduce_" + n)
        grads[n] = _swap_c(part, "swap_" + n).reshape(w[n].shape)

    dm_all = _all_gather8(gr['m6'].reshape(L, 6 * D), "gather_dmods")
    grads['b_ada'] = _sum_leading(jnp.pad(dm_all, ((0, 0), (0, 8 - L), (0, 0))), "reduce_b_ada")[:L]
    dm16 = jnp.pad(lax.dynamic_slice_in_dim(dm_all, shard * ada_cols, ada_cols, axis=2), ((0, 16 - N_DEV), (0, 0), (0, 0)))
    grads['w_ada'] = jnp.stack([_mm(c16, dm16[:, l], M=D, N=ada_cols, K=16, ta=True, a_tf=_silu, out_dtype=F32,
                                    name="ada_bwd_w") for l in range(L)])

    small_names = ['a', 'bb_re', 'bb_im', 'cc_re', 'cc_im', 'ssm_d', 'b_glu', 'g_attn', 'g_ssm',
                   'ln1_g', 'ln1_b', 'ln2_g', 'ln2_b']
    for n in ['bb_re', 'bb_im', 'cc_re', 'cc_im']:
        gr[n] = jax.vmap(_from_blocks)(gr[n])
    packed, _ = _pack([gr[n] for n in small_names], 1024)
    total = _sum_leading(_all_gather8(packed, "gather_small"), "reduce_small")
    red = dict(zip(small_names, _unpack(total, [gr[n] for n in small_names])))
    G, P = ssm_lam_re.shape[2], ssm_lam_re.shape[3]
    d_a = jnp.swapaxes(red['a'], 2, 3).reshape(L, 2, 2, G, P)
    (grads['ssm_lam_re'], grads['ssm_lam_im'], grads['ssm_log_step'], grads['ssm_b_re'],
     grads['ssm_b_im']) = disc_vjp((d_a[:, :, 0], d_a[:, :, 1], jnp.swapaxes(red['bb_re'], -1, -2),
                                    jnp.swapaxes(red['bb_im'], -1, -2)))
    grads['ssm_c_re'] = red['cc_re']
    grads['ssm_c_im'] = -red['cc_im']
    for n in ['ssm_d', 'b_glu', 'g_attn', 'g_ssm', 'ln1_g', 'ln1_b', 'ln2_g', 'ln2_b']:
        grads[n] = red[n].reshape(w[n].shape)

    delta, new_m, new_v = {}, {}, {}
    for n in ['w_ada'] + BIG:
        two_d = lambda t: t.reshape(-1, t.shape[-1])
        d_, m_, v_ = _adamw(two_d(w[n]), two_d(grads[n]), two_d(mom[n]), two_d(var[n]), "adamw_" + n)
        delta[n], new_m[n], new_v[n] = d_.reshape(w[n].shape), m_.reshape(w[n].shape), v_.reshape(w[n].shape)
    pk = lambda d: _pack([d[n] for n in SMALL], 1024)[0]
    d_, m_, v_ = _adamw(pk(w), pk(grads), pk(mom), pk(var), "adamw_small")
    like = [w[n] for n in SMALL]
    for n, dn, mn, vn in zip(SMALL, _unpack(d_, like), _unpack(m_, like), _unpack(v_, like)):
        delta[n], new_m[n], new_v[n] = dn, mn, vn

    return (loss, dx[None], *[grads[n] for n in WEIGHTS], *[delta[n] for n in WEIGHTS],
            *[new_m[n] for n in WEIGHTS], *[new_v[n] for n in WEIGHTS])
```

```python
import functools
import math

import jax
import jax.numpy as jnp
from jax import lax
from jax.experimental import pallas as pl
from jax.experimental.pallas import tpu as pltpu

F32 = jnp.float32
BF16 = jnp.bfloat16
MESH = pl.DeviceIdType.MESH
ANY = pl.BlockSpec(memory_space=pl.ANY)

HEAD_DIM = 64
SSM_GROUP = 16
SSM_STATE = 64
GROUPS_PER_BLOCK = 8
ATTN_PATTERNS = ((128, 1), (512, 4), (2048, 16))
LN_EPS = 1e-5
NEG_BIG = -1e30
ADAM_LR, ADAM_B1, ADAM_B2, ADAM_EPS, ADAM_WD, ADAM_STEP = 0.001, 0.9, 0.999, 1e-08, 0.01, 10
VMEM_LIMIT_V7X = 56 * 1024 * 1024
N_DEV = 8

WEIGHTS = ['w_ada', 'b_ada', 'w_in', 'ssm_lam_re', 'ssm_lam_im', 'ssm_log_step', 'ssm_b_re', 'ssm_b_im',
           'ssm_c_re', 'ssm_c_im', 'ssm_d', 'w_glu', 'b_glu', 'g_attn', 'g_ssm', 'w_out', 'ln1_g', 'ln1_b',
           'w_mlp1', 'w_mlp2', 'ln2_g', 'ln2_b']
BIG = ['w_in', 'w_glu', 'w_out', 'w_mlp1', 'w_mlp2']
SMALL = [n for n in WEIGHTS if n not in BIG and n != 'w_ada']


def _params(n_grid):
    return pltpu.CompilerParams(dimension_semantics=("arbitrary",) * n_grid, vmem_limit_bytes=VMEM_LIMIT_V7X)


def _tile(n, cap):
    t = 1 << (max(1, min(n, cap)).bit_length() - 1)
    while n % t:
        t //= 2
    return t


def _operand_spec(stack, transposed, tr, tc, nrb, ncb, pick):
    def index(i, j, k):
        r, c = pick(i, j, k)
        s = None
        if stack == 'r':
            s, r = r // nrb, r % nrb
        elif stack == 'c':
            s, c = c // ncb, c % ncb
        idx = (c, r) if transposed else (r, c)
        return idx if s is None else (s,) + idx
    blk = (tc, tr) if transposed else (tr, tc)
    return pl.BlockSpec(blk if stack is None else (None,) + blk, index)


def _mm(a, b, *, M, N, K, name, out_dtype, ta=False, tb=False, a_st=None, b_st=None, o_st=None, ns=1,
        a_tf=None, epi=None, extra=None, bias=None, cap_m=1024, cap_n=1024, cap_k=1024):
    pm = M // ns if 'm' in (a_st, o_st) else M
    pn = N // ns if 'n' in (b_st, o_st) else N
    pk = K // ns if 'k' in (a_st, b_st) else K
    tm, tn, tk = _tile(pm, cap_m), _tile(pn, cap_n), _tile(pk, cap_k)
    gm, gn, gk = M // tm, N // tn, K // tk
    nmb, nnb, nkb = pm // tm, pn // tn, pk // tk
    a_spec = _operand_spec({None: None, 'm': 'r', 'k': 'c'}[a_st], ta, tm, tk, nmb, nkb, lambda i, j, k: (i, k))
    b_spec = _operand_spec({None: None, 'k': 'r', 'n': 'c'}[b_st], tb, tk, tn, nkb, nnb, lambda i, j, k: (k, j))
    o_spec = _operand_spec({None: None, 'm': 'r', 'n': 'c'}[o_st], False, tm, tn, nmb, nnb, lambda i, j, k: (i, j))
    dn = (((0 if ta else 1,), (1 if tb else 0,)), ((), ()))
    ins, specs = [a, b], [a_spec, b_spec]
    if extra is not None:
        ins.append(extra)
        specs.append(o_spec)
    if bias is not None:
        ins.append(bias)
        specs.append(pl.BlockSpec((1, tn), lambda i, j, k: (0, j)))
    n_in = len(ins)
    if o_st is None:
        o_shape = (M, N)
    elif o_st == 'm':
        o_shape = (ns, pm, N)
    else:
        o_shape = (ns, M, pn)

    def body(*refs):
        a_ref, b_ref = refs[0], refs[1]
        o_ref, acc = refs[n_in], refs[n_in + 1]
        k = pl.program_id(2)

        @pl.when(k == 0)
        def _():
            acc[...] = jnp.zeros_like(acc)

        av = a_ref[...]
        if a_tf is not None:
            av = a_tf(av.astype(F32))
        acc[...] += lax.dot_general(av.astype(BF16), b_ref[...].astype(BF16), dn, preferred_element_type=F32)

        @pl.when(k == gk - 1)
        def _():
            r = acc[...]
            pos = 2
            if extra is not None:
                r = epi(r, refs[pos][...].astype(F32))
                pos += 1
            elif epi is not None:
                r = epi(r)
            if bias is not None:
                r = r + refs[pos][...]
            o_ref[...] = r.astype(o_ref.dtype)

    return pl.pallas_call(
        body, grid=(gm, gn, gk), in_specs=specs, out_specs=o_spec,
        out_shape=jax.ShapeDtypeStruct(o_shape, out_dtype),
        scratch_shapes=[pltpu.VMEM((tm, tn), F32)], compiler_params=_params(3), name=name)(*ins)


def _relu2(v):
    r = jnp.maximum(v, 0.0)
    return r * r


def _silu(v):
    return v / (1.0 + jnp.exp(-v))


def _rb(tr, w):
    return pl.BlockSpec((tr, w), lambda i: (i, 0))


def _pb(r, w):
    return pl.BlockSpec((r, w), lambda i: (0, 0))


def _row_call(body, n_rows, tr, ins, in_specs, outs, out_specs, name):
    return pl.pallas_call(body, grid=(n_rows // tr,), in_specs=in_specs, out_specs=out_specs,
                          out_shape=outs, compiler_params=_params(1), name=name)(*ins)


def _accumulate(ref, value):
    @pl.when(pl.program_id(0) == 0)
    def _():
        ref[...] = jnp.zeros_like(ref)
    ref[...] += jnp.sum(value, axis=0, keepdims=True)


def _modulate(x, m6, row_shift, row_scale, name):
    S, D = x.shape
    tr = _tile(S, 256)

    def body(x_ref, m_ref, h_ref):
        sh = m_ref[row_shift:row_shift + 1, :]
        sc = m_ref[row_scale:row_scale + 1, :]
        h_ref[...] = (x_ref[...] * (1.0 + sc) + sh).astype(BF16)

    return _row_call(body, S, tr, [x, m6], [_rb(tr, D), _pb(6, D)],
                     jax.ShapeDtypeStruct((S, D), BF16), _rb(tr, D), name)


def _gelu(v):
    c = math.sqrt(2.0 / math.pi)
    return 0.5 * v * (1.0 + jnp.tanh(c * (v + 0.044715 * v * v * v)))


def _gelu_grad(v):
    c = math.sqrt(2.0 / math.pi)
    t = jnp.tanh(c * (v + 0.044715 * v * v * v))
    return 0.5 * (1.0 + t) + 0.5 * v * (1.0 - t * t) * c * (1.0 + 3.0 * 0.044715 * v * v)


def _ssm_act(u, y0, y1, dskip):
    S, W = u.shape
    tr = _tile(S, 256)

    def body(u_ref, y0_ref, y1_ref, d_ref, pre_ref, g_ref):
        pre = u_ref[...] * d_ref[...] + y0_ref[...] + y1_ref[...]
        pre_ref[...] = pre
        g_ref[...] = _gelu(pre)

    return _row_call(body, S, tr, [u, y0, y1, dskip], [_rb(tr, W)] * 3 + [_pb(1, W)],
                     [jax.ShapeDtypeStruct((S, W), F32)] * 2, [_rb(tr, W)] * 2, "ssm_act")


def _merge(attn, g, z, b_glu, g_attn, g_ssm):
    S, W = attn.shape
    tr = _tile(S, 256)

    def body(a_ref, g_ref, z_ref, b_ref, ga_ref, gs_ref, o_ref):
        a = a_ref[...]
        ra = lax.rsqrt(jnp.mean(a * a, axis=-1, keepdims=True) + LN_EPS)
        o_ref[:, 0:W] = (a * ra * ga_ref[...]).astype(BF16)
        s = g_ref[...] / (1.0 + jnp.exp(-(z_ref[...] + b_ref[...])))
        rs = lax.rsqrt(jnp.mean(s * s, axis=-1, keepdims=True) + LN_EPS)
        o_ref[:, W:2 * W] = (s * rs * gs_ref[...]).astype(BF16)

    return _row_call(body, S, tr, [attn, g, z, b_glu, g_attn, g_ssm],
                     [_rb(tr, W)] * 3 + [_pb(1, W)] * 3,
                     jax.ShapeDtypeStruct((S, 2 * W), BF16), _rb(tr, 2 * W), "merge")


def _post_ln(x, branch, m6, row_gate, ln_g, ln_b, alpha, name):
    S, D = x.shape
    tr = _tile(S, 256)

    def body(x_ref, br_ref, m_ref, g_ref, b_ref, o_ref):
        gate = m_ref[row_gate:row_gate + 1, :]
        s = alpha * x_ref[...] + (1.0 + gate) * br_ref[...]
        mu = jnp.mean(s, axis=-1, keepdims=True)
        d = s - mu
        var = jnp.mean(d * d, axis=-1, keepdims=True)
        o_ref[...] = d * lax.rsqrt(var + LN_EPS) * g_ref[...] + b_ref[...]

    return _row_call(body, S, tr, [x, branch, m6, ln_g, ln_b],
                     [_rb(tr, D), _rb(tr, D), _pb(6, D), _pb(1, D), _pb(1, D)],
                     jax.ShapeDtypeStruct((S, D), F32), _rb(tr, D), name)


def _loss_head(x, target):
    S, D = x.shape
    tr = _tile(S, 256)

    def body(x_ref, t_ref, dx_ref, acc_ref):
        e = x_ref[...] - t_ref[...]
        dx_ref[...] = e * (1.0 / D)
        _accumulate(acc_ref, e * e)

    return _row_call(body, S, tr, [x, target], [_rb(tr, D)] * 2,
                     [jax.ShapeDtypeStruct((S, D), F32), jax.ShapeDtypeStruct((1, D), F32)],
                     [_rb(tr, D), _pb(1, D)], "loss_head")


def _ln_bwd(dxo, x_in, branch, m6, row_gate, ln_g, alpha, name):
    S, D = dxo.shape
    tr = _tile(S, 256)

    def body(dxo_ref, x_ref, br_ref, m_ref, g_ref, ds_ref, dbr_ref, dg_ref, db_ref, dgate_ref):
        gate = m_ref[row_gate:row_gate + 1, :]
        br = br_ref[...]
        s = alpha * x_ref[...] + (1.0 + gate) * br
        mu = jnp.mean(s, axis=-1, keepdims=True)
        d = s - mu
        var = jnp.mean(d * d, axis=-1, keepdims=True)
        rstd = lax.rsqrt(var + LN_EPS)
        xhat = d * rstd
        dxo = dxo_ref[...]
        dxh = dxo * g_ref[...]
        ds = rstd * (dxh - jnp.mean(dxh, axis=-1, keepdims=True)
                     - xhat * jnp.mean(dxh * xhat, axis=-1, keepdims=True))
        ds_ref[...] = ds
        dbr_ref[...] = ((1.0 + gate) * ds).astype(BF16)
        _accumulate(dg_ref, dxo * xhat)
        _accumulate(db_ref, dxo)
        _accumulate(dgate_ref, ds * br)

    vec = jax.ShapeDtypeStruct((1, D), F32)
    return _row_call(body, S, tr, [dxo, x_in, branch, m6, ln_g],
                     [_rb(tr, D)] * 3 + [_pb(6, D), _pb(1, D)],
                     [jax.ShapeDtypeStruct((S, D), F32), jax.ShapeDtypeStruct((S, D), BF16), vec, vec, vec],
                     [_rb(tr, D), _rb(tr, D), _pb(1, D), _pb(1, D), _pb(1, D)], name)


def _mod_bwd(ds, dh, x_in, m6, row_scale, alpha, name):
    S, D = ds.shape
    tr = _tile(S, 256)

    def body(ds_ref, dh_ref, x_ref, m_ref, dx_ref, dsc_ref, dsh_ref):
        sc = m_ref[row_scale:row_scale + 1, :]
        dh = dh_ref[...]
        dx_ref[...] = alpha * ds_ref[...] + dh * (1.0 + sc)
        _accumulate(dsc_ref, dh * x_ref[...])
        _accumulate(dsh_ref, dh)

    vec = jax.ShapeDtypeStruct((1, D), F32)
    return _row_call(body, S, tr, [ds, dh, x_in, m6], [_rb(tr, D)] * 3 + [_pb(6, D)],
                     [jax.ShapeDtypeStruct((S, D), F32), vec, vec],
                     [_rb(tr, D), _pb(1, D), _pb(1, D)], name)


def _merge_bwd(dmerged, attn, g, z, b_glu, g_attn, g_ssm):
    S, W = attn.shape
    tr = _tile(S, 256)

    def rms_bwd(x, gamma, dy):
        r = lax.rsqrt(jnp.mean(x * x, axis=-1, keepdims=True) + LN_EPS)
        gdy = gamma * dy
        dx = gdy * r - x * (r * r * r) * jnp.mean(gdy * x, axis=-1, keepdims=True)
        return dx, dy * x * r

    def body(dm_ref, a_ref, g_ref, z_ref, b_ref, ga_ref, gs_ref, da_ref, dgp_ref, dz_ref, dga_ref, dgs_ref, db_ref):
        da, dga = rms_bwd(a_ref[...], ga_ref[...], dm_ref[:, 0:W])
        da_ref[...] = da
        _accumulate(dga_ref, dga)
        gv = g_ref[...]
        sig = 1.0 / (1.0 + jnp.exp(-(z_ref[...] + b_ref[...])))
        dssm, dgs = rms_bwd(gv * sig, gs_ref[...], dm_ref[:, W:2 * W])
        _accumulate(dgs_ref, dgs)
        dgp_ref[...] = dssm * sig
        dz = dssm * gv * sig * (1.0 - sig)
        dz_ref[...] = dz.astype(BF16)
        _accumulate(db_ref, dz)

    vec = jax.ShapeDtypeStruct((1, W), F32)
    mat = jax.ShapeDtypeStruct((S, W), F32)
    return _row_call(body, S, tr, [dmerged, attn, g, z, b_glu, g_attn, g_ssm],
                     [_rb(tr, 2 * W)] + [_rb(tr, W)] * 3 + [_pb(1, W)] * 3,
                     [mat, mat, jax.ShapeDtypeStruct((S, W), BF16), vec, vec, vec],
                     [_rb(tr, W)] * 3 + [_pb(1, W)] * 3, "merge_bwd")


def _act_bwd(dgp, dgl, pre, u, dskip):
    S, W = pre.shape
    tr = _tile(S, 256)

    def body(a_ref, b_ref, pre_ref, u_ref, d_ref, dy_ref, du_ref, dd_ref):
        dy = (a_ref[...] + b_ref[...]) * _gelu_grad(pre_ref[...])
        dy_ref[...] = dy
        du_ref[...] = dy * d_ref[...]
        _accumulate(dd_ref, dy * u_ref[...])

    mat = jax.ShapeDtypeStruct((S, W), F32)
    return _row_call(body, S, tr, [dgp, dgl, pre, u, dskip], [_rb(tr, W)] * 4 + [_pb(1, W)],
                     [mat, mat, jax.ShapeDtypeStruct((1, W), F32)], [_rb(tr, W)] * 2 + [_pb(1, W)], "act_bwd")


def _dproj(dq, dkT, dvT, du_skip, du0, du1, reach):
    S, W = dq.shape
    tr = _tile(S, 256)
    lanes = 2 * HEAD_DIM
    n_pairs, nkb = W // lanes, tr // lanes
    assert tr % lanes == 0 and reach % tr == 0

    def body(dq_ref, dk_ref, dv_ref, a_ref, b_ref, c_ref, o_ref):
        o_ref[0] = dq_ref[...].astype(BF16)
        for h in range(n_pairs):
            for b in range(nkb):
                rows, cols = slice(b * lanes, (b + 1) * lanes), slice(h * lanes, (h + 1) * lanes)
                o_ref[1, rows, cols] = dk_ref[h, b].T.astype(BF16)
                o_ref[2, rows, cols] = dv_ref[h, b].T.astype(BF16)
        o_ref[3] = (a_ref[...] + b_ref[...] + c_ref[...]).astype(BF16)

    tsp = pl.BlockSpec((n_pairs, nkb, lanes, lanes), lambda i: (0, i + reach // tr, 0, 0))
    return _row_call(body, S, tr, [dq, dkT, dvT, du_skip, du0, du1],
                     [_rb(tr, W), tsp, tsp, _rb(tr, W), _rb(tr, W), _rb(tr, W)],
                     jax.ShapeDtypeStruct((4, S, W), BF16), pl.BlockSpec((4, tr, W), lambda i: (0, i, 0)), "dproj")


def _attn_reach():
    return max(w // 2 for w, _ in ATTN_PATTERNS)


def _bias_table(tq, width, reach, head):
    i = lax.broadcasted_iota(jnp.int32, (tq, width), 0)
    j = lax.broadcasted_iota(jnp.int32, (tq, width), 1)
    ad = jnp.abs(j - i - reach)
    mult = jnp.zeros((tq, width), jnp.int32)
    for window, dil in ATTN_PATTERNS:
        assert dil & (dil - 1) == 0
        mult += ((jnp.bitwise_and(ad, dil - 1) == 0) & (ad <= window // 2)).astype(jnp.int32)
    logm = jnp.zeros((tq, width), F32)
    for n in range(2, len(ATTN_PATTERNS) + 1):
        logm = jnp.where(mult == n, math.log(n), logm)
    return logm, ad.astype(F32), mult > 0


def _attn_common(n_heads, q_ref, k_ref, v_ref, bias_ref, tq, reach, seq):
    width = tq + 2 * reach
    hp, qi = pl.program_id(0), pl.program_id(1)

    @pl.when(qi == 0)
    def _():
        logm, ad, ok = _bias_table(tq, width, reach, None)
        for a in range(2):
            head = (2 * hp + a + 1).astype(F32)
            slope = jnp.exp(jnp.full((tq, width), -8.0 * math.log(2.0) / n_heads, F32) * head)
            bias_ref[a] = jnp.where(ok, logm - slope * ad, NEG_BIG)

    start = pl.multiple_of(qi * tq, tq)
    kw = k_ref[pl.ds(start, width), :]
    vw = v_ref[pl.ds(start, width), :]
    kpos = qi * tq - reach + lax.broadcasted_iota(jnp.int32, (1, width), 1)
    vbias = jnp.where((kpos >= 0) & (kpos < seq), 0.0, NEG_BIG).astype(F32)
    lane = lax.broadcasted_iota(jnp.int32, (1, 2 * HEAD_DIM), 1)
    masks = [lane < HEAD_DIM, lane >= HEAD_DIM]
    return start, kw, vw, vbias, masks


_NT = (((1,), (1,)), ((), ()))
assert math.log2(HEAD_DIM) % 2 == 0


def _attention_call(body, grid, in_specs, out_specs, out_shape, scratch, name, ins, riders):
    if riders:
        first = lambda: (pl.program_id(0) == 0) & (pl.program_id(1) == 0)
        last = lambda: (pl.program_id(0) == grid[0] - 1) & (pl.program_id(1) == grid[1] - 1)
        body = _hosted(body, len(in_specs), len(out_specs), len(scratch), riders, first, last)
        name += "_carrier"
    rider_ins = [a for t in riders for a in t.ins]
    res = pl.pallas_call(
        body, grid=grid, in_specs=in_specs + [ANY] * len(rider_ins), out_specs=out_specs + [ANY] * len(riders),
        out_shape=out_shape + [t.out_shape for t in riders],
        scratch_shapes=scratch + [s for t in riders for s in t.semaphores()],
        compiler_params=_params(2), name=name)(*ins, *rider_ins)
    return res[:len(out_specs)], res[len(out_specs):]


def _attention_fwd(qkv, kp, vp, n_heads, tq, riders=()):
    _, S, W = qkv.shape
    reach = _attn_reach()
    width = tq + 2 * reach
    Sp = S + 2 * reach
    scale = HEAD_DIM ** -0.5
    lanes = 2 * HEAD_DIM

    def body(q_ref, k_ref, v_ref, o_ref, lse_ref, bias_ref):
        _, kw, vw, vbias, masks = _attn_common(n_heads, q_ref, k_ref, v_ref, bias_ref, tq, reach, S)
        q = q_ref[...] * scale
        out = jnp.zeros((tq, lanes), F32)
        lse = jnp.zeros((tq, lanes), F32)
        for a in range(2):
            qa = jnp.where(masks[a], q, jnp.zeros_like(q))
            s = lax.dot_general(qa, kw, _NT, preferred_element_type=F32) + bias_ref[a] + vbias
            m = jnp.max(s, axis=-1, keepdims=True)
            p = jnp.exp(s - m)
            l = jnp.sum(p, axis=-1, keepdims=True)
            o = jnp.dot(p.astype(BF16), vw, preferred_element_type=F32) / l
            out = jnp.where(masks[a], o, out)
            lse = jnp.where(masks[a], m + jnp.log(l), lse)
        o_ref[...] = out
        lse_ref[...] = lse

    qsp = pl.BlockSpec((None, tq, lanes), lambda h, i: (0, i, h))
    ksp = pl.BlockSpec((Sp, lanes), lambda h, i: (0, h))
    osp = pl.BlockSpec((tq, lanes), lambda h, i: (i, h))
    mat = jax.ShapeDtypeStruct((S, W), F32)
    return _attention_call(body, (W // lanes, S // tq), [qsp, ksp, ksp], [osp, osp], [mat, mat],
                           [pltpu.VMEM((2, tq, width), F32)], "attn_fwd", (qkv, kp, vp), list(riders))


def _attention_bwd(qkv, kp, vp, out, lse, dout, n_heads, tq, riders=()):
    _, S, W = qkv.shape
    reach = _attn_reach()
    width = tq + 2 * reach
    Sp = S + 2 * reach
    scale = HEAD_DIM ** -0.5
    lanes = 2 * HEAD_DIM

    def body(q_ref, k_ref, v_ref, o_ref, lse_ref, do_ref, dq_ref, dk_ref, dv_ref, bias_ref):
        start, kw, vw, vbias, masks = _attn_common(n_heads, q_ref, k_ref, v_ref, bias_ref, tq, reach, S)

        @pl.when(pl.program_id(1) == 0)
        def _():
            dk_ref[...] = jnp.zeros_like(dk_ref)
            dv_ref[...] = jnp.zeros_like(dv_ref)

        q = q_ref[...] * scale
        do = do_ref[...]
        prod = do * o_ref[...]
        lse_all = lse_ref[...]
        dq = jnp.zeros((tq, lanes), F32)
        dkt = jnp.zeros((lanes, width), F32)
        dvt = jnp.zeros((lanes, width), F32)
        for a in range(2):
            qa = jnp.where(masks[a], q, jnp.zeros_like(q))
            doa = jnp.where(masks[a], do, 0.0)
            s = lax.dot_general(qa, kw, _NT, preferred_element_type=F32) + bias_ref[a] + vbias
            lse_a = lse_all[:, a * HEAD_DIM:a * HEAD_DIM + 1]
            p = jnp.exp(s - lse_a)
            delta = jnp.sum(jnp.where(masks[a], prod, 0.0), axis=-1, keepdims=True)
            dp = lax.dot_general(doa.astype(BF16), vw, _NT, preferred_element_type=F32)
            ds = (p * (dp - delta)).astype(BF16)
            dvt += jnp.dot(doa.T.astype(BF16), p.astype(BF16), preferred_element_type=F32)
            dkt += jnp.dot(qa.astype(F32).T.astype(BF16), ds, preferred_element_type=F32)
            dq = jnp.where(masks[a], jnp.dot(ds, kw, preferred_element_type=F32) * scale, dq)
        dq_ref[...] = dq
        first = pl.program_id(1) * (tq // lanes)
        for j in range(width // lanes):
            dk_ref[first + j] += dkt[:, j * lanes:(j + 1) * lanes]
            dv_ref[first + j] += dvt[:, j * lanes:(j + 1) * lanes]

    assert tq % lanes == 0 and reach % lanes == 0
    qsp = pl.BlockSpec((None, tq, lanes), lambda h, i: (0, i, h))
    ksp = pl.BlockSpec((Sp, lanes), lambda h, i: (0, h))
    osp = pl.BlockSpec((tq, lanes), lambda h, i: (i, h))
    tsp = pl.BlockSpec((None, Sp // lanes, lanes, lanes), lambda h, i: (h, 0, 0, 0))
    tiles = jax.ShapeDtypeStruct((W // lanes, Sp // lanes, lanes, lanes), F32)
    return _attention_call(body, (W // lanes, S // tq), [qsp, ksp, ksp, osp, osp, osp], [osp, tsp, tsp],
                           [jax.ShapeDtypeStruct((S, W), F32), tiles, tiles],
                           [pltpu.VMEM((2, tq, width), F32)], "attn_bwd", (qkv, kp, vp, out, lse, dout),
                           list(riders))


SB = GROUPS_PER_BLOCK * SSM_STATE
CB = GROUPS_PER_BLOCK * SSM_GROUP
RUNS = 8


def _powers(pw, a_r, a_i, n):
    p_r, p_i = a_r, a_i
    for j in range(n):
        pw[0, j:j + 1, :] = p_r
        pw[1, j:j + 1, :] = p_i
        p_r, p_i = p_r * a_r - p_i * a_i, p_r * a_i + p_i * a_r


def _rows8(j):
    return slice(j * RUNS, (j + 1) * RUNS)


def _run_scan(br, bi, T, a_r, a_i, pw, e_r, e_i, ent, down, conj):
    n = T // RUNS
    sg = -1.0 if conj else 1.0
    A_r = jnp.broadcast_to(a_r, (RUNS, SB))
    A_i = jnp.broadcast_to(sg * a_i, (RUNS, SB))
    x_r = x_i = None
    for j in (range(n - 1, -1, -1) if down else range(n)):
        b_r, b_i = br[_rows8(j), :], bi[_rows8(j), :]
        if x_r is None:
            x_r, x_i = b_r, b_i
        else:
            x_r, x_i = A_r * x_r - A_i * x_i + b_r, A_r * x_i + A_i * x_r + b_i
            br[_rows8(j), :] = x_r
            bi[_rows8(j), :] = x_i
    al_r, al_i = pw[0, n - 1:n, :], sg * pw[1, n - 1:n, :]
    s_r, s_i = e_r, e_i
    for c in (range(RUNS - 1, -1, -1) if down else range(RUNS)):
        ent[0, c:c + 1, :] = s_r
        ent[1, c:c + 1, :] = s_i
        s_r, s_i = (al_r * s_r - al_i * s_i + x_r[c:c + 1, :], al_r * s_i + al_i * s_r + x_i[c:c + 1, :])
    E_r, E_i = ent[0], ent[1]
    for j in range(n):
        p = n - 1 - j if down else j
        p_r, p_i = pw[0, p:p + 1, :], sg * pw[1, p:p + 1, :]
        v_r, v_i = br[_rows8(j), :], bi[_rows8(j), :]
        br[_rows8(j), :] = v_r + p_r * E_r - p_i * E_i
        bi[_rows8(j), :] = v_i + p_r * E_i + p_i * E_r
    return s_r, s_i


def _load_run_major(ref, T):
    n = T // RUNS
    return jnp.concatenate([ref[pl.ds(j, RUNS, stride=n), :] for j in range(n)], axis=0)


def _store_row_major(ref, value, T):
    n = T // RUNS
    for j in range(n):
        ref[pl.ds(j, RUNS, stride=n), :] = value[_rows8(j), :]


def _s5_specs(nb, T, rev, adjoint):
    flip = rev != adjoint
    blk = (lambda i: nb - 1 - i) if flip else (lambda i: i)
    usp = pl.BlockSpec((T, CB), lambda k, i: (blk(i), k))
    asp = pl.BlockSpec((None, 2, SB), lambda k, i: (k, 0, 0))
    wsp = pl.BlockSpec((None, CB, SB), lambda k, i: (k, 0, 0))
    csp = pl.BlockSpec((None, None, 2, SB), lambda k, i: (k, blk(i), 0, 0))
    return usp, asp, wsp, csp


def _s5_fwd(u, a, b_re, b_im, c_re, c_im, T, rev):
    S, W = u.shape
    NK = W // CB
    nb = S // T
    usp, asp, wsp, csp = _s5_specs(nb, T, rev, False)

    def body(u_ref, a_ref, bre_ref, bim_ref, cre_ref, cim_ref, y_ref, car_ref, wr, wi, st, pw, ent):
        a_r, a_i = a_ref[0:1, :], a_ref[1:2, :]

        @pl.when(pl.program_id(1) == 0)
        def _():
            st[...] = jnp.zeros_like(st)
            _powers(pw, a_r, a_i, T // RUNS)

        car_ref[...] = st[0:2, :]
        ub = _load_run_major(u_ref, T).astype(BF16)
        wr[...] = jnp.dot(ub, bre_ref[...], preferred_element_type=F32)
        wi[...] = jnp.dot(ub, bim_ref[...], preferred_element_type=F32)
        s_r, s_i = _run_scan(wr, wi, T, a_r, a_i, pw, st[0:1, :], st[1:2, :], ent, rev, False)
        st[0:1, :] = s_r
        st[1:2, :] = s_i
        y = (lax.dot_general(wr[...].astype(BF16), cre_ref[...], _NT, preferred_element_type=F32)
             - lax.dot_general(wi[...].astype(BF16), cim_ref[...], _NT, preferred_element_type=F32))
        _store_row_major(y_ref, y, T)

    return pl.pallas_call(
        body, grid=(NK, nb), in_specs=[usp, asp, wsp, wsp, wsp, wsp], out_specs=[usp, csp],
        out_shape=[jax.ShapeDtypeStruct((S, W), F32), jax.ShapeDtypeStruct((NK, nb, 2, SB), F32)],
        scratch_shapes=[pltpu.VMEM((T, SB), F32), pltpu.VMEM((T, SB), F32), pltpu.VMEM((8, SB), F32),
                        pltpu.VMEM((2, T // RUNS, SB), F32), pltpu.VMEM((2, RUNS, SB), F32)],
        compiler_params=_params(2), name="s5_fwd_rev" if rev else "s5_fwd")(u, a, b_re, b_im, c_re, c_im)


def _s5_bwd(u, dy, a, b_re, b_im, c_re, c_im, carries, T, rev):
    S, W = u.shape
    NK = W // CB
    nb = S // T
    usp, asp, wsp, csp = _s5_specs(nb, T, rev, True)
    n = T // RUNS

    def body(u_ref, dy_ref, a_ref, bre_ref, bim_ref, cre_ref, cim_ref, car_ref,
             du_ref, dbre_ref, dbim_ref, dcre_ref, dcim_ref, da_ref, wr, wi, gr, gi, lam, pw, ent):
        a_r, a_i = a_ref[0:1, :], a_ref[1:2, :]

        @pl.when(pl.program_id(1) == 0)
        def _():
            lam[...] = jnp.zeros_like(lam)
            for r in (dbre_ref, dbim_ref, dcre_ref, dcim_ref, da_ref):
                r[...] = jnp.zeros_like(r)
            _powers(pw, a_r, a_i, n)

        u32 = _load_run_major(u_ref, T)
        ub = u32.astype(BF16)
        dyk = _load_run_major(dy_ref, T)
        dyb = dyk.astype(BF16)
        wr[...] = jnp.dot(ub, bre_ref[...], preferred_element_type=F32)
        wi[...] = jnp.dot(ub, bim_ref[...], preferred_element_type=F32)
        x0r, x0i = car_ref[0:1, :], car_ref[1:2, :]
        _run_scan(wr, wi, T, a_r, a_i, pw, x0r, x0i, ent, rev, False)

        gr[...] = jnp.dot(dyb, cre_ref[...], preferred_element_type=F32)
        gi[...] = -jnp.dot(dyb, cim_ref[...], preferred_element_type=F32)
        l_r, l_i = _run_scan(gr, gi, T, a_r, a_i, pw, lam[0:1, :], lam[1:2, :], ent, not rev, True)
        lam[0:1, :] = l_r
        lam[1:2, :] = l_i

        sub = lax.broadcasted_iota(jnp.int32, (RUNS, SB), 0)

        def before(buf, x0, j):
            if rev:
                if j < n - 1:
                    return buf[_rows8(j + 1), :]
                return jnp.where(sub == RUNS - 1, x0, pltpu.roll(buf[_rows8(0), :], RUNS - 1, 0))
            if j > 0:
                return buf[_rows8(j - 1), :]
            return jnp.where(sub == 0, x0, pltpu.roll(buf[_rows8(n - 1), :], 1, 0))

        acc_r = jnp.zeros((RUNS, SB), F32)
        acc_i = jnp.zeros((RUNS, SB), F32)
        for j in range(n):
            g_r, g_i = gr[_rows8(j), :], gi[_rows8(j), :]
            p_r, p_i = before(wr, x0r, j), before(wi, x0i, j)
            acc_r += g_r * p_r + g_i * p_i
            acc_i += g_i * p_r - g_r * p_i
        da_ref[0:1, :] += jnp.sum(acc_r, axis=0, keepdims=True)
        da_ref[1:2, :] += jnp.sum(acc_i, axis=0, keepdims=True)

        lrb, lib = gr[...].astype(BF16), gi[...].astype(BF16)
        du = (lax.dot_general(lrb, bre_ref[...], _NT, preferred_element_type=F32)
              + lax.dot_general(lib, bim_ref[...], _NT, preferred_element_type=F32))
        _store_row_major(du_ref, du, T)
        ut = u32.T.astype(BF16)
        dbre_ref[...] += jnp.dot(ut, lrb, preferred_element_type=F32)
        dbim_ref[...] += jnp.dot(ut, lib, preferred_element_type=F32)
        dyt = dyk.T.astype(BF16)
        dcre_ref[...] += jnp.dot(dyt, wr[...].astype(BF16), preferred_element_type=F32)
        dcim_ref[...] += jnp.dot(dyt, wi[...].astype(BF16), preferred_element_type=F32)

    blk = jax.ShapeDtypeStruct((NK, CB, SB), F32)
    return pl.pallas_call(
        body, grid=(NK, nb), in_specs=[usp, usp, asp, wsp, wsp, wsp, wsp, csp],
        out_specs=[usp, wsp, wsp, wsp, wsp, asp],
        out_shape=[jax.ShapeDtypeStruct((S, W), F32), blk, blk, blk, blk, jax.ShapeDtypeStruct((NK, 2, SB), F32)],
        scratch_shapes=[pltpu.VMEM((T, SB), F32), pltpu.VMEM((T, SB), F32),
                        pltpu.VMEM((T, SB), F32), pltpu.VMEM((T, SB), F32), pltpu.VMEM((8, SB), F32),
                        pltpu.VMEM((2, n, SB), F32), pltpu.VMEM((2, RUNS, SB), F32)],
        compiler_params=_params(2), name="s5_bwd_rev" if rev else "s5_bwd")(
            u, dy, a, b_re, b_im, c_re, c_im, carries)


def _s5_discretize(lam_re, lam_im, log_step, b_re, b_im):
    step = jnp.exp(log_step)[..., None]
    mag = jnp.exp(lam_re * step)
    a_re, a_im = mag * jnp.cos(lam_im * step), mag * jnp.sin(lam_im * step)
    den = lam_re * lam_re + lam_im * lam_im
    coef_re = ((a_re - 1.0) * lam_re + a_im * lam_im) / den
    coef_im = (a_im * lam_re - (a_re - 1.0) * lam_im) / den
    bb_re = coef_re[..., None] * b_re - coef_im[..., None] * b_im
    bb_im = coef_re[..., None] * b_im + coef_im[..., None] * b_re
    return a_re, a_im, bb_re, bb_im


def _to_blocks(w_gcp):
    two, G, C, P = w_gcp.shape
    nk = G // GROUPS_PER_BLOCK
    x = w_gcp.reshape(two, nk, GROUPS_PER_BLOCK, C, P)
    eye = jnp.eye(GROUPS_PER_BLOCK, dtype=w_gcp.dtype)
    return jnp.einsum('dkgcp,gh->dkgchp', x, eye).reshape(two, nk, GROUPS_PER_BLOCK * C, GROUPS_PER_BLOCK * P)


def _from_blocks(blk):
    two, nk, cb, sb = blk.shape
    C, P = cb // GROUPS_PER_BLOCK, sb // GROUPS_PER_BLOCK
    x = blk.reshape(two, nk, GROUPS_PER_BLOCK, C, GROUPS_PER_BLOCK, P)
    eye = jnp.eye(GROUPS_PER_BLOCK, dtype=blk.dtype)
    return jnp.einsum('dkgchp,gh->dkgcp', x, eye).reshape(two, nk * GROUPS_PER_BLOCK, C, P)


def _me():
    return lax.axis_index("x"), lax.axis_index("y"), lax.axis_index("c")


def _peer(k):
    x, y, c = _me()
    return (1 - x if k & 4 else x, 1 - y if k & 2 else y, 1 - c if k & 1 else c)


def _logical(dev):
    return 4 * dev[0] + 2 * dev[1] + dev[2]


class _Transfer:
    def __init__(self, ks, src_of, dst_of, out_shape, ins, split=1):
        self.ks, self.src_of, self.dst_of, self.out_shape, self.ins, self.split = ks, src_of, dst_of, out_shape, ins, split

    def semaphores(self):
        n = len(self.ks) * self.split
        return [pltpu.SemaphoreType.DMA((n,)), pltpu.SemaphoreType.DMA((n,))]

    def copies(self, in_refs, out_ref, send, recv):
        out = []
        for k in self.ks:
            src, dst = self.src_of(k, in_refs, out_ref), self.dst_of(k, in_refs, out_ref)
            rows = src.shape[0] // self.split
            for q in range(self.split):
                part = pl.ds(q * rows, rows)
                j = len(out)
                out.append(pltpu.make_async_remote_copy(
                    src_ref=src if self.split == 1 else src.at[part],
                    dst_ref=dst if self.split == 1 else dst.at[part],
                    send_sem=send.at[j], recv_sem=recv.at[j], device_id=_peer(k), device_id_type=MESH))
        return out


def _exchange(name, t):
    n_in = len(t.ins)

    def body(*refs):
        copies = t.copies(refs[:n_in], refs[n_in], *refs[n_in + 1:])
        for cp in copies:
            cp.start()
        for cp in copies:
            cp.wait_recv()
        for cp in copies:
            cp.wait_send()

    return pl.pallas_call(body, in_specs=[ANY] * n_in, out_specs=ANY, out_shape=t.out_shape,
                          scratch_shapes=t.semaphores(), name=name)(*t.ins)


def _hosted(body, n_in, n_out, n_scratch, riders, is_first, is_last):
    n_rin = sum(len(t.ins) for t in riders)

    def wrapped(*refs):
        host_in, rider_in = refs[:n_in], refs[n_in:n_in + n_rin]
        pos = n_in + n_rin
        host_out, rider_out = refs[pos:pos + n_out], refs[pos + n_out:pos + n_out + len(riders)]
        pos += n_out + len(riders)
        host_scratch, sems = refs[pos:pos + n_scratch], refs[pos + n_scratch:]

        def copies():
            out, at = [], 0
            for i, t in enumerate(riders):
                out += t.copies(rider_in[at:at + len(t.ins)], rider_out[i], sems[2 * i], sems[2 * i + 1])
                at += len(t.ins)
            return out

        @pl.when(is_first())
        def _():
            for cp in copies():
                cp.start()

        body(*host_in, *host_out, *host_scratch)

        @pl.when(is_last())
        def _():
            for cp in copies():
                cp.wait_recv()
            for cp in copies():
                cp.wait_send()

    return wrapped


def _all_gather8(v, name):
    out = jax.ShapeDtypeStruct((N_DEV,) + v.shape, v.dtype)
    slot = lambda k, ins, o: o.at[_logical(_me())]
    got = _exchange(name, _Transfer(list(range(1, 8)), lambda k, ins, o: ins[0], slot, out, [v]))
    return lax.dynamic_update_index_in_dim(got, v, _logical(_me()), 0)


def _halves_transfer(v):
    R, C = v.shape
    half = lambda ins: ins[0].at[pl.ds(lax.axis_index("c") * (R // 2), R // 2)]
    chip = lambda: 2 * lax.axis_index("x") + lax.axis_index("y")
    return _Transfer([2, 4, 6], lambda k, ins, o: half(ins), lambda k, ins, o: o.at[chip()],
                     jax.ShapeDtypeStruct((4, R // 2, C), v.dtype), [v])


def _join_halves(v, halves, name):
    R, C = v.shape
    c = lax.axis_index("c")
    mine = lax.dynamic_slice_in_dim(v, c * (R // 2), R // 2, axis=0)
    halves = lax.dynamic_update_index_in_dim(halves, mine, 2 * lax.axis_index("x") + lax.axis_index("y"), 0)
    core = lambda o: o.at[:, lax.axis_index("c")]
    both = _exchange(name + "_join", _Transfer(
        [1], lambda k, ins, o: ins[0], lambda k, ins, o: core(o),
        jax.ShapeDtypeStruct((4, 2, R // 2, C), v.dtype), [halves], split=4))
    return lax.dynamic_update_index_in_dim(both, halves, c, 1).reshape(4, R, C)


def _scatter_transfer(g):
    four, two, R2, C = g.shape
    piece = lambda dev, ins: ins[0].at[2 * dev[0] + dev[1], dev[2]]
    slot = lambda k, ins, o: o.at[_logical(_me())]
    return _Transfer(list(range(1, 8)), lambda k, ins, o: piece(_peer(k), ins), slot,
                     jax.ShapeDtypeStruct((N_DEV, R2, C), g.dtype), [g])


def _reduce_scattered(g, landed, name):
    x, y, c = _me()
    mine = lax.dynamic_index_in_dim(lax.dynamic_index_in_dim(g, 2 * x + y, 0, False), c, 0, False)
    landed = lax.dynamic_update_index_in_dim(landed, mine, _logical(_me()), 0)
    part = _sum_leading(landed, "reduce_" + name)
    return _swap_c(part, "swap_" + name).reshape(2 * part.shape[0], part.shape[1])


def _swap_c(v, name):
    out = jax.ShapeDtypeStruct((2,) + v.shape, v.dtype)
    slot = lambda: lax.axis_index("c")
    got = _exchange(name, _Transfer([1], lambda k, ins, o: ins[0], lambda k, ins, o: o.at[slot()], out, [v],
                                    split=8))
    return lax.dynamic_update_index_in_dim(got, v, slot(), 0)


def _sum_leading(v, name):
    n, R, C = v.shape
    tr = _tile(R, max(8, (1 << 19) // (C * n)))

    def body(v_ref, o_ref):
        acc = v_ref[0].astype(F32)
        for s in range(1, n):
            acc = acc + v_ref[s].astype(F32)
        o_ref[...] = acc

    return pl.pallas_call(body, grid=(R // tr,), in_specs=[pl.BlockSpec((n, tr, C), lambda i: (0, i, 0))],
                          out_specs=pl.BlockSpec((tr, C), lambda i: (i, 0)),
                          out_shape=jax.ShapeDtypeStruct((R, C), F32), compiler_params=_params(1), name=name)(v)


def _adamw(w, g, m, v, name):
    R, C = w.shape
    tr = _tile(R, max(8, (1 << 18) // C))
    c1 = 1.0 - ADAM_B1 ** ADAM_STEP
    c2 = 1.0 - ADAM_B2 ** ADAM_STEP

    def body(w_ref, g_ref, m_ref, v_ref, d_ref, nm_ref, nv_ref):
        gv = g_ref[...]
        nm = ADAM_B1 * m_ref[...] + (1.0 - ADAM_B1) * gv
        nv = ADAM_B2 * v_ref[...] + (1.0 - ADAM_B2) * (gv * gv)
        nm_ref[...] = nm
        nv_ref[...] = nv
        d_ref[...] = -ADAM_LR * ((nm / c1) / (jnp.sqrt(nv / c2) + ADAM_EPS) + ADAM_WD * w_ref[...])

    sp = pl.BlockSpec((tr, C), lambda i: (i, 0))
    return pl.pallas_call(body, grid=(R // tr,), in_specs=[sp] * 4, out_specs=[sp] * 3,
                          out_shape=[jax.ShapeDtypeStruct((R, C), F32)] * 3, compiler_params=_params(1),
                          name=name)(w, g, m, v)


def _forward_layer(x, lw, gw, next_shards, alpha, n_heads, tq, ts):
    S, D = x.shape
    W = D // 2
    reach = _attn_reach()
    m6 = lw['m6']
    FF = gw['w_mlp1'].shape[2] * 4

    h = _modulate(x, m6, 0, 1, "modulate1")
    qkv = _mm(h, gw['w_in'], M=S, N=3 * W, K=D, b_st='n', o_st='n', ns=3, out_dtype=BF16, cap_k=2048, name="proj_qkv")
    u = _mm(h, gw['w_in'][3], M=S, N=W, K=D, out_dtype=F32, cap_k=2048, name="proj_u")
    pad = ((reach, reach), (0, 0))
    kp, vp = jnp.pad(qkv[1], pad), jnp.pad(qkv[2], pad)
    riders = [_halves_transfer(next_shards[n]) for n in BIG] if next_shards else []
    (attn, lse), halves = _attention_fwd(qkv, kp, vp, n_heads, tq, riders)
    gw_next = ({n: _join_halves(next_shards[n], hv, "gather_" + n) for n, hv in zip(BIG, halves)}
               if next_shards else None)
    y, carries = zip(*[_s5_fwd(u, lw['a'][d], lw['b_re'][d], lw['b_im'][d], lw['c_re'][d], lw['c_im'][d], ts,
                               rev=bool(d)) for d in range(2)])
    pre, g = _ssm_act(u, y[0], y[1], lw['ssm_d'])
    z = _mm(g, gw['w_glu'].reshape(W, W), M=S, N=W, K=W, out_dtype=F32, name="glu")
    merged = _merge(attn, g, z, lw['b_glu'], lw['g_attn'], lw['g_ssm'])
    mix = _mm(merged, gw['w_out'].reshape(D, D), M=S, N=D, K=D, out_dtype=F32, cap_k=2048, name="out_proj")
    x1 = _post_ln(x, mix, m6, 2, lw['ln1_g'], lw['ln1_b'], alpha, "post_ln1")
    h2 = _modulate(x1, m6, 3, 4, "modulate2")
    a4 = _mm(h2, gw['w_mlp1'], M=S, N=FF, K=D, b_st='n', o_st='n', ns=4, out_dtype=F32, cap_k=2048, name="mlp1")
    ff = _mm(a4, gw['w_mlp2'], M=S, N=D, K=FF, a_st='k', b_st='k', ns=4, a_tf=_relu2, out_dtype=F32, name="mlp2")
    x2 = _post_ln(x1, ff, m6, 5, lw['ln2_g'], lw['ln2_b'], alpha, "post_ln2")
    saved = dict(x=x, h=h, qkv=qkv, u=u, attn=attn, lse=lse, carries=carries, pre=pre, g=g, z=z, merged=merged,
                 mix=mix, x1=x1, h2=h2, a4=a4, ff=ff, gw=gw)
    return x2, saved, gw_next


def _backward_layer(dx2, lw, sv, pending, alpha, n_heads, tq, ts):
    S, D = dx2.shape
    W = D // 2
    reach = _attn_reach()
    m6, gw = lw['m6'], sv['gw']
    FF = gw['w_mlp1'].shape[2] * 4
    gr = {}

    ds2, dff, gr['ln2_g'], gr['ln2_b'], d_g2 = _ln_bwd(dx2, sv['x1'], sv['ff'], m6, 5, lw['ln2_g'], alpha, "ln2_bwd")
    da4 = _mm(dff, gw['w_mlp2'], M=S, N=FF, K=D, tb=True, b_st='n', o_st='n', ns=4, extra=sv['a4'],
              epi=lambda acc, a: 2.0 * jnp.maximum(a, 0.0) * acc, out_dtype=BF16, cap_k=2048, name="mlp2_bwd_x")
    gr['w_mlp2'] = _mm(sv['a4'], dff, M=FF, N=D, K=S, ta=True, a_st='m', o_st='m', ns=4, a_tf=_relu2,
                       out_dtype=BF16, name="mlp2_bwd_w")
    dh2 = _mm(da4, gw['w_mlp1'], M=S, N=D, K=FF, tb=True, a_st='k', b_st='k', ns=4, out_dtype=F32, name="mlp1_bwd_x")
    gr['w_mlp1'] = _mm(sv['h2'], da4, M=D, N=FF, K=S, ta=True, b_st='n', o_st='n', ns=4, out_dtype=BF16,
                       name="mlp1_bwd_w")
    dx1, d_sc2, d_sh2 = _mod_bwd(ds2, dh2, sv['x1'], m6, 4, alpha, "mod2_bwd")

    ds1, dmix, gr['ln1_g'], gr['ln1_b'], d_g1 = _ln_bwd(dx1, sv['x'], sv['mix'], m6, 2, lw['ln1_g'], alpha, "ln1_bwd")
    dmerged = _mm(dmix, gw['w_out'].reshape(D, D), M=S, N=D, K=D, tb=True, out_dtype=F32, cap_k=2048, name="out_proj_bwd_x")
    gr['w_out'] = _mm(sv['merged'], dmix, M=D, N=D, K=S, ta=True, out_dtype=BF16, name="out_proj_bwd_w")
    dattn, dgp, dz, gr['g_attn'], gr['g_ssm'], gr['b_glu'] = _merge_bwd(
        dmerged, sv['attn'], sv['g'], sv['z'], lw['b_glu'], lw['g_attn'], lw['g_ssm'])
    dgl = _mm(dz, gw['w_glu'].reshape(W, W), M=S, N=W, K=W, tb=True, out_dtype=F32, name="glu_bwd_x")
    gr['w_glu'] = _mm(sv['g'], dz, M=W, N=W, K=S, ta=True, out_dtype=BF16, name="glu_bwd_w")
    dy, du_skip, gr['ssm_d'] = _act_bwd(dgp, dgl, sv['pre'], sv['u'], lw['ssm_d'])
    s5 = [_s5_bwd(sv['u'], dy, lw['a'][d], lw['b_re'][d], lw['b_im'][d], lw['c_re'][d], lw['c_im'][d],
                  sv['carries'][d], ts, rev=bool(d)) for d in range(2)]
    du = [s5[d][0] for d in range(2)]
    for pos, n in enumerate(['bb_re', 'bb_im', 'cc_re', 'cc_im', 'a']):
        gr[n] = jnp.stack([s5[d][pos + 1] for d in range(2)])
    pad = ((reach, reach), (0, 0))
    kp, vp = jnp.pad(sv['qkv'][1], pad), jnp.pad(sv['qkv'][2], pad)
    riders = [_scatter_transfer(pending[n]) for n in BIG] if pending else []
    (dq, dkT, dvT), landed = _attention_bwd(sv['qkv'], kp, vp, sv['attn'], sv['lse'], dattn, n_heads, tq, riders)
    dproj = _dproj(dq, dkT, dvT, du_skip, du[0], du[1], reach)
    dh = _mm(dproj, gw['w_in'], M=S, N=D, K=4 * W, tb=True, a_st='k', b_st='k', ns=4, out_dtype=F32, name="proj_bwd_x")
    gr['w_in'] = _mm(sv['h'], dproj, M=D, N=4 * W, K=S, ta=True, b_st='n', o_st='n', ns=4, out_dtype=BF16,
                     name="proj_bwd_w")
    dx, d_sc1, d_sh1 = _mod_bwd(ds1, dh, sv['x'], m6, 1, alpha, "mod1_bwd")
    gr['m6'] = jnp.concatenate([d_sh1, d_sc1, d_g1, d_sh2, d_sc2, d_g2], axis=0)
    return dx, gr, landed


def _pack(arrays, width):
    flat = jnp.concatenate([a.reshape(-1) for a in arrays])
    n = flat.shape[0]
    rows = -(-n // (8 * width)) * 8
    return jnp.pad(flat, (0, rows * width - n)).reshape(rows, width), n


def _unpack(packed, like):
    flat = packed.reshape(-1)
    out, pos = [], 0
    for a in like:
        out.append(flat[pos:pos + a.size].reshape(a.shape))
        pos += a.size
    return out


def kernel(x, c, w_ada, b_ada, w_in, ssm_lam_re, ssm_lam_im, ssm_log_step, ssm_b_re, ssm_b_im, ssm_c_re, ssm_c_im, ssm_d, w_glu, b_glu, g_attn, g_ssm, w_out, ln1_g, ln1_b, w_mlp1, w_mlp2, ln2_g, ln2_b, loss_target, m_w_ada, m_b_ada, m_w_in, m_ssm_lam_re, m_ssm_lam_im, m_ssm_log_step, m_ssm_b_re, m_ssm_b_im, m_ssm_c_re, m_ssm_c_im, m_ssm_d, m_w_glu, m_b_glu, m_g_attn, m_g_ssm, m_w_out, m_ln1_g, m_ln1_b, m_w_mlp1, m_w_mlp2, m_ln2_g, m_ln2_b, v_w_ada, v_b_ada, v_w_in, v_ssm_lam_re, v_ssm_lam_im, v_ssm_log_step, v_ssm_b_re, v_ssm_b_im, v_ssm_c_re, v_ssm_c_im, v_ssm_d, v_w_glu, v_b_glu, v_g_attn, v_g_ssm, v_w_out, v_ln1_g, v_ln1_b, v_w_mlp1, v_w_mlp2, v_ln2_g, v_ln2_b):
    args = locals()
    w = {n: args[n] for n in WEIGHTS}
    mom = {n: args['m_' + n] for n in WEIGHTS}
    var = {n: args['v_' + n] for n in WEIGHTS}
    L, D, ada_cols = w_ada.shape
    S = x.shape[1]
    W = D // 2
    n_heads = W // HEAD_DIM
    alpha = (2 * L) ** 0.25
    tq = _tile(S, 256)
    ts = _tile(S, 256)
    xi, yi, ci = _me()
    shard = 2 * xi + yi
    me = 4 * xi + 2 * yi + ci

    c_all = _all_gather8(c, "gather_c").reshape(N_DEV, D)
    c16 = jnp.pad(c_all, ((0, 16 - N_DEV), (0, 0)))
    b_sh = lax.dynamic_slice_in_dim(b_ada, shard * ada_cols, ada_cols, axis=1)
    mods_sh = jnp.stack([_mm(c16, w_ada[l], M=16, N=ada_cols, K=D, a_tf=_silu, bias=b_sh[l:l + 1], out_dtype=F32,
                             name="ada") for l in range(L)])
    mods_all = _all_gather8(mods_sh, "gather_mods")
    mods = jnp.concatenate([mods_all[2 * j] for j in range(4)], axis=-1)
    m6_all = lax.dynamic_index_in_dim(mods, me, axis=1, keepdims=False).reshape(L, 6, D)

    (a_re, a_im, bb_re, bb_im), disc_vjp = jax.vjp(
        _s5_discretize, ssm_lam_re, ssm_lam_im, ssm_log_step, ssm_b_re, ssm_b_im)
    N = a_re.shape[2] * a_re.shape[3]
    per_layer = dict(
        m6=m6_all,
        a=jnp.stack([a_re.reshape(L, 2, N // SB, SB), a_im.reshape(L, 2, N // SB, SB)], axis=3),
        b_re=jax.vmap(_to_blocks)(jnp.swapaxes(bb_re, -1, -2)).astype(BF16),
        b_im=jax.vmap(_to_blocks)(jnp.swapaxes(bb_im, -1, -2)).astype(BF16),
        c_re=jax.vmap(_to_blocks)(ssm_c_re).astype(BF16),
        c_im=jax.vmap(_to_blocks)(ssm_c_im).astype(BF16),
        ssm_d=ssm_d[:, None], b_glu=b_glu[:, None], g_attn=g_attn[:, None], g_ssm=g_ssm[:, None],
        ln1_g=ln1_g[:, None], ln1_b=ln1_b[:, None], ln2_g=ln2_g[:, None], ln2_b=ln2_b[:, None],
        w_in=w_in, w_glu=w_glu, w_out=w_out, w_mlp1=w_mlp1, w_mlp2=w_mlp2)

    layers = [{n: v[l] for n, v in per_layer.items()} for l in range(L)]
    shards = [{n: layers[l][n].astype(BF16) for n in BIG} for l in range(L)]
    gw = {n: _join_halves(v, _exchange("gather_" + n, _halves_transfer(v)), "gather_" + n)
          for n, v in shards[0].items()}
    xc, saved = x[0], []
    for l in range(L):
        xc, sv, gw = _forward_layer(xc, layers[l], gw, shards[l + 1] if l + 1 < L else None, alpha, n_heads, tq, ts)
        saved.append(sv)
    dx, sq = _loss_head(xc, loss_target[0])
    loss = lax.psum(0.5 * jnp.sum(sq) / D, ("x", "y", "c"))

    halves_of = lambda g: {n: g[n].reshape(4, 2, w[n].shape[1] // 2, w[n].shape[2]) for n in BIG}
    per_layer_grads, big = [None] * L, [None] * L
    pending = None
    for l in reversed(range(L)):
        dx, per_layer_grads[l], landed = _backward_layer(dx, layers[l], saved[l], pending, alpha, n_heads, tq, ts)
        if pending is not None:
            big[l + 1] = {n: _reduce_scattered(pending[n], lv, n) for n, lv in zip(BIG, landed)}
        pending = halves_of(per_layer_grads[l])
    big[0] = {n: _reduce_scattered(pending[n], _exchange("scatter_" + n, _scatter_transfer(pending[n])), n)
              for n in BIG}
    gr = {n: jnp.stack([g[n] for g in per_layer_grads]) for n in per_layer_grads[0] if n not in BIG}
    grads = {n: jnp.stack([big[l][n] for l in range(L)]) for n in BIG}

    dm_all = _all_gather8(gr['m6'].reshape(L, 6 * D), "gather_dmods")
    grads['b_ada'] = _sum_leading(jnp.pad(dm_all, ((0, 0), (0, 8 - L), (0, 0))), "reduce_b_ada")[:L]
    dm16 = jnp.pad(lax.dynamic_slice_in_dim(dm_all, shard * ada_cols, ada_cols, axis=2), ((0, 16 - N_DEV), (0, 0), (0, 0)))
    grads['w_ada'] = jnp.stack([_mm(c16, dm16[:, l], M=D, N=ada_cols, K=16, ta=True, a_tf=_silu, out_dtype=F32,
                                    name="ada_bwd_w") for l in range(L)])

    small_names = ['a', 'bb_re', 'bb_im', 'cc_re', 'cc_im', 'ssm_d', 'b_glu', 'g_attn', 'g_ssm',
                   'ln1_g', 'ln1_b', 'ln2_g', 'ln2_b']
    for n in ['bb_re', 'bb_im', 'cc_re', 'cc_im']:
        gr[n] = jax.vmap(_from_blocks)(gr[n])
    packed, _ = _pack([gr[n] for n in small_names], 1024)
    total = _sum_leading(_all_gather8(packed, "gather_small"), "reduce_small")
    red = dict(zip(small_names, _unpack(total, [gr[n] for n in small_names])))
    G, P = ssm_lam_re.shape[2], ssm_lam_re.shape[3]
    d_a = jnp.swapaxes(red['a'], 2, 3).reshape(L, 2, 2, G, P)
    (grads['ssm_lam_re'], grads['ssm_lam_im'], grads['ssm_log_step'], grads['ssm_b_re'],
     grads['ssm_b_im']) = disc_vjp((d_a[:, :, 0], d_a[:, :, 1], jnp.swapaxes(red['bb_re'], -1, -2),
                                    jnp.swapaxes(red['bb_im'], -1, -2)))
    grads['ssm_c_re'] = red['cc_re']
    grads['ssm_c_im'] = -red['cc_im']
    for n in ['ssm_d', 'b_glu', 'g_attn', 'g_ssm', 'ln1_g', 'ln1_b', 'ln2_g', 'ln2_b']:
        grads[n] = red[n].reshape(w[n].shape)

    delta, new_m, new_v = {}, {}, {}
    for n in ['w_ada'] + BIG:
        two_d = lambda t: t.reshape(-1, t.shape[-1])
        d_, m_, v_ = _adamw(two_d(w[n]), two_d(grads[n]), two_d(mom[n]), two_d(var[n]), "adamw_" + n)
        delta[n], new_m[n], new_v[n] = d_.reshape(w[n].shape), m_.reshape(w[n].shape), v_.reshape(w[n].shape)
    pk = lambda d: _pack([d[n] for n in SMALL], 1024)[0]
    d_, m_, v_ = _adamw(pk(w), pk(grads), pk(mom), pk(var), "adamw_small")
    like = [w[n] for n in SMALL]
    for n, dn, mn, vn in zip(SMALL, _unpack(d_, like), _unpack(m_, like), _unpack(v_, like)):
        delta[n], new_m[n], new_v[n] = dn, mn, vn

    return (loss, dx[None], *[grads[n] for n in WEIGHTS], *[delta[n] for n in WEIGHTS],
            *[new_m[n] for n in WEIGHTS], *[new_v[n] for n in WEIGHTS])
```

```python
import functools
import math

import jax
import jax.numpy as jnp
from jax import lax
from jax.experimental import pallas as pl
from jax.experimental.pallas import tpu as pltpu

F32 = jnp.float32
BF16 = jnp.bfloat16
MESH = pl.DeviceIdType.MESH
ANY = pl.BlockSpec(memory_space=pl.ANY)

HEAD_DIM = 64
SSM_GROUP = 16
SSM_STATE = 64
GROUPS_PER_BLOCK = 8
ATTN_PATTERNS = ((128, 1), (512, 4), (2048, 16))
LN_EPS = 1e-5
NEG_BIG = -1e30
ADAM_LR, ADAM_B1, ADAM_B2, ADAM_EPS, ADAM_WD, ADAM_STEP = 0.001, 0.9, 0.999, 1e-08, 0.01, 10
VMEM_LIMIT_V7X = 56 * 1024 * 1024
N_DEV = 8

WEIGHTS = ['w_ada', 'b_ada', 'w_in', 'ssm_lam_re', 'ssm_lam_im', 'ssm_log_step', 'ssm_b_re', 'ssm_b_im',
           'ssm_c_re', 'ssm_c_im', 'ssm_d', 'w_glu', 'b_glu', 'g_attn', 'g_ssm', 'w_out', 'ln1_g', 'ln1_b',
           'w_mlp1', 'w_mlp2', 'ln2_g', 'ln2_b']
BIG = ['w_in', 'w_glu', 'w_out', 'w_mlp1', 'w_mlp2']
SMALL = [n for n in WEIGHTS if n not in BIG and n != 'w_ada']


def _params(n_grid):
    return pltpu.CompilerParams(dimension_semantics=("arbitrary",) * n_grid, vmem_limit_bytes=VMEM_LIMIT_V7X)


def _tile(n, cap):
    t = 1 << (max(1, min(n, cap)).bit_length() - 1)
    while n % t:
        t //= 2
    return t


def _operand_spec(stack, transposed, tr, tc, nrb, ncb, pick):
    def index(i, j, k):
        r, c = pick(i, j, k)
        s = None
        if stack == 'r':
            s, r = r // nrb, r % nrb
        elif stack == 'c':
            s, c = c // ncb, c % ncb
        idx = (c, r) if transposed else (r, c)
        return idx if s is None else (s,) + idx
    blk = (tc, tr) if transposed else (tr, tc)
    return pl.BlockSpec(blk if stack is None else (None,) + blk, index)


def _mm(a, b, *, M, N, K, name, out_dtype, ta=False, tb=False, a_st=None, b_st=None, o_st=None, ns=1,
        a_tf=None, epi=None, extra=None, bias=None, cap_m=1024, cap_n=1024, cap_k=2048):
    pm = M // ns if 'm' in (a_st, o_st) else M
    pn = N // ns if 'n' in (b_st, o_st) else N
    pk = K // ns if 'k' in (a_st, b_st) else K
    tm, tn, tk = _tile(pm, cap_m), _tile(pn, cap_n), _tile(pk, cap_k)
    gm, gn, gk = M // tm, N // tn, K // tk
    nmb, nnb, nkb = pm // tm, pn // tn, pk // tk
    a_spec = _operand_spec({None: None, 'm': 'r', 'k': 'c'}[a_st], ta, tm, tk, nmb, nkb, lambda i, j, k: (i, k))
    b_spec = _operand_spec({None: None, 'k': 'r', 'n': 'c'}[b_st], tb, tk, tn, nkb, nnb, lambda i, j, k: (k, j))
    o_spec = _operand_spec({None: None, 'm': 'r', 'n': 'c'}[o_st], False, tm, tn, nmb, nnb, lambda i, j, k: (i, j))
    dn = (((0 if ta else 1,), (1 if tb else 0,)), ((), ()))
    ins, specs = [a, b], [a_spec, b_spec]
    if extra is not None:
        ins.append(extra)
        specs.append(o_spec)
    if bias is not None:
        ins.append(bias)
        specs.append(pl.BlockSpec((1, tn), lambda i, j, k: (0, j)))
    n_in = len(ins)
    if o_st is None:
        o_shape = (M, N)
    elif o_st == 'm':
        o_shape = (ns, pm, N)
    else:
        o_shape = (ns, M, pn)

    def body(*refs):
        a_ref, b_ref = refs[0], refs[1]
        o_ref, acc = refs[n_in], refs[n_in + 1]
        k = pl.program_id(2)

        @pl.when(k == 0)
        def _():
            acc[...] = jnp.zeros_like(acc)

        av = a_ref[...]
        if a_tf is not None:
            av = a_tf(av.astype(F32))
        acc[...] += lax.dot_general(av.astype(BF16), b_ref[...].astype(BF16), dn, preferred_element_type=F32)

        @pl.when(k == gk - 1)
        def _():
            r = acc[...]
            pos = 2
            if extra is not None:
                r = epi(r, refs[pos][...].astype(F32))
                pos += 1
            elif epi is not None:
                r = epi(r)
            if bias is not None:
                r = r + refs[pos][...]
            o_ref[...] = r.astype(o_ref.dtype)

    return pl.pallas_call(
        body, grid=(gm, gn, gk), in_specs=specs, out_specs=o_spec,
        out_shape=jax.ShapeDtypeStruct(o_shape, out_dtype),
        scratch_shapes=[pltpu.VMEM((tm, tn), F32)], compiler_params=_params(3), name=name)(*ins)


def _relu2(v):
    r = jnp.maximum(v, 0.0)
    return r * r


def _silu(v):
    return v / (1.0 + jnp.exp(-v))


def _rb(tr, w):
    return pl.BlockSpec((tr, w), lambda i: (i, 0))


def _pb(r, w):
    return pl.BlockSpec((r, w), lambda i: (0, 0))


def _row_call(body, n_rows, tr, ins, in_specs, outs, out_specs, name):
    return pl.pallas_call(body, grid=(n_rows // tr,), in_specs=in_specs, out_specs=out_specs,
                          out_shape=outs, compiler_params=_params(1), name=name)(*ins)


def _accumulate(ref, value):
    @pl.when(pl.program_id(0) == 0)
    def _():
        ref[...] = jnp.zeros_like(ref)
    ref[...] += jnp.sum(value, axis=0, keepdims=True)


def _modulate(x, m6, row_shift, row_scale, name):
    S, D = x.shape
    tr = _tile(S, 256)

    def body(x_ref, m_ref, h_ref):
        sh = m_ref[row_shift:row_shift + 1, :]
        sc = m_ref[row_scale:row_scale + 1, :]
        h_ref[...] = (x_ref[...] * (1.0 + sc) + sh).astype(BF16)

    return _row_call(body, S, tr, [x, m6], [_rb(tr, D), _pb(6, D)],
                     jax.ShapeDtypeStruct((S, D), BF16), _rb(tr, D), name)


def _gelu(v):
    c = math.sqrt(2.0 / math.pi)
    return 0.5 * v * (1.0 + jnp.tanh(c * (v + 0.044715 * v * v * v)))


def _gelu_grad(v):
    c = math.sqrt(2.0 / math.pi)
    t = jnp.tanh(c * (v + 0.044715 * v * v * v))
    return 0.5 * (1.0 + t) + 0.5 * v * (1.0 - t * t) * c * (1.0 + 3.0 * 0.044715 * v * v)


def _ssm_act(u, y0, y1, dskip):
    S, W = u.shape
    tr = _tile(S, 256)

    def body(u_ref, y0_ref, y1_ref, d_ref, pre_ref, g_ref):
        pre = u_ref[...] * d_ref[...] + y0_ref[...] + y1_ref[...]
        pre_ref[...] = pre
        g_ref[...] = _gelu(pre)

    return _row_call(body, S, tr, [u, y0, y1, dskip], [_rb(tr, W)] * 3 + [_pb(1, W)],
                     [jax.ShapeDtypeStruct((S, W), F32)] * 2, [_rb(tr, W)] * 2, "ssm_act")


def _merge(attn, g, z, b_glu, g_attn, g_ssm):
    S, W = attn.shape
    tr = _tile(S, 256)

    def body(a_ref, g_ref, z_ref, b_ref, ga_ref, gs_ref, o_ref):
        a = a_ref[...]
        ra = lax.rsqrt(jnp.mean(a * a, axis=-1, keepdims=True) + LN_EPS)
        o_ref[:, 0:W] = (a * ra * ga_ref[...]).astype(BF16)
        s = g_ref[...] / (1.0 + jnp.exp(-(z_ref[...] + b_ref[...])))
        rs = lax.rsqrt(jnp.mean(s * s, axis=-1, keepdims=True) + LN_EPS)
        o_ref[:, W:2 * W] = (s * rs * gs_ref[...]).astype(BF16)

    return _row_call(body, S, tr, [attn, g, z, b_glu, g_attn, g_ssm],
                     [_rb(tr, W)] * 3 + [_pb(1, W)] * 3,
                     jax.ShapeDtypeStruct((S, 2 * W), BF16), _rb(tr, 2 * W), "merge")


def _post_ln(x, branch, m6, row_gate, ln_g, ln_b, alpha, name):
    S, D = x.shape
    tr = _tile(S, 256)

    def body(x_ref, br_ref, m_ref, g_ref, b_ref, o_ref):
        gate = m_ref[row_gate:row_gate + 1, :]
        s = alpha * x_ref[...] + (1.0 + gate) * br_ref[...]
        mu = jnp.mean(s, axis=-1, keepdims=True)
        d = s - mu
        var = jnp.mean(d * d, axis=-1, keepdims=True)
        o_ref[...] = d * lax.rsqrt(var + LN_EPS) * g_ref[...] + b_ref[...]

    return _row_call(body, S, tr, [x, branch, m6, ln_g, ln_b],
                     [_rb(tr, D), _rb(tr, D), _pb(6, D), _pb(1, D), _pb(1, D)],
                     jax.ShapeDtypeStruct((S, D), F32), _rb(tr, D), name)


def _loss_head(x, target):
    S, D = x.shape
    tr = _tile(S, 256)

    def body(x_ref, t_ref, dx_ref, acc_ref):
        e = x_ref[...] - t_ref[...]
        dx_ref[...] = e * (1.0 / D)
        _accumulate(acc_ref, e * e)

    return _row_call(body, S, tr, [x, target], [_rb(tr, D)] * 2,
                     [jax.ShapeDtypeStruct((S, D), F32), jax.ShapeDtypeStruct((1, D), F32)],
                     [_rb(tr, D), _pb(1, D)], "loss_head")


def _ln_bwd(dxo, x_in, branch, m6, row_gate, ln_g, alpha, name):
    S, D = dxo.shape
    tr = _tile(S, 256)

    def body(dxo_ref, x_ref, br_ref, m_ref, g_ref, ds_ref, dbr_ref, dg_ref, db_ref, dgate_ref):
        gate = m_ref[row_gate:row_gate + 1, :]
        br = br_ref[...]
        s = alpha * x_ref[...] + (1.0 + gate) * br
        mu = jnp.mean(s, axis=-1, keepdims=True)
        d = s - mu
        var = jnp.mean(d * d, axis=-1, keepdims=True)
        rstd = lax.rsqrt(var + LN_EPS)
        xhat = d * rstd
        dxo = dxo_ref[...]
        dxh = dxo * g_ref[...]
        ds = rstd * (dxh - jnp.mean(dxh, axis=-1, keepdims=True)
                     - xhat * jnp.mean(dxh * xhat, axis=-1, keepdims=True))
        ds_ref[...] = ds
        dbr_ref[...] = ((1.0 + gate) * ds).astype(BF16)
        _accumulate(dg_ref, dxo * xhat)
        _accumulate(db_ref, dxo)
        _accumulate(dgate_ref, ds * br)

    vec = jax.ShapeDtypeStruct((1, D), F32)
    return _row_call(body, S, tr, [dxo, x_in, branch, m6, ln_g],
                     [_rb(tr, D)] * 3 + [_pb(6, D), _pb(1, D)],
                     [jax.ShapeDtypeStruct((S, D), F32), jax.ShapeDtypeStruct((S, D), BF16), vec, vec, vec],
                     [_rb(tr, D), _rb(tr, D), _pb(1, D), _pb(1, D), _pb(1, D)], name)


def _mod_bwd(ds, dh, x_in, m6, row_scale, alpha, name):
    S, D = ds.shape
    tr = _tile(S, 256)

    def body(ds_ref, dh_ref, x_ref, m_ref, dx_ref, dsc_ref, dsh_ref):
        sc = m_ref[row_scale:row_scale + 1, :]
        dh = dh_ref[...]
        dx_ref[...] = alpha * ds_ref[...] + dh * (1.0 + sc)
        _accumulate(dsc_ref, dh * x_ref[...])
        _accumulate(dsh_ref, dh)

    vec = jax.ShapeDtypeStruct((1, D), F32)
    return _row_call(body, S, tr, [ds, dh, x_in, m6], [_rb(tr, D)] * 3 + [_pb(6, D)],
                     [jax.ShapeDtypeStruct((S, D), F32), vec, vec],
                     [_rb(tr, D), _pb(1, D), _pb(1, D)], name)


def _merge_bwd(dmerged, attn, g, z, b_glu, g_attn, g_ssm):
    S, W = attn.shape
    tr = _tile(S, 256)

    def rms_bwd(x, gamma, dy):
        r = lax.rsqrt(jnp.mean(x * x, axis=-1, keepdims=True) + LN_EPS)
        gdy = gamma * dy
        dx = gdy * r - x * (r * r * r) * jnp.mean(gdy * x, axis=-1, keepdims=True)
        return dx, dy * x * r

    def body(dm_ref, a_ref, g_ref, z_ref, b_ref, ga_ref, gs_ref, da_ref, dgp_ref, dz_ref, dga_ref, dgs_ref, db_ref):
        da, dga = rms_bwd(a_ref[...], ga_ref[...], dm_ref[:, 0:W])
        da_ref[...] = da
        _accumulate(dga_ref, dga)
        gv = g_ref[...]
        sig = 1.0 / (1.0 + jnp.exp(-(z_ref[...] + b_ref[...])))
        dssm, dgs = rms_bwd(gv * sig, gs_ref[...], dm_ref[:, W:2 * W])
        _accumulate(dgs_ref, dgs)
        dgp_ref[...] = dssm * sig
        dz = dssm * gv * sig * (1.0 - sig)
        dz_ref[...] = dz.astype(BF16)
        _accumulate(db_ref, dz)

    vec = jax.ShapeDtypeStruct((1, W), F32)
    mat = jax.ShapeDtypeStruct((S, W), F32)
    return _row_call(body, S, tr, [dmerged, attn, g, z, b_glu, g_attn, g_ssm],
                     [_rb(tr, 2 * W)] + [_rb(tr, W)] * 3 + [_pb(1, W)] * 3,
                     [mat, mat, jax.ShapeDtypeStruct((S, W), BF16), vec, vec, vec],
                     [_rb(tr, W)] * 3 + [_pb(1, W)] * 3, "merge_bwd")


def _act_bwd(dgp, dgl, pre, u, dskip):
    S, W = pre.shape
    tr = _tile(S, 256)

    def body(a_ref, b_ref, pre_ref, u_ref, d_ref, dy_ref, du_ref, dd_ref):
        dy = (a_ref[...] + b_ref[...]) * _gelu_grad(pre_ref[...])
        dy_ref[...] = dy
        du_ref[...] = dy * d_ref[...]
        _accumulate(dd_ref, dy * u_ref[...])

    mat = jax.ShapeDtypeStruct((S, W), F32)
    return _row_call(body, S, tr, [dgp, dgl, pre, u, dskip], [_rb(tr, W)] * 4 + [_pb(1, W)],
                     [mat, mat, jax.ShapeDtypeStruct((1, W), F32)], [_rb(tr, W)] * 2 + [_pb(1, W)], "act_bwd")


def _dproj(dq, dkT, dvT, du_skip, du0, du1, reach):
    S, W = dq.shape
    tr = _tile(S, 256)
    lanes = 2 * HEAD_DIM
    n_pairs, nkb = W // lanes, tr // lanes
    assert tr % lanes == 0 and reach % tr == 0

    def body(dq_ref, dk_ref, dv_ref, a_ref, b_ref, c_ref, o_ref):
        o_ref[0] = dq_ref[...].astype(BF16)
        for h in range(n_pairs):
            for b in range(nkb):
                rows, cols = slice(b * lanes, (b + 1) * lanes), slice(h * lanes, (h + 1) * lanes)
                o_ref[1, rows, cols] = dk_ref[h, b].T.astype(BF16)
                o_ref[2, rows, cols] = dv_ref[h, b].T.astype(BF16)
        o_ref[3] = (a_ref[...] + b_ref[...] + c_ref[...]).astype(BF16)

    tsp = pl.BlockSpec((n_pairs, nkb, lanes, lanes), lambda i: (0, i + reach // tr, 0, 0))
    return _row_call(body, S, tr, [dq, dkT, dvT, du_skip, du0, du1],
                     [_rb(tr, W), tsp, tsp, _rb(tr, W), _rb(tr, W), _rb(tr, W)],
                     jax.ShapeDtypeStruct((4, S, W), BF16), pl.BlockSpec((4, tr, W), lambda i: (0, i, 0)), "dproj")


def _attn_reach():
    return max(w // 2 for w, _ in ATTN_PATTERNS)


def _bias_table(tq, width, reach, head):
    i = lax.broadcasted_iota(jnp.int32, (tq, width), 0)
    j = lax.broadcasted_iota(jnp.int32, (tq, width), 1)
    ad = jnp.abs(j - i - reach)
    mult = jnp.zeros((tq, width), jnp.int32)
    for window, dil in ATTN_PATTERNS:
        assert dil & (dil - 1) == 0
        mult += ((jnp.bitwise_and(ad, dil - 1) == 0) & (ad <= window // 2)).astype(jnp.int32)
    logm = jnp.zeros((tq, width), F32)
    for n in range(2, len(ATTN_PATTERNS) + 1):
        logm = jnp.where(mult == n, math.log(n), logm)
    return logm, ad.astype(F32), mult > 0


def _attn_common(n_heads, q_ref, k_ref, v_ref, bias_ref, tq, reach, seq):
    width = tq + 2 * reach
    hp, qi = pl.program_id(0), pl.program_id(1)

    @pl.when(qi == 0)
    def _():
        logm, ad, ok = _bias_table(tq, width, reach, None)
        for a in range(2):
            head = (2 * hp + a + 1).astype(F32)
            slope = jnp.exp(jnp.full((tq, width), -8.0 * math.log(2.0) / n_heads, F32) * head)
            bias_ref[a] = jnp.where(ok, logm - slope * ad, NEG_BIG)

    start = pl.multiple_of(qi * tq, tq)
    kw = k_ref[pl.ds(start, width), :]
    vw = v_ref[pl.ds(start, width), :]
    kpos = qi * tq - reach + lax.broadcasted_iota(jnp.int32, (1, width), 1)
    vbias = jnp.where((kpos >= 0) & (kpos < seq), 0.0, NEG_BIG).astype(F32)
    lane = lax.broadcasted_iota(jnp.int32, (1, 2 * HEAD_DIM), 1)
    masks = [lane < HEAD_DIM, lane >= HEAD_DIM]
    return start, kw, vw, vbias, masks


_NT = (((1,), (1,)), ((), ()))
assert math.log2(HEAD_DIM) % 2 == 0


def _attention_call(body, grid, in_specs, out_specs, out_shape, scratch, name, ins, riders):
    if riders:
        first = lambda: (pl.program_id(0) == 0) & (pl.program_id(1) == 0)
        last = lambda: (pl.program_id(0) == grid[0] - 1) & (pl.program_id(1) == grid[1] - 1)
        body = _hosted(body, len(in_specs), len(out_specs), len(scratch), riders, first, last)
        name += "_carrier"
    rider_ins = [a for t in riders for a in t.ins]
    res = pl.pallas_call(
        body, grid=grid, in_specs=in_specs + [ANY] * len(rider_ins), out_specs=out_specs + [ANY] * len(riders),
        out_shape=out_shape + [t.out_shape for t in riders],
        scratch_shapes=scratch + [s for t in riders for s in t.semaphores()],
        compiler_params=_params(2), name=name)(*ins, *rider_ins)
    return res[:len(out_specs)], res[len(out_specs):]


def _attention_fwd(qkv, kp, vp, n_heads, tq, riders=()):
    _, S, W = qkv.shape
    reach = _attn_reach()
    width = tq + 2 * reach
    Sp = S + 2 * reach
    scale = HEAD_DIM ** -0.5
    lanes = 2 * HEAD_DIM

    def body(q_ref, k_ref, v_ref, o_ref, lse_ref, bias_ref):
        _, kw, vw, vbias, masks = _attn_common(n_heads, q_ref, k_ref, v_ref, bias_ref, tq, reach, S)
        q = q_ref[...] * scale
        out = jnp.zeros((tq, lanes), F32)
        lse = jnp.zeros((tq, lanes), F32)
        for a in range(2):
            qa = jnp.where(masks[a], q, jnp.zeros_like(q))
            s = lax.dot_general(qa, kw, _NT, preferred_element_type=F32) + bias_ref[a] + vbias
            m = jnp.max(s, axis=-1, keepdims=True)
            p = jnp.exp(s - m)
            l = jnp.sum(p, axis=-1, keepdims=True)
            o = jnp.dot(p.astype(BF16), vw, preferred_element_type=F32) / l
            out = jnp.where(masks[a], o, out)
            lse = jnp.where(masks[a], m + jnp.log(l), lse)
        o_ref[...] = out
        lse_ref[...] = lse

    qsp = pl.BlockSpec((None, tq, lanes), lambda h, i: (0, i, h))
    ksp = pl.BlockSpec((Sp, lanes), lambda h, i: (0, h))
    osp = pl.BlockSpec((tq, lanes), lambda h, i: (i, h))
    mat = jax.ShapeDtypeStruct((S, W), F32)
    return _attention_call(body, (W // lanes, S // tq), [qsp, ksp, ksp], [osp, osp], [mat, mat],
                           [pltpu.VMEM((2, tq, width), F32)], "attn_fwd", (qkv, kp, vp), list(riders))


def _attention_bwd(qkv, kp, vp, out, lse, dout, n_heads, tq, riders=()):
    _, S, W = qkv.shape
    reach = _attn_reach()
    width = tq + 2 * reach
    Sp = S + 2 * reach
    scale = HEAD_DIM ** -0.5
    lanes = 2 * HEAD_DIM

    def body(q_ref, k_ref, v_ref, o_ref, lse_ref, do_ref, dq_ref, dk_ref, dv_ref, bias_ref):
        start, kw, vw, vbias, masks = _attn_common(n_heads, q_ref, k_ref, v_ref, bias_ref, tq, reach, S)

        @pl.when(pl.program_id(1) == 0)
        def _():
            dk_ref[...] = jnp.zeros_like(dk_ref)
            dv_ref[...] = jnp.zeros_like(dv_ref)

        q = q_ref[...] * scale
        do = do_ref[...]
        prod = do * o_ref[...]
        lse_all = lse_ref[...]
        dq = jnp.zeros((tq, lanes), F32)
        dkt = jnp.zeros((lanes, width), F32)
        dvt = jnp.zeros((lanes, width), F32)
        for a in range(2):
            qa = jnp.where(masks[a], q, jnp.zeros_like(q))
            doa = jnp.where(masks[a], do, 0.0)
            s = lax.dot_general(qa, kw, _NT, preferred_element_type=F32) + bias_ref[a] + vbias
            lse_a = lse_all[:, a * HEAD_DIM:a * HEAD_DIM + 1]
            p = jnp.exp(s - lse_a)
            delta = jnp.sum(jnp.where(masks[a], prod, 0.0), axis=-1, keepdims=True)
            dp = lax.dot_general(doa.astype(BF16), vw, _NT, preferred_element_type=F32)
            ds = (p * (dp - delta)).astype(BF16)
            dvt += jnp.dot(doa.T.astype(BF16), p.astype(BF16), preferred_element_type=F32)
            dkt += jnp.dot(qa.astype(F32).T.astype(BF16), ds, preferred_element_type=F32)
            dq = jnp.where(masks[a], jnp.dot(ds, kw, preferred_element_type=F32) * scale, dq)
        dq_ref[...] = dq
        first = pl.program_id(1) * (tq // lanes)
        for j in range(width // lanes):
            dk_ref[first + j] += dkt[:, j * lanes:(j + 1) * lanes]
            dv_ref[first + j] += dvt[:, j * lanes:(j + 1) * lanes]

    assert tq % lanes == 0 and reach % lanes == 0
    qsp = pl.BlockSpec((None, tq, lanes), lambda h, i: (0, i, h))
    ksp = pl.BlockSpec((Sp, lanes), lambda h, i: (0, h))
    osp = pl.BlockSpec((tq, lanes), lambda h, i: (i, h))
    tsp = pl.BlockSpec((None, Sp // lanes, lanes, lanes), lambda h, i: (h, 0, 0, 0))
    tiles = jax.ShapeDtypeStruct((W // lanes, Sp // lanes, lanes, lanes), F32)
    return _attention_call(body, (W // lanes, S // tq), [qsp, ksp, ksp, osp, osp, osp], [osp, tsp, tsp],
                           [jax.ShapeDtypeStruct((S, W), F32), tiles, tiles],
                           [pltpu.VMEM((2, tq, width), F32)], "attn_bwd", (qkv, kp, vp, out, lse, dout),
                           list(riders))


SB = GROUPS_PER_BLOCK * SSM_STATE
CB = GROUPS_PER_BLOCK * SSM_GROUP
RUNS = 8


def _powers(pw, a_r, a_i, n):
    p_r, p_i = a_r, a_i
    for j in range(n):
        pw[0, j:j + 1, :] = p_r
        pw[1, j:j + 1, :] = p_i
        p_r, p_i = p_r * a_r - p_i * a_i, p_r * a_i + p_i * a_r


def _rows8(j):
    return slice(j * RUNS, (j + 1) * RUNS)


def _run_scan(br, bi, T, a_r, a_i, pw, e_r, e_i, ent, down, conj):
    n = T // RUNS
    sg = -1.0 if conj else 1.0
    A_r = jnp.broadcast_to(a_r, (RUNS, SB))
    A_i = jnp.broadcast_to(sg * a_i, (RUNS, SB))
    x_r = x_i = None
    for j in (range(n - 1, -1, -1) if down else range(n)):
        b_r, b_i = br[_rows8(j), :], bi[_rows8(j), :]
        if x_r is None:
            x_r, x_i = b_r, b_i
        else:
            x_r, x_i = A_r * x_r - A_i * x_i + b_r, A_r * x_i + A_i * x_r + b_i
            br[_rows8(j), :] = x_r
            bi[_rows8(j), :] = x_i
    al_r, al_i = pw[0, n - 1:n, :], sg * pw[1, n - 1:n, :]
    s_r, s_i = e_r, e_i
    for c in (range(RUNS - 1, -1, -1) if down else range(RUNS)):
        ent[0, c:c + 1, :] = s_r
        ent[1, c:c + 1, :] = s_i
        s_r, s_i = (al_r * s_r - al_i * s_i + x_r[c:c + 1, :], al_r * s_i + al_i * s_r + x_i[c:c + 1, :])
    E_r, E_i = ent[0], ent[1]
    for j in range(n):
        p = n - 1 - j if down else j
        p_r, p_i = pw[0, p:p + 1, :], sg * pw[1, p:p + 1, :]
        v_r, v_i = br[_rows8(j), :], bi[_rows8(j), :]
        br[_rows8(j), :] = v_r + p_r * E_r - p_i * E_i
        bi[_rows8(j), :] = v_i + p_r * E_i + p_i * E_r
    return s_r, s_i


def _load_run_major(ref, T):
    n = T // RUNS
    return jnp.concatenate([ref[pl.ds(j, RUNS, stride=n), :] for j in range(n)], axis=0)


def _store_row_major(ref, value, T):
    n = T // RUNS
    for j in range(n):
        ref[pl.ds(j, RUNS, stride=n), :] = value[_rows8(j), :]


def _s5_specs(nb, T, rev, adjoint):
    flip = rev != adjoint
    blk = (lambda i: nb - 1 - i) if flip else (lambda i: i)
    usp = pl.BlockSpec((T, CB), lambda k, i: (blk(i), k))
    asp = pl.BlockSpec((None, 2, SB), lambda k, i: (k, 0, 0))
    wsp = pl.BlockSpec((None, CB, SB), lambda k, i: (k, 0, 0))
    csp = pl.BlockSpec((None, None, 2, SB), lambda k, i: (k, blk(i), 0, 0))
    return usp, asp, wsp, csp


def _s5_fwd(u, a, b_re, b_im, c_re, c_im, T, rev):
    S, W = u.shape
    NK = W // CB
    nb = S // T
    usp, asp, wsp, csp = _s5_specs(nb, T, rev, False)

    def body(u_ref, a_ref, bre_ref, bim_ref, cre_ref, cim_ref, y_ref, car_ref, wr, wi, st, pw, ent):
        a_r, a_i = a_ref[0:1, :], a_ref[1:2, :]

        @pl.when(pl.program_id(1) == 0)
        def _():
            st[...] = jnp.zeros_like(st)
            _powers(pw, a_r, a_i, T // RUNS)

        car_ref[...] = st[0:2, :]
        ub = _load_run_major(u_ref, T).astype(BF16)
        wr[...] = jnp.dot(ub, bre_ref[...], preferred_element_type=F32)
        wi[...] = jnp.dot(ub, bim_ref[...], preferred_element_type=F32)
        s_r, s_i = _run_scan(wr, wi, T, a_r, a_i, pw, st[0:1, :], st[1:2, :], ent, rev, False)
        st[0:1, :] = s_r
        st[1:2, :] = s_i
        y = (lax.dot_general(wr[...].astype(BF16), cre_ref[...], _NT, preferred_element_type=F32)
             - lax.dot_general(wi[...].astype(BF16), cim_ref[...], _NT, preferred_element_type=F32))
        _store_row_major(y_ref, y, T)

    return pl.pallas_call(
        body, grid=(NK, nb), in_specs=[usp, asp, wsp, wsp, wsp, wsp], out_specs=[usp, csp],
        out_shape=[jax.ShapeDtypeStruct((S, W), F32), jax.ShapeDtypeStruct((NK, nb, 2, SB), F32)],
        scratch_shapes=[pltpu.VMEM((T, SB), F32), pltpu.VMEM((T, SB), F32), pltpu.VMEM((8, SB), F32),
                        pltpu.VMEM((2, T // RUNS, SB), F32), pltpu.VMEM((2, RUNS, SB), F32)],
        compiler_params=_params(2), name="s5_fwd_rev" if rev else "s5_fwd")(u, a, b_re, b_im, c_re, c_im)


def _s5_bwd(u, dy, a, b_re, b_im, c_re, c_im, carries, T, rev):
    S, W = u.shape
    NK = W // CB
    nb = S // T
    usp, asp, wsp, csp = _s5_specs(nb, T, rev, True)
    n = T // RUNS

    def body(u_ref, dy_ref, a_ref, bre_ref, bim_ref, cre_ref, cim_ref, car_ref,
             du_ref, dbre_ref, dbim_ref, dcre_ref, dcim_ref, da_ref, wr, wi, gr, gi, lam, pw, ent):
        a_r, a_i = a_ref[0:1, :], a_ref[1:2, :]

        @pl.when(pl.program_id(1) == 0)
        def _():
            lam[...] = jnp.zeros_like(lam)
            for r in (dbre_ref, dbim_ref, dcre_ref, dcim_ref, da_ref):
                r[...] = jnp.zeros_like(r)
            _powers(pw, a_r, a_i, n)

        u32 = _load_run_major(u_ref, T)
        ub = u32.astype(BF16)
        dyk = _load_run_major(dy_ref, T)
        dyb = dyk.astype(BF16)
        wr[...] = jnp.dot(ub, bre_ref[...], preferred_element_type=F32)
        wi[...] = jnp.dot(ub, bim_ref[...], preferred_element_type=F32)
        x0r, x0i = car_ref[0:1, :], car_ref[1:2, :]
        _run_scan(wr, wi, T, a_r, a_i, pw, x0r, x0i, ent, rev, False)

        gr[...] = jnp.dot(dyb, cre_ref[...], preferred_element_type=F32)
        gi[...] = -jnp.dot(dyb, cim_ref[...], preferred_element_type=F32)
        l_r, l_i = _run_scan(gr, gi, T, a_r, a_i, pw, lam[0:1, :], lam[1:2, :], ent, not rev, True)
        lam[0:1, :] = l_r
        lam[1:2, :] = l_i

        sub = lax.broadcasted_iota(jnp.int32, (RUNS, SB), 0)

        def before(buf, x0, j):
            if rev:
                if j < n - 1:
                    return buf[_rows8(j + 1), :]
                return jnp.where(sub == RUNS - 1, x0, pltpu.roll(buf[_rows8(0), :], RUNS - 1, 0))
            if j > 0:
                return buf[_rows8(j - 1), :]
            return jnp.where(sub == 0, x0, pltpu.roll(buf[_rows8(n - 1), :], 1, 0))

        acc_r = jnp.zeros((RUNS, SB), F32)
        acc_i = jnp.zeros((RUNS, SB), F32)
        for j in range(n):
            g_r, g_i = gr[_rows8(j), :], gi[_rows8(j), :]
            p_r, p_i = before(wr, x0r, j), before(wi, x0i, j)
            acc_r += g_r * p_r + g_i * p_i
            acc_i += g_i * p_r - g_r * p_i
        da_ref[0:1, :] += jnp.sum(acc_r, axis=0, keepdims=True)
        da_ref[1:2, :] += jnp.sum(acc_i, axis=0, keepdims=True)

        lrb, lib = gr[...].astype(BF16), gi[...].astype(BF16)
        du = (lax.dot_general(lrb, bre_ref[...], _NT, preferred_element_type=F32)
              + lax.dot_general(lib, bim_ref[...], _NT, preferred_element_type=F32))
        _store_row_major(du_ref, du, T)
        ut = u32.T.astype(BF16)
        dbre_ref[...] += jnp.dot(ut, lrb, preferred_element_type=F32)
        dbim_ref[...] += jnp.dot(ut, lib, preferred_element_type=F32)
        dyt = dyk.T.astype(BF16)
        dcre_ref[...] += jnp.dot(dyt, wr[...].astype(BF16), preferred_element_type=F32)
        dcim_ref[...] += jnp.dot(dyt, wi[...].astype(BF16), preferred_element_type=F32)

    blk = jax.ShapeDtypeStruct((NK, CB, SB), F32)
    return pl.pallas_call(
        body, grid=(NK, nb), in_specs=[usp, usp, asp, wsp, wsp, wsp, wsp, csp],
        out_specs=[usp, wsp, wsp, wsp, wsp, asp],
        out_shape=[jax.ShapeDtypeStruct((S, W), F32), blk, blk, blk, blk, jax.ShapeDtypeStruct((NK, 2, SB), F32)],
        scratch_shapes=[pltpu.VMEM((T, SB), F32), pltpu.VMEM((T, SB), F32),
                        pltpu.VMEM((T, SB), F32), pltpu.VMEM((T, SB), F32), pltpu.VMEM((8, SB), F32),
                        pltpu.VMEM((2, n, SB), F32), pltpu.VMEM((2, RUNS, SB), F32)],
        compiler_params=_params(2), name="s5_bwd_rev" if rev else "s5_bwd")(
            u, dy, a, b_re, b_im, c_re, c_im, carries)


def _s5_discretize(lam_re, lam_im, log_step, b_re, b_im):
    step = jnp.exp(log_step)[..., None]
    mag = jnp.exp(lam_re * step)
    a_re, a_im = mag * jnp.cos(lam_im * step), mag * jnp.sin(lam_im * step)
    den = lam_re * lam_re + lam_im * lam_im
    coef_re = ((a_re - 1.0) * lam_re + a_im * lam_im) / den
    coef_im = (a_im * lam_re - (a_re - 1.0) * lam_im) / den
    bb_re = coef_re[..., None] * b_re - coef_im[..., None] * b_im
    bb_im = coef_re[..., None] * b_im + coef_im[..., None] * b_re
    return a_re, a_im, bb_re, bb_im


def _to_blocks(w_gcp):
    two, G, C, P = w_gcp.shape
    nk = G // GROUPS_PER_BLOCK
    x = w_gcp.reshape(two, nk, GROUPS_PER_BLOCK, C, P)
    eye = jnp.eye(GROUPS_PER_BLOCK, dtype=w_gcp.dtype)
    return jnp.einsum('dkgcp,gh->dkgchp', x, eye).reshape(two, nk, GROUPS_PER_BLOCK * C, GROUPS_PER_BLOCK * P)


def _from_blocks(blk):
    two, nk, cb, sb = blk.shape
    C, P = cb // GROUPS_PER_BLOCK, sb // GROUPS_PER_BLOCK
    x = blk.reshape(two, nk, GROUPS_PER_BLOCK, C, GROUPS_PER_BLOCK, P)
    eye = jnp.eye(GROUPS_PER_BLOCK, dtype=blk.dtype)
    return jnp.einsum('dkgchp,gh->dkgcp', x, eye).reshape(two, nk * GROUPS_PER_BLOCK, C, P)


def _me():
    return lax.axis_index("x"), lax.axis_index("y"), lax.axis_index("c")


def _peer(k):
    x, y, c = _me()
    return (1 - x if k & 4 else x, 1 - y if k & 2 else y, 1 - c if k & 1 else c)


def _logical(dev):
    return 4 * dev[0] + 2 * dev[1] + dev[2]


class _Transfer:
    def __init__(self, ks, src_of, dst_of, out_shape, ins, split=1):
        self.ks, self.src_of, self.dst_of, self.out_shape, self.ins, self.split = ks, src_of, dst_of, out_shape, ins, split

    def semaphores(self):
        n = len(self.ks) * self.split
        return [pltpu.SemaphoreType.DMA((n,)), pltpu.SemaphoreType.DMA((n,))]

    def copies(self, in_refs, out_ref, send, recv):
        out = []
        for k in self.ks:
            src, dst = self.src_of(k, in_refs, out_ref), self.dst_of(k, in_refs, out_ref)
            rows = src.shape[0] // self.split
            for q in range(self.split):
                part = pl.ds(q * rows, rows)
                j = len(out)
                out.append(pltpu.make_async_remote_copy(
                    src_ref=src if self.split == 1 else src.at[part],
                    dst_ref=dst if self.split == 1 else dst.at[part],
                    send_sem=send.at[j], recv_sem=recv.at[j], device_id=_peer(k), device_id_type=MESH))
        return out


def _exchange(name, t):
    n_in = len(t.ins)

    def body(*refs):
        copies = t.copies(refs[:n_in], refs[n_in], *refs[n_in + 1:])
        for cp in copies:
            cp.start()
        for cp in copies:
            cp.wait_recv()
        for cp in copies:
            cp.wait_send()

    return pl.pallas_call(body, in_specs=[ANY] * n_in, out_specs=ANY, out_shape=t.out_shape,
                          scratch_shapes=t.semaphores(), name=name)(*t.ins)


def _hosted(body, n_in, n_out, n_scratch, riders, is_first, is_last):
    n_rin = sum(len(t.ins) for t in riders)

    def wrapped(*refs):
        host_in, rider_in = refs[:n_in], refs[n_in:n_in + n_rin]
        pos = n_in + n_rin
        host_out, rider_out = refs[pos:pos + n_out], refs[pos + n_out:pos + n_out + len(riders)]
        pos += n_out + len(riders)
        host_scratch, sems = refs[pos:pos + n_scratch], refs[pos + n_scratch:]

        def copies():
            out, at = [], 0
            for i, t in enumerate(riders):
                out += t.copies(rider_in[at:at + len(t.ins)], rider_out[i], sems[2 * i], sems[2 * i + 1])
                at += len(t.ins)
            return out

        @pl.when(is_first())
        def _():
            for cp in copies():
                cp.start()

        body(*host_in, *host_out, *host_scratch)

        @pl.when(is_last())
        def _():
            for cp in copies():
                cp.wait_recv()
            for cp in copies():
                cp.wait_send()

    return wrapped


def _gather8_transfer(v):
    out = jax.ShapeDtypeStruct((N_DEV,) + v.shape, v.dtype)
    slot = lambda k, ins, o: o.at[_logical(_me())]
    return _Transfer(list(range(1, 8)), lambda k, ins, o: ins[0], slot, out, [v])


def _all_gather8(v, name):
    got = _exchange(name, _gather8_transfer(v))
    return lax.dynamic_update_index_in_dim(got, v, _logical(_me()), 0)


def _halves_transfer(v):
    R, C = v.shape
    half = lambda ins: ins[0].at[pl.ds(lax.axis_index("c") * (R // 2), R // 2)]
    chip = lambda: 2 * lax.axis_index("x") + lax.axis_index("y")
    return _Transfer([2, 4, 6], lambda k, ins, o: half(ins), lambda k, ins, o: o.at[chip()],
                     jax.ShapeDtypeStruct((4, R // 2, C), v.dtype), [v])


def _join_halves(v, halves, name):
    R, C = v.shape
    c = lax.axis_index("c")
    mine = lax.dynamic_slice_in_dim(v, c * (R // 2), R // 2, axis=0)
    halves = lax.dynamic_update_index_in_dim(halves, mine, 2 * lax.axis_index("x") + lax.axis_index("y"), 0)
    core = lambda o: o.at[:, lax.axis_index("c")]
    both = _exchange(name + "_join", _Transfer(
        [1], lambda k, ins, o: ins[0], lambda k, ins, o: core(o),
        jax.ShapeDtypeStruct((4, 2, R // 2, C), v.dtype), [halves], split=4))
    return lax.dynamic_update_index_in_dim(both, halves, c, 1).reshape(4, R, C)


def _scatter_transfer(g):
    four, two, R2, C = g.shape
    piece = lambda dev, ins: ins[0].at[2 * dev[0] + dev[1], dev[2]]
    slot = lambda k, ins, o: o.at[_logical(_me())]
    return _Transfer(list(range(1, 8)), lambda k, ins, o: piece(_peer(k), ins), slot,
                     jax.ShapeDtypeStruct((N_DEV, R2, C), g.dtype), [g])


def _reduce_scattered(g, landed, name):
    x, y, c = _me()
    mine = lax.dynamic_index_in_dim(lax.dynamic_index_in_dim(g, 2 * x + y, 0, False), c, 0, False)
    landed = lax.dynamic_update_index_in_dim(landed, mine, _logical(_me()), 0)
    part = _sum_leading(landed, "reduce_" + name)
    return _swap_c(part, "swap_" + name).reshape(2 * part.shape[0], part.shape[1])


def _swap_c(v, name):
    out = jax.ShapeDtypeStruct((2,) + v.shape, v.dtype)
    slot = lambda: lax.axis_index("c")
    got = _exchange(name, _Transfer([1], lambda k, ins, o: ins[0], lambda k, ins, o: o.at[slot()], out, [v],
                                    split=8))
    return lax.dynamic_update_index_in_dim(got, v, slot(), 0)


def _sum_leading(v, name):
    n, R, C = v.shape
    tr = _tile(R, max(8, (1 << 19) // (C * n)))

    def body(v_ref, o_ref):
        acc = v_ref[0].astype(F32)
        for s in range(1, n):
            acc = acc + v_ref[s].astype(F32)
        o_ref[...] = acc

    return pl.pallas_call(body, grid=(R // tr,), in_specs=[pl.BlockSpec((n, tr, C), lambda i: (0, i, 0))],
                          out_specs=pl.BlockSpec((tr, C), lambda i: (i, 0)),
                          out_shape=jax.ShapeDtypeStruct((R, C), F32), compiler_params=_params(1), name=name)(v)


def _adamw(w, g, m, v, name):
    R, C = w.shape
    tr = _tile(R, max(8, (1 << 18) // C))
    c1 = 1.0 - ADAM_B1 ** ADAM_STEP
    c2 = 1.0 - ADAM_B2 ** ADAM_STEP

    def body(w_ref, g_ref, m_ref, v_ref, d_ref, nm_ref, nv_ref):
        gv = g_ref[...]
        nm = ADAM_B1 * m_ref[...] + (1.0 - ADAM_B1) * gv
        nv = ADAM_B2 * v_ref[...] + (1.0 - ADAM_B2) * (gv * gv)
        nm_ref[...] = nm
        nv_ref[...] = nv
        d_ref[...] = -ADAM_LR * ((nm / c1) / (jnp.sqrt(nv / c2) + ADAM_EPS) + ADAM_WD * w_ref[...])

    sp = pl.BlockSpec((tr, C), lambda i: (i, 0))
    return pl.pallas_call(body, grid=(R // tr,), in_specs=[sp] * 4, out_specs=[sp] * 3,
                          out_shape=[jax.ShapeDtypeStruct((R, C), F32)] * 3, compiler_params=_params(1),
                          name=name)(w, g, m, v)


def _forward_layer(x, lw, w_in_full, shards, next_w_in, alpha, n_heads, tq, ts):
    S, D = x.shape
    W = D // 2
    reach = _attn_reach()
    m6 = lw['m6']
    gw = {'w_in': w_in_full}
    FF = shards['w_mlp1'].shape[1] * 4

    h = _modulate(x, m6, 0, 1, "modulate1")
    qkv = _mm(h, gw['w_in'], M=S, N=3 * W, K=D, b_st='n', o_st='n', ns=3, out_dtype=BF16, cap_k=2048, name="proj_qkv")
    u = _mm(h, gw['w_in'][3], M=S, N=W, K=D, out_dtype=F32, cap_k=2048, name="proj_u")
    pad = ((reach, reach), (0, 0))
    kp, vp = jnp.pad(qkv[1], pad), jnp.pad(qkv[2], pad)
    carried = [(n, shards[n]) for n in BIG[1:]] + ([('w_in', next_w_in)] if next_w_in is not None else [])
    (attn, lse), halves = _attention_fwd(qkv, kp, vp, n_heads, tq, [_halves_transfer(v) for _, v in carried])
    joined = [_join_halves(v, hv, "gather_" + n) for (n, v), hv in zip(carried, halves)]
    gw.update({n: j for (n, _), j in zip(carried[:len(BIG) - 1], joined)})
    w_in_next = joined[-1] if next_w_in is not None else None
    y, carries = zip(*[_s5_fwd(u, lw['a'][d], lw['b_re'][d], lw['b_im'][d], lw['c_re'][d], lw['c_im'][d], ts,
                               rev=bool(d)) for d in range(2)])
    pre, g = _ssm_act(u, y[0], y[1], lw['ssm_d'])
    z = _mm(g, gw['w_glu'].reshape(W, W), M=S, N=W, K=W, out_dtype=F32, name="glu")
    merged = _merge(attn, g, z, lw['b_glu'], lw['g_attn'], lw['g_ssm'])
    mix = _mm(merged, gw['w_out'].reshape(D, D), M=S, N=D, K=D, out_dtype=F32, cap_k=2048, name="out_proj")
    x1 = _post_ln(x, mix, m6, 2, lw['ln1_g'], lw['ln1_b'], alpha, "post_ln1")
    h2 = _modulate(x1, m6, 3, 4, "modulate2")
    a4 = _mm(h2, gw['w_mlp1'], M=S, N=FF, K=D, b_st='n', o_st='n', ns=4, out_dtype=F32, cap_k=2048, name="mlp1")
    ff = _mm(a4, gw['w_mlp2'], M=S, N=D, K=FF, a_st='k', b_st='k', ns=4, a_tf=_relu2, out_dtype=F32, name="mlp2")
    x2 = _post_ln(x1, ff, m6, 5, lw['ln2_g'], lw['ln2_b'], alpha, "post_ln2")
    saved = dict(x=x, h=h, qkv=qkv, u=u, attn=attn, lse=lse, carries=carries, pre=pre, g=g, z=z, merged=merged,
                 mix=mix, x1=x1, h2=h2, a4=a4, ff=ff, gw=gw)
    return x2, saved, w_in_next


SMALL_GRADS = ['a', 'bb_re', 'bb_im', 'cc_re', 'cc_im', 'ssm_d', 'b_glu', 'g_attn', 'g_ssm',
               'ln1_g', 'ln1_b', 'ln2_g', 'ln2_b']


def _as_halves(g):
    C = g.shape[-1]
    return g.reshape(4, 2, g.size // (8 * C), C)


def _backward_layer(dx2, lw, sv, pending_w_in, alpha, n_heads, tq, ts):
    S, D = dx2.shape
    W = D // 2
    reach = _attn_reach()
    m6, gw = lw['m6'], sv['gw']
    FF = gw['w_mlp1'].shape[2] * 4
    gr = {}

    ds2, dff, gr['ln2_g'], gr['ln2_b'], d_g2 = _ln_bwd(dx2, sv['x1'], sv['ff'], m6, 5, lw['ln2_g'], alpha, "ln2_bwd")
    da4 = _mm(dff, gw['w_mlp2'], M=S, N=FF, K=D, tb=True, b_st='n', o_st='n', ns=4, extra=sv['a4'],
              epi=lambda acc, a: 2.0 * jnp.maximum(a, 0.0) * acc, out_dtype=BF16, cap_k=2048, name="mlp2_bwd_x")
    gr['w_mlp2'] = _mm(sv['a4'], dff, M=FF, N=D, K=S, ta=True, a_st='m', o_st='m', ns=4, a_tf=_relu2,
                       out_dtype=BF16, name="mlp2_bwd_w")
    dh2 = _mm(da4, gw['w_mlp1'], M=S, N=D, K=FF, tb=True, a_st='k', b_st='k', ns=4, out_dtype=F32, name="mlp1_bwd_x")
    gr['w_mlp1'] = _mm(sv['h2'], da4, M=D, N=FF, K=S, ta=True, b_st='n', o_st='n', ns=4, out_dtype=BF16,
                       name="mlp1_bwd_w")
    dx1, d_sc2, d_sh2 = _mod_bwd(ds2, dh2, sv['x1'], m6, 4, alpha, "mod2_bwd")

    ds1, dmix, gr['ln1_g'], gr['ln1_b'], d_g1 = _ln_bwd(dx1, sv['x'], sv['mix'], m6, 2, lw['ln1_g'], alpha, "ln1_bwd")
    dmerged = _mm(dmix, gw['w_out'].reshape(D, D), M=S, N=D, K=D, tb=True, out_dtype=F32, cap_k=2048, name="out_proj_bwd_x")
    gr['w_out'] = _mm(sv['merged'], dmix, M=D, N=D, K=S, ta=True, out_dtype=BF16, name="out_proj_bwd_w")
    dattn, dgp, dz, gr['g_attn'], gr['g_ssm'], gr['b_glu'] = _merge_bwd(
        dmerged, sv['attn'], sv['g'], sv['z'], lw['b_glu'], lw['g_attn'], lw['g_ssm'])
    dgl = _mm(dz, gw['w_glu'].reshape(W, W), M=S, N=W, K=W, tb=True, out_dtype=F32, name="glu_bwd_x")
    gr['w_glu'] = _mm(sv['g'], dz, M=W, N=W, K=S, ta=True, out_dtype=BF16, name="glu_bwd_w")
    dy, du_skip, gr['ssm_d'] = _act_bwd(dgp, dgl, sv['pre'], sv['u'], lw['ssm_d'])
    s5 = [_s5_bwd(sv['u'], dy, lw['a'][d], lw['b_re'][d], lw['b_im'][d], lw['c_re'][d], lw['c_im'][d],
                  sv['carries'][d], ts, rev=bool(d)) for d in range(2)]
    du = [s5[d][0] for d in range(2)]
    for pos, n in enumerate(['bb_re', 'bb_im', 'cc_re', 'cc_im', 'a']):
        gr[n] = jnp.stack([s5[d][pos + 1] for d in range(2)])
    pad = ((reach, reach), (0, 0))
    kp, vp = jnp.pad(sv['qkv'][1], pad), jnp.pad(sv['qkv'][2], pad)
    for n in ['bb_re', 'bb_im', 'cc_re', 'cc_im']:
        gr[n] = _from_blocks(gr[n])
    small, _ = _pack([gr[n] for n in SMALL_GRADS], 1024)
    scattered = [(n, _as_halves(gr[n])) for n in BIG[1:]]
    if pending_w_in is not None:
        scattered.append(('w_in', pending_w_in))
    riders = [_scatter_transfer(g) for _, g in scattered] + [_gather8_transfer(small)]
    (dq, dkT, dvT), landed = _attention_bwd(sv['qkv'], kp, vp, sv['attn'], sv['lse'], dattn, n_heads, tq, riders)
    reduced = [_reduce_scattered(g, lv, n) for (n, g), lv in zip(scattered, landed)]
    gr.update({n: r for (n, _), r in zip(scattered[:len(BIG) - 1], reduced)})
    w_in_above = reduced[-1] if pending_w_in is not None else None
    own = lax.dynamic_update_index_in_dim(landed[-1], small, _logical(_me()), 0)
    summed = _unpack(_sum_leading(own, "reduce_small"), [gr[n] for n in SMALL_GRADS])
    gr.update(dict(zip(SMALL_GRADS, summed)))
    dproj = _dproj(dq, dkT, dvT, du_skip, du[0], du[1], reach)
    dh = _mm(dproj, gw['w_in'], M=S, N=D, K=4 * W, tb=True, a_st='k', b_st='k', ns=4, out_dtype=F32, cap_n=2048, name="proj_bwd_x")
    gr['w_in'] = _mm(sv['h'], dproj, M=D, N=4 * W, K=S, ta=True, b_st='n', o_st='n', ns=4, out_dtype=BF16,
                     name="proj_bwd_w")
    dx, d_sc1, d_sh1 = _mod_bwd(ds1, dh, sv['x'], m6, 1, alpha, "mod1_bwd")
    gr['m6'] = jnp.concatenate([d_sh1, d_sc1, d_g1, d_sh2, d_sc2, d_g2], axis=0)
    return dx, gr, w_in_above


def _pack(arrays, width):
    flat = jnp.concatenate([a.reshape(-1) for a in arrays])
    n = flat.shape[0]
    rows = -(-n // (8 * width)) * 8
    return jnp.pad(flat, (0, rows * width - n)).reshape(rows, width), n


def _unpack(packed, like):
    flat = packed.reshape(-1)
    out, pos = [], 0
    for a in like:
        out.append(flat[pos:pos + a.size].reshape(a.shape))
        pos += a.size
    return out


def kernel(x, c, w_ada, b_ada, w_in, ssm_lam_re, ssm_lam_im, ssm_log_step, ssm_b_re, ssm_b_im, ssm_c_re, ssm_c_im, ssm_d, w_glu, b_glu, g_attn, g_ssm, w_out, ln1_g, ln1_b, w_mlp1, w_mlp2, ln2_g, ln2_b, loss_target, m_w_ada, m_b_ada, m_w_in, m_ssm_lam_re, m_ssm_lam_im, m_ssm_log_step, m_ssm_b_re, m_ssm_b_im, m_ssm_c_re, m_ssm_c_im, m_ssm_d, m_w_glu, m_b_glu, m_g_attn, m_g_ssm, m_w_out, m_ln1_g, m_ln1_b, m_w_mlp1, m_w_mlp2, m_ln2_g, m_ln2_b, v_w_ada, v_b_ada, v_w_in, v_ssm_lam_re, v_ssm_lam_im, v_ssm_log_step, v_ssm_b_re, v_ssm_b_im, v_ssm_c_re, v_ssm_c_im, v_ssm_d, v_w_glu, v_b_glu, v_g_attn, v_g_ssm, v_w_out, v_ln1_g, v_ln1_b, v_w_mlp1, v_w_mlp2, v_ln2_g, v_ln2_b):
    args = locals()
    w = {n: args[n] for n in WEIGHTS}
    mom = {n: args['m_' + n] for n in WEIGHTS}
    var = {n: args['v_' + n] for n in WEIGHTS}
    L, D, ada_cols = w_ada.shape
    S = x.shape[1]
    W = D // 2
    n_heads = W // HEAD_DIM
    alpha = (2 * L) ** 0.25
    tq = _tile(S, 256)
    ts = _tile(S, 512)
    xi, yi, ci = _me()
    shard = 2 * xi + yi
    me = 4 * xi + 2 * yi + ci

    c_all = _all_gather8(c, "gather_c").reshape(N_DEV, D)
    c16 = jnp.pad(c_all, ((0, 16 - N_DEV), (0, 0)))
    b_sh = lax.dynamic_slice_in_dim(b_ada, shard * ada_cols, ada_cols, axis=1)
    mods_sh = jnp.stack([_mm(c16, w_ada[l], M=16, N=ada_cols, K=D, a_tf=_silu, bias=b_sh[l:l + 1], out_dtype=F32,
                             name="ada") for l in range(L)])
    mods_all = _all_gather8(mods_sh, "gather_mods")
    mods = jnp.concatenate([mods_all[2 * j] for j in range(4)], axis=-1)
    m6_all = lax.dynamic_index_in_dim(mods, me, axis=1, keepdims=False).reshape(L, 6, D)

    (a_re, a_im, bb_re, bb_im), disc_vjp = jax.vjp(
        _s5_discretize, ssm_lam_re, ssm_lam_im, ssm_log_step, ssm_b_re, ssm_b_im)
    N = a_re.shape[2] * a_re.shape[3]
    per_layer = dict(
        m6=m6_all,
        a=jnp.stack([a_re.reshape(L, 2, N // SB, SB), a_im.reshape(L, 2, N // SB, SB)], axis=3),
        b_re=jax.vmap(_to_blocks)(jnp.swapaxes(bb_re, -1, -2)).astype(BF16),
        b_im=jax.vmap(_to_blocks)(jnp.swapaxes(bb_im, -1, -2)).astype(BF16),
        c_re=jax.vmap(_to_blocks)(ssm_c_re).astype(BF16),
        c_im=jax.vmap(_to_blocks)(ssm_c_im).astype(BF16),
        ssm_d=ssm_d[:, None], b_glu=b_glu[:, None], g_attn=g_attn[:, None], g_ssm=g_ssm[:, None],
        ln1_g=ln1_g[:, None], ln1_b=ln1_b[:, None], ln2_g=ln2_g[:, None], ln2_b=ln2_b[:, None],
        w_in=w_in, w_glu=w_glu, w_out=w_out, w_mlp1=w_mlp1, w_mlp2=w_mlp2)

    layers = [{n: v[l] for n, v in per_layer.items()} for l in range(L)]
    shards = [{n: layers[l][n].astype(BF16) for n in BIG} for l in range(L)]
    first = shards[0]['w_in']
    w_in_full = _join_halves(first, _exchange("gather_w_in", _halves_transfer(first)), "gather_w_in")
    xc, saved = x[0], []
    for l in range(L):
        xc, sv, w_in_full = _forward_layer(xc, layers[l], w_in_full, shards[l],
                                           shards[l + 1]['w_in'] if l + 1 < L else None, alpha, n_heads, tq, ts)
        saved.append(sv)
    dx, sq = _loss_head(xc, loss_target[0])
    loss = lax.psum(0.5 * jnp.sum(sq) / D, ("x", "y", "c"))

    per_layer_grads = [None] * L
    pending = None
    for l in reversed(range(L)):
        dx, per_layer_grads[l], above = _backward_layer(dx, layers[l], saved[l], pending, alpha, n_heads, tq, ts)
        if pending is not None:
            per_layer_grads[l + 1]['w_in'] = above
        pending = _as_halves(per_layer_grads[l]['w_in'])
    per_layer_grads[0]['w_in'] = _reduce_scattered(
        pending, _exchange("scatter_w_in", _scatter_transfer(pending)), 'w_in')
    gr = {n: jnp.stack([g[n] for g in per_layer_grads]) for n in per_layer_grads[0]}
    grads = {n: gr[n].reshape(w[n].shape) for n in BIG}

    dm_all = _all_gather8(gr['m6'].reshape(L, 6 * D), "gather_dmods")
    grads['b_ada'] = _sum_leading(jnp.pad(dm_all, ((0, 0), (0, 8 - L), (0, 0))), "reduce_b_ada")[:L]
    dm16 = jnp.pad(lax.dynamic_slice_in_dim(dm_all, shard * ada_cols, ada_cols, axis=2), ((0, 16 - N_DEV), (0, 0), (0, 0)))
    grads['w_ada'] = jnp.stack([_mm(c16, dm16[:, l], M=D, N=ada_cols, K=16, ta=True, a_tf=_silu, out_dtype=F32,
                                    name="ada_bwd_w") for l in range(L)])

    red = gr
    G, P = ssm_lam_re.shape[2], ssm_lam_re.shape[3]
    d_a = jnp.swapaxes(red['a'], 2, 3).reshape(L, 2, 2, G, P)
    (grads['ssm_lam_re'], grads['ssm_lam_im'], grads['ssm_log_step'], grads['ssm_b_re'],
     grads['ssm_b_im']) = disc_vjp((d_a[:, :, 0], d_a[:, :, 1], jnp.swapaxes(red['bb_re'], -1, -2),
                                    jnp.swapaxes(red['bb_im'], -1, -2)))
    grads['ssm_c_re'] = red['cc_re']
    grads['ssm_c_im'] = -red['cc_im']
    for n in ['ssm_d', 'b_glu', 'g_attn', 'g_ssm', 'ln1_g', 'ln1_b', 'ln2_g', 'ln2_b']:
        grads[n] = red[n].reshape(w[n].shape)

    delta, new_m, new_v = {}, {}, {}
    for n in ['w_ada'] + BIG:
        two_d = lambda t: t.reshape(-1, t.shape[-1])
        d_, m_, v_ = _adamw(two_d(w[n]), two_d(grads[n]), two_d(mom[n]), two_d(var[n]), "adamw_" + n)
        delta[n], new_m[n], new_v[n] = d_.reshape(w[n].shape), m_.reshape(w[n].shape), v_.reshape(w[n].shape)
    pk = lambda d: _pack([d[n] for n in SMALL], 1024)[0]
    d_, m_, v_ = _adamw(pk(w), pk(grads), pk(mom), pk(var), "adamw_small")
    like = [w[n] for n in SMALL]
    for n, dn, mn, vn in zip(SMALL, _unpack(d_, like), _unpack(m_, like), _unpack(v_, like)):
        delta[n], new_m[n], new_v[n] = dn, mn, vn

    return (loss, dx[None], *[grads[n] for n in WEIGHTS], *[delta[n] for n in WEIGHTS],
            *[new_m[n] for n in WEIGHTS], *[new_v[n] for n in WEIGHTS])
```

```python
import functools
import math

import jax
import jax.numpy as jnp
from jax import lax
from jax.experimental import pallas as pl
from jax.experimental.pallas import tpu as pltpu

F32 = jnp.float32
BF16 = jnp.bfloat16
MESH = pl.DeviceIdType.MESH
ANY = pl.BlockSpec(memory_space=pl.ANY)

HEAD_DIM = 64
SSM_GROUP = 16
SSM_STATE = 64
GROUPS_PER_BLOCK = 8
ATTN_PATTERNS = ((128, 1), (512, 4), (2048, 16))
LN_EPS = 1e-5
NEG_BIG = -1e30
ADAM_LR, ADAM_B1, ADAM_B2, ADAM_EPS, ADAM_WD, ADAM_STEP = 0.001, 0.9, 0.999, 1e-08, 0.01, 10
VMEM_LIMIT_V7X = 56 * 1024 * 1024
N_DEV = 8

WEIGHTS = ['w_ada', 'b_ada', 'w_in', 'ssm_lam_re', 'ssm_lam_im', 'ssm_log_step', 'ssm_b_re', 'ssm_b_im',
           'ssm_c_re', 'ssm_c_im', 'ssm_d', 'w_glu', 'b_glu', 'g_attn', 'g_ssm', 'w_out', 'ln1_g', 'ln1_b',
           'w_mlp1', 'w_mlp2', 'ln2_g', 'ln2_b']
BIG = ['w_in', 'w_glu', 'w_out', 'w_mlp1', 'w_mlp2']
SMALL = [n for n in WEIGHTS if n not in BIG and n != 'w_ada']


def _params(n_grid):
    return pltpu.CompilerParams(dimension_semantics=("arbitrary",) * n_grid, vmem_limit_bytes=VMEM_LIMIT_V7X)


def _tile(n, cap):
    t = 1 << (max(1, min(n, cap)).bit_length() - 1)
    while n % t:
        t //= 2
    return t


def _operand_spec(stack, transposed, tr, tc, nrb, ncb, pick):
    def index(i, j, k):
        r, c = pick(i, j, k)
        s = None
        if stack == 'r':
            s, r = r // nrb, r % nrb
        elif stack == 'c':
            s, c = c // ncb, c % ncb
        idx = (c, r) if transposed else (r, c)
        return idx if s is None else (s,) + idx
    blk = (tc, tr) if transposed else (tr, tc)
    return pl.BlockSpec(blk if stack is None else (None,) + blk, index)


def _mm(a, b, *, M, N, K, name, out_dtype, ta=False, tb=False, a_st=None, b_st=None, o_st=None, ns=1,
        a_tf=None, epi=None, extra=None, bias=None, cap_m=1024, cap_n=1024, cap_k=2048):
    pm = M // ns if 'm' in (a_st, o_st) else M
    pn = N // ns if 'n' in (b_st, o_st) else N
    pk = K // ns if 'k' in (a_st, b_st) else K
    tm, tn, tk = _tile(pm, cap_m), _tile(pn, cap_n), _tile(pk, cap_k)
    gm, gn, gk = M // tm, N // tn, K // tk
    nmb, nnb, nkb = pm // tm, pn // tn, pk // tk
    a_spec = _operand_spec({None: None, 'm': 'r', 'k': 'c'}[a_st], ta, tm, tk, nmb, nkb, lambda i, j, k: (i, k))
    b_spec = _operand_spec({None: None, 'k': 'r', 'n': 'c'}[b_st], tb, tk, tn, nkb, nnb, lambda i, j, k: (k, j))
    o_spec = _operand_spec({None: None, 'm': 'r', 'n': 'c'}[o_st], False, tm, tn, nmb, nnb, lambda i, j, k: (i, j))
    dn = (((0 if ta else 1,), (1 if tb else 0,)), ((), ()))
    ins, specs = [a, b], [a_spec, b_spec]
    if extra is not None:
        ins.append(extra)
        specs.append(o_spec)
    if bias is not None:
        ins.append(bias)
        specs.append(pl.BlockSpec((1, tn), lambda i, j, k: (0, j)))
    n_in = len(ins)
    if o_st is None:
        o_shape = (M, N)
    elif o_st == 'm':
        o_shape = (ns, pm, N)
    else:
        o_shape = (ns, M, pn)

    def body(*refs):
        a_ref, b_ref = refs[0], refs[1]
        o_ref, acc = refs[n_in], refs[n_in + 1]
        k = pl.program_id(2)

        @pl.when(k == 0)
        def _():
            acc[...] = jnp.zeros_like(acc)

        av = a_ref[...]
        if a_tf is not None:
            av = a_tf(av.astype(F32))
        acc[...] += lax.dot_general(av.astype(BF16), b_ref[...].astype(BF16), dn, preferred_element_type=F32)

        @pl.when(k == gk - 1)
        def _():
            r = acc[...]
            pos = 2
            if extra is not None:
                r = epi(r, refs[pos][...].astype(F32))
                pos += 1
            elif epi is not None:
                r = epi(r)
            if bias is not None:
                r = r + refs[pos][...]
            o_ref[...] = r.astype(o_ref.dtype)

    return pl.pallas_call(
        body, grid=(gm, gn, gk), in_specs=specs, out_specs=o_spec,
        out_shape=jax.ShapeDtypeStruct(o_shape, out_dtype),
        scratch_shapes=[pltpu.VMEM((tm, tn), F32)], compiler_params=_params(3), name=name)(*ins)


def _relu2(v):
    r = jnp.maximum(v, 0.0)
    return r * r


def _silu(v):
    return v / (1.0 + jnp.exp(-v))


def _rb(tr, w):
    return pl.BlockSpec((tr, w), lambda i: (i, 0))


def _pb(r, w):
    return pl.BlockSpec((r, w), lambda i: (0, 0))


def _row_call(body, n_rows, tr, ins, in_specs, outs, out_specs, name):
    return pl.pallas_call(body, grid=(n_rows // tr,), in_specs=in_specs, out_specs=out_specs,
                          out_shape=outs, compiler_params=_params(1), name=name)(*ins)


def _accumulate(ref, value):
    @pl.when(pl.program_id(0) == 0)
    def _():
        ref[...] = jnp.zeros_like(ref)
    ref[...] += jnp.sum(value, axis=0, keepdims=True)


def _modulate(x, m6, row_shift, row_scale, name):
    S, D = x.shape
    tr = _tile(S, 256)

    def body(x_ref, m_ref, h_ref):
        sh = m_ref[row_shift:row_shift + 1, :]
        sc = m_ref[row_scale:row_scale + 1, :]
        h_ref[...] = (x_ref[...] * (1.0 + sc) + sh).astype(BF16)

    return _row_call(body, S, tr, [x, m6], [_rb(tr, D), _pb(6, D)],
                     jax.ShapeDtypeStruct((S, D), BF16), _rb(tr, D), name)


def _gelu(v):
    c = math.sqrt(2.0 / math.pi)
    return 0.5 * v * (1.0 + jnp.tanh(c * (v + 0.044715 * v * v * v)))


def _gelu_grad(v):
    c = math.sqrt(2.0 / math.pi)
    t = jnp.tanh(c * (v + 0.044715 * v * v * v))
    return 0.5 * (1.0 + t) + 0.5 * v * (1.0 - t * t) * c * (1.0 + 3.0 * 0.044715 * v * v)


def _ssm_act(u, y0, y1, dskip):
    S, W = u.shape
    tr = _tile(S, 256)

    def body(u_ref, y0_ref, y1_ref, d_ref, pre_ref, g_ref):
        pre = u_ref[...] * d_ref[...] + y0_ref[...] + y1_ref[...]
        pre_ref[...] = pre
        g_ref[...] = _gelu(pre)

    return _row_call(body, S, tr, [u, y0, y1, dskip], [_rb(tr, W)] * 3 + [_pb(1, W)],
                     [jax.ShapeDtypeStruct((S, W), F32)] * 2, [_rb(tr, W)] * 2, "ssm_act")


def _merge(attn, g, z, b_glu, g_attn, g_ssm):
    S, W = attn.shape
    tr = _tile(S, 256)

    def body(a_ref, g_ref, z_ref, b_ref, ga_ref, gs_ref, o_ref):
        a = a_ref[...]
        ra = lax.rsqrt(jnp.mean(a * a, axis=-1, keepdims=True) + LN_EPS)
        o_ref[:, 0:W] = (a * ra * ga_ref[...]).astype(BF16)
        s = g_ref[...] / (1.0 + jnp.exp(-(z_ref[...] + b_ref[...])))
        rs = lax.rsqrt(jnp.mean(s * s, axis=-1, keepdims=True) + LN_EPS)
        o_ref[:, W:2 * W] = (s * rs * gs_ref[...]).astype(BF16)

    return _row_call(body, S, tr, [attn, g, z, b_glu, g_attn, g_ssm],
                     [_rb(tr, W)] * 3 + [_pb(1, W)] * 3,
                     jax.ShapeDtypeStruct((S, 2 * W), BF16), _rb(tr, 2 * W), "merge")


def _post_ln(x, branch, m6, row_gate, ln_g, ln_b, alpha, name):
    S, D = x.shape
    tr = _tile(S, 256)

    def body(x_ref, br_ref, m_ref, g_ref, b_ref, o_ref):
        gate = m_ref[row_gate:row_gate + 1, :]
        s = alpha * x_ref[...] + (1.0 + gate) * br_ref[...]
        mu = jnp.mean(s, axis=-1, keepdims=True)
        d = s - mu
        var = jnp.mean(d * d, axis=-1, keepdims=True)
        o_ref[...] = d * lax.rsqrt(var + LN_EPS) * g_ref[...] + b_ref[...]

    return _row_call(body, S, tr, [x, branch, m6, ln_g, ln_b],
                     [_rb(tr, D), _rb(tr, D), _pb(6, D), _pb(1, D), _pb(1, D)],
                     jax.ShapeDtypeStruct((S, D), F32), _rb(tr, D), name)


def _loss_head(x, target):
    S, D = x.shape
    tr = _tile(S, 256)

    def body(x_ref, t_ref, dx_ref, acc_ref):
        e = x_ref[...] - t_ref[...]
        dx_ref[...] = e * (1.0 / D)
        _accumulate(acc_ref, e * e)

    return _row_call(body, S, tr, [x, target], [_rb(tr, D)] * 2,
                     [jax.ShapeDtypeStruct((S, D), F32), jax.ShapeDtypeStruct((1, D), F32)],
                     [_rb(tr, D), _pb(1, D)], "loss_head")


def _ln_bwd(dxo, x_in, branch, m6, row_gate, ln_g, alpha, name):
    S, D = dxo.shape
    tr = _tile(S, 256)

    def body(dxo_ref, x_ref, br_ref, m_ref, g_ref, ds_ref, dbr_ref, dg_ref, db_ref, dgate_ref):
        gate = m_ref[row_gate:row_gate + 1, :]
        br = br_ref[...]
        s = alpha * x_ref[...] + (1.0 + gate) * br
        mu = jnp.mean(s, axis=-1, keepdims=True)
        d = s - mu
        var = jnp.mean(d * d, axis=-1, keepdims=True)
        rstd = lax.rsqrt(var + LN_EPS)
        xhat = d * rstd
        dxo = dxo_ref[...]
        dxh = dxo * g_ref[...]
        ds = rstd * (dxh - jnp.mean(dxh, axis=-1, keepdims=True)
                     - xhat * jnp.mean(dxh * xhat, axis=-1, keepdims=True))
        ds_ref[...] = ds
        dbr_ref[...] = ((1.0 + gate) * ds).astype(BF16)
        _accumulate(dg_ref, dxo * xhat)
        _accumulate(db_ref, dxo)
        _accumulate(dgate_ref, ds * br)

    vec = jax.ShapeDtypeStruct((1, D), F32)
    return _row_call(body, S, tr, [dxo, x_in, branch, m6, ln_g],
                     [_rb(tr, D)] * 3 + [_pb(6, D), _pb(1, D)],
                     [jax.ShapeDtypeStruct((S, D), F32), jax.ShapeDtypeStruct((S, D), BF16), vec, vec, vec],
                     [_rb(tr, D), _rb(tr, D), _pb(1, D), _pb(1, D), _pb(1, D)], name)


def _mod_bwd(ds, dh, x_in, m6, row_scale, alpha, name):
    S, D = ds.shape
    tr = _tile(S, 256)

    def body(ds_ref, dh_ref, x_ref, m_ref, dx_ref, dsc_ref, dsh_ref):
        sc = m_ref[row_scale:row_scale + 1, :]
        dh = dh_ref[...]
        dx_ref[...] = alpha * ds_ref[...] + dh * (1.0 + sc)
        _accumulate(dsc_ref, dh * x_ref[...])
        _accumulate(dsh_ref, dh)

    vec = jax.ShapeDtypeStruct((1, D), F32)
    return _row_call(body, S, tr, [ds, dh, x_in, m6], [_rb(tr, D)] * 3 + [_pb(6, D)],
                     [jax.ShapeDtypeStruct((S, D), F32), vec, vec],
                     [_rb(tr, D), _pb(1, D), _pb(1, D)], name)


def _merge_bwd(dmerged, attn, g, z, b_glu, g_attn, g_ssm):
    S, W = attn.shape
    tr = _tile(S, 256)

    def rms_bwd(x, gamma, dy):
        r = lax.rsqrt(jnp.mean(x * x, axis=-1, keepdims=True) + LN_EPS)
        gdy = gamma * dy
        dx = gdy * r - x * (r * r * r) * jnp.mean(gdy * x, axis=-1, keepdims=True)
        return dx, dy * x * r

    def body(dm_ref, a_ref, g_ref, z_ref, b_ref, ga_ref, gs_ref, da_ref, dgp_ref, dz_ref, dga_ref, dgs_ref, db_ref):
        da, dga = rms_bwd(a_ref[...], ga_ref[...], dm_ref[:, 0:W])
        da_ref[...] = da
        _accumulate(dga_ref, dga)
        gv = g_ref[...]
        sig = 1.0 / (1.0 + jnp.exp(-(z_ref[...] + b_ref[...])))
        dssm, dgs = rms_bwd(gv * sig, gs_ref[...], dm_ref[:, W:2 * W])
        _accumulate(dgs_ref, dgs)
        dgp_ref[...] = dssm * sig
        dz = dssm * gv * sig * (1.0 - sig)
        dz_ref[...] = dz.astype(BF16)
        _accumulate(db_ref, dz)

    vec = jax.ShapeDtypeStruct((1, W), F32)
    mat = jax.ShapeDtypeStruct((S, W), F32)
    return _row_call(body, S, tr, [dmerged, attn, g, z, b_glu, g_attn, g_ssm],
                     [_rb(tr, 2 * W)] + [_rb(tr, W)] * 3 + [_pb(1, W)] * 3,
                     [mat, mat, jax.ShapeDtypeStruct((S, W), BF16), vec, vec, vec],
                     [_rb(tr, W)] * 3 + [_pb(1, W)] * 3, "merge_bwd")


def _act_bwd(dgp, dgl, pre, u, dskip):
    S, W = pre.shape
    tr = _tile(S, 256)

    def body(a_ref, b_ref, pre_ref, u_ref, d_ref, dy_ref, du_ref, dd_ref):
        dy = (a_ref[...] + b_ref[...]) * _gelu_grad(pre_ref[...])
        dy_ref[...] = dy
        du_ref[...] = dy * d_ref[...]
        _accumulate(dd_ref, dy * u_ref[...])

    mat = jax.ShapeDtypeStruct((S, W), F32)
    return _row_call(body, S, tr, [dgp, dgl, pre, u, dskip], [_rb(tr, W)] * 4 + [_pb(1, W)],
                     [mat, mat, jax.ShapeDtypeStruct((1, W), F32)], [_rb(tr, W)] * 2 + [_pb(1, W)], "act_bwd")


def _dproj(dq, dkT, dvT, du_skip, du0, du1, reach):
    S, W = dq.shape
    tr = _tile(S, 256)
    lanes = 2 * HEAD_DIM
    n_pairs, nkb = W // lanes, tr // lanes
    assert tr % lanes == 0 and reach % tr == 0

    def body(dq_ref, dk_ref, dv_ref, a_ref, b_ref, c_ref, o_ref):
        o_ref[0] = dq_ref[...].astype(BF16)
        for h in range(n_pairs):
            for b in range(nkb):
                rows, cols = slice(b * lanes, (b + 1) * lanes), slice(h * lanes, (h + 1) * lanes)
                o_ref[1, rows, cols] = dk_ref[h, b].T.astype(BF16)
                o_ref[2, rows, cols] = dv_ref[h, b].T.astype(BF16)
        o_ref[3] = (a_ref[...] + b_ref[...] + c_ref[...]).astype(BF16)

    tsp = pl.BlockSpec((n_pairs, nkb, lanes, lanes), lambda i: (0, i + reach // tr, 0, 0))
    return _row_call(body, S, tr, [dq, dkT, dvT, du_skip, du0, du1],
                     [_rb(tr, W), tsp, tsp, _rb(tr, W), _rb(tr, W), _rb(tr, W)],
                     jax.ShapeDtypeStruct((4, S, W), BF16), pl.BlockSpec((4, tr, W), lambda i: (0, i, 0)), "dproj")


def _attn_reach():
    return max(w // 2 for w, _ in ATTN_PATTERNS)


def _bias_table(tq, width, reach, head):
    i = lax.broadcasted_iota(jnp.int32, (tq, width), 0)
    j = lax.broadcasted_iota(jnp.int32, (tq, width), 1)
    ad = jnp.abs(j - i - reach)
    mult = jnp.zeros((tq, width), jnp.int32)
    for window, dil in ATTN_PATTERNS:
        assert dil & (dil - 1) == 0
        mult += ((jnp.bitwise_and(ad, dil - 1) == 0) & (ad <= window // 2)).astype(jnp.int32)
    logm = jnp.zeros((tq, width), F32)
    for n in range(2, len(ATTN_PATTERNS) + 1):
        logm = jnp.where(mult == n, math.log(n), logm)
    return logm, ad.astype(F32), mult > 0


def _bias_tables(n_heads, tq):
    reach = _attn_reach()
    width = tq + 2 * reach

    def body(o_ref):
        logm, ad, ok = _bias_table(tq, width, reach, None)
        head = (pl.program_id(0) + 1).astype(F32)
        slope = jnp.exp(jnp.full((tq, width), -8.0 * math.log(2.0) / n_heads, F32) * head)
        o_ref[...] = jnp.where(ok, logm - slope * ad, NEG_BIG)

    return pl.pallas_call(body, grid=(n_heads,), out_specs=pl.BlockSpec((None, tq, width), lambda h: (h, 0, 0)),
                          out_shape=jax.ShapeDtypeStruct((n_heads, tq, width), F32), compiler_params=_params(1),
                          name="attn_bias")()


def _attn_setup(tables_ref, bias_ref, sem, tq):
    hp, qi = pl.program_id(0), pl.program_id(1)

    @pl.when(qi == 0)
    def _():
        cp = pltpu.make_async_copy(tables_ref.at[pl.ds(2 * hp, 2)], bias_ref, sem)
        cp.start()
        cp.wait()

    lane = lax.broadcasted_iota(jnp.int32, (1, 2 * HEAD_DIM), 1)
    return pl.multiple_of(qi * tq, tq), [lane < HEAD_DIM, lane >= HEAD_DIM]


def _key_validity(first_key, n, seq):
    kpos = first_key + lax.broadcasted_iota(jnp.int32, (1, n), 1)
    return jnp.where((kpos >= 0) & (kpos < seq), 0.0, NEG_BIG).astype(F32)


_NT = (((1,), (1,)), ((), ()))
assert math.log2(HEAD_DIM) % 2 == 0


def _carrier_call(body, grid, in_specs, out_specs, out_shape, scratch, name, ins, riders):
    if riders:
        first = lambda: (pl.program_id(0) == 0) & (pl.program_id(1) == 0)
        last = lambda: (pl.program_id(0) == grid[0] - 1) & (pl.program_id(1) == grid[1] - 1)
        body = _hosted(body, len(in_specs), len(out_specs), len(scratch), riders, first, last)
        name += "_carrier"
    rider_ins = [a for t in riders for a in t.ins]
    res = pl.pallas_call(
        body, grid=grid, in_specs=in_specs + [ANY] * len(rider_ins), out_specs=out_specs + [ANY] * len(riders),
        out_shape=out_shape + [t.out_shape for t in riders],
        scratch_shapes=scratch + [s for t in riders for s in t.semaphores()],
        compiler_params=_params(2), name=name)(*ins, *rider_ins)
    return res[:len(out_specs)], res[len(out_specs):]


def _attention_fwd(qkv, kp, vp, tables, tq, riders=()):
    _, S, W = qkv.shape
    reach = _attn_reach()
    width = tq + 2 * reach
    Sp = S + 2 * reach
    scale = HEAD_DIM ** -0.5
    lanes = 2 * HEAD_DIM

    def body(q_ref, k_ref, v_ref, tables_ref, o_ref, lse_ref, bias_ref, sem):
        start, masks = _attn_setup(tables_ref, bias_ref, sem, tq)
        kw = k_ref[pl.ds(start, width), :]
        vw = v_ref[pl.ds(start, width), :]
        vbias = _key_validity(pl.program_id(1) * tq - reach, width, S)
        q = q_ref[...] * scale
        out = jnp.zeros((tq, lanes), F32)
        lse = jnp.zeros((tq, lanes), F32)
        for a in range(2):
            qa = jnp.where(masks[a], q, jnp.zeros_like(q))
            s = lax.dot_general(qa, kw, _NT, preferred_element_type=F32) + bias_ref[a] + vbias
            m = jnp.max(s, axis=-1, keepdims=True)
            p = jnp.exp(s - m)
            l = jnp.sum(p, axis=-1, keepdims=True)
            o = jnp.dot(p.astype(BF16), vw, preferred_element_type=F32) / l
            out = jnp.where(masks[a], o, out)
            lse = jnp.where(masks[a], m + jnp.log(l), lse)
        o_ref[...] = out
        lse_ref[...] = lse

    qsp = pl.BlockSpec((None, tq, lanes), lambda h, i: (0, i, h))
    ksp = pl.BlockSpec((Sp, lanes), lambda h, i: (0, h))
    osp = pl.BlockSpec((tq, lanes), lambda h, i: (i, h))
    mat = jax.ShapeDtypeStruct((S, W), F32)
    return _carrier_call(body, (W // lanes, S // tq), [qsp, ksp, ksp, ANY], [osp, osp], [mat, mat],
                           [pltpu.VMEM((2, tq, width), F32), pltpu.SemaphoreType.DMA], "attn_fwd",
                           (qkv, kp, vp, tables), list(riders))


def _attention_bwd(qkv, kp, vp, out, lse, dout, tables, tq, riders=()):
    _, S, W = qkv.shape
    reach = _attn_reach()
    width = tq + 2 * reach
    Sp = S + 2 * reach
    scale = HEAD_DIM ** -0.5
    lanes = 2 * HEAD_DIM
    kc = tq
    assert width % kc == 0 and kc % lanes == 0

    def body(q_ref, k_ref, v_ref, tables_ref, o_ref, lse_ref, do_ref, dq_ref, dk_ref, dv_ref, bias_ref, sem):
        start, masks = _attn_setup(tables_ref, bias_ref, sem, tq)
        qi = pl.program_id(1)

        @pl.when(qi == 0)
        def _():
            dk_ref[...] = jnp.zeros_like(dk_ref)
            dv_ref[...] = jnp.zeros_like(dv_ref)

        q = q_ref[...] * scale
        do = do_ref[...]
        prod = do * o_ref[...]
        lse_all = lse_ref[...]
        dq = jnp.zeros((tq, lanes), F32)
        for a in range(2):
            qa = jnp.where(masks[a], q, jnp.zeros_like(q))
            doa = jnp.where(masks[a], do, 0.0).astype(BF16)
            qat = qa.astype(F32).T.astype(BF16)
            doat = jnp.where(masks[a], do, 0.0).T.astype(BF16)
            lse_a = lse_all[:, a * HEAD_DIM:a * HEAD_DIM + 1]
            delta = jnp.sum(jnp.where(masks[a], prod, 0.0), axis=-1, keepdims=True)
            dq_a = jnp.zeros((tq, lanes), F32)
            for c in range(width // kc):
                kw = k_ref[pl.ds(start + c * kc, kc), :]
                vw = v_ref[pl.ds(start + c * kc, kc), :]
                s = (lax.dot_general(qa, kw, _NT, preferred_element_type=F32) + bias_ref[a, :, c * kc:(c + 1) * kc]
                     + _key_validity(qi * tq - reach + c * kc, kc, S))
                p = jnp.exp(s - lse_a)
                dp = lax.dot_general(doa, vw, _NT, preferred_element_type=F32)
                ds = (p * (dp - delta)).astype(BF16)
                dvt = jnp.dot(doat, p.astype(BF16), preferred_element_type=F32)
                dkt = jnp.dot(qat, ds, preferred_element_type=F32)
                dq_a += jnp.dot(ds, kw, preferred_element_type=F32)
                first = qi * (tq // lanes) + c * (kc // lanes)
                for j in range(kc // lanes):
                    dk_ref[first + j] += dkt[:, j * lanes:(j + 1) * lanes]
                    dv_ref[first + j] += dvt[:, j * lanes:(j + 1) * lanes]
            dq = jnp.where(masks[a], dq_a * scale, dq)
        dq_ref[...] = dq

    assert tq % lanes == 0 and reach % lanes == 0
    qsp = pl.BlockSpec((None, tq, lanes), lambda h, i: (0, i, h))
    ksp = pl.BlockSpec((Sp, lanes), lambda h, i: (0, h))
    osp = pl.BlockSpec((tq, lanes), lambda h, i: (i, h))
    tsp = pl.BlockSpec((None, Sp // lanes, lanes, lanes), lambda h, i: (h, 0, 0, 0))
    tiles = jax.ShapeDtypeStruct((W // lanes, Sp // lanes, lanes, lanes), F32)
    return _carrier_call(body, (W // lanes, S // tq), [qsp, ksp, ksp, ANY, osp, osp, osp], [osp, tsp, tsp],
                           [jax.ShapeDtypeStruct((S, W), F32), tiles, tiles],
                           [pltpu.VMEM((2, tq, width), F32), pltpu.SemaphoreType.DMA], "attn_bwd",
                           (qkv, kp, vp, tables, out, lse, dout), list(riders))


SB = GROUPS_PER_BLOCK * SSM_STATE
CB = GROUPS_PER_BLOCK * SSM_GROUP
RUNS = 8


def _powers(pw, a_r, a_i, n):
    p_r, p_i = a_r, a_i
    for j in range(n):
        pw[0, j:j + 1, :] = p_r
        pw[1, j:j + 1, :] = p_i
        p_r, p_i = p_r * a_r - p_i * a_i, p_r * a_i + p_i * a_r


def _rows8(j):
    return slice(j * RUNS, (j + 1) * RUNS)


def _run_scan(br, bi, T, a_r, a_i, pw, e_r, e_i, ent, down, conj):
    n = T // RUNS
    sg = -1.0 if conj else 1.0
    A_r = jnp.broadcast_to(a_r, (RUNS, SB))
    A_i = jnp.broadcast_to(sg * a_i, (RUNS, SB))
    x_r = x_i = None
    for j in (range(n - 1, -1, -1) if down else range(n)):
        b_r, b_i = br[_rows8(j), :], bi[_rows8(j), :]
        if x_r is None:
            x_r, x_i = b_r, b_i
        else:
            x_r, x_i = A_r * x_r - A_i * x_i + b_r, A_r * x_i + A_i * x_r + b_i
            br[_rows8(j), :] = x_r
            bi[_rows8(j), :] = x_i
    al_r, al_i = pw[0, n - 1:n, :], sg * pw[1, n - 1:n, :]
    s_r, s_i = e_r, e_i
    for c in (range(RUNS - 1, -1, -1) if down else range(RUNS)):
        ent[0, c:c + 1, :] = s_r
        ent[1, c:c + 1, :] = s_i
        s_r, s_i = (al_r * s_r - al_i * s_i + x_r[c:c + 1, :], al_r * s_i + al_i * s_r + x_i[c:c + 1, :])
    E_r, E_i = ent[0], ent[1]
    for j in range(n):
        p = n - 1 - j if down else j
        p_r, p_i = pw[0, p:p + 1, :], sg * pw[1, p:p + 1, :]
        v_r, v_i = br[_rows8(j), :], bi[_rows8(j), :]
        br[_rows8(j), :] = v_r + p_r * E_r - p_i * E_i
        bi[_rows8(j), :] = v_i + p_r * E_i + p_i * E_r
    return s_r, s_i


def _load_run_major(ref, T):
    n = T // RUNS
    return jnp.concatenate([ref[pl.ds(j, RUNS, stride=n), :] for j in range(n)], axis=0)


def _store_row_major(ref, value, T):
    n = T // RUNS
    for j in range(n):
        ref[pl.ds(j, RUNS, stride=n), :] = value[_rows8(j), :]


def _s5_specs(nb, T, rev, adjoint):
    flip = rev != adjoint
    blk = (lambda i: nb - 1 - i) if flip else (lambda i: i)
    usp = pl.BlockSpec((T, CB), lambda k, i: (blk(i), k))
    asp = pl.BlockSpec((None, 2, SB), lambda k, i: (k, 0, 0))
    wsp = pl.BlockSpec((None, CB, SB), lambda k, i: (k, 0, 0))
    csp = pl.BlockSpec((None, None, 2, SB), lambda k, i: (k, blk(i), 0, 0))
    return usp, asp, wsp, csp


def _s5_fwd(u, a, b_re, b_im, c_re, c_im, T, rev):
    S, W = u.shape
    NK = W // CB
    nb = S // T
    usp, asp, wsp, csp = _s5_specs(nb, T, rev, False)

    def body(u_ref, a_ref, bre_ref, bim_ref, cre_ref, cim_ref, y_ref, car_ref, wr, wi, st, pw, ent):
        a_r, a_i = a_ref[0:1, :], a_ref[1:2, :]

        @pl.when(pl.program_id(1) == 0)
        def _():
            st[...] = jnp.zeros_like(st)
            _powers(pw, a_r, a_i, T // RUNS)

        car_ref[...] = st[0:2, :]
        ub = _load_run_major(u_ref, T).astype(BF16)
        wr[...] = jnp.dot(ub, bre_ref[...], preferred_element_type=F32)
        wi[...] = jnp.dot(ub, bim_ref[...], preferred_element_type=F32)
        s_r, s_i = _run_scan(wr, wi, T, a_r, a_i, pw, st[0:1, :], st[1:2, :], ent, rev, False)
        st[0:1, :] = s_r
        st[1:2, :] = s_i
        y = (lax.dot_general(wr[...].astype(BF16), cre_ref[...], _NT, preferred_element_type=F32)
             - lax.dot_general(wi[...].astype(BF16), cim_ref[...], _NT, preferred_element_type=F32))
        _store_row_major(y_ref, y, T)

    return pl.pallas_call(
        body, grid=(NK, nb), in_specs=[usp, asp, wsp, wsp, wsp, wsp], out_specs=[usp, csp],
        out_shape=[jax.ShapeDtypeStruct((S, W), F32), jax.ShapeDtypeStruct((NK, nb, 2, SB), F32)],
        scratch_shapes=[pltpu.VMEM((T, SB), F32), pltpu.VMEM((T, SB), F32), pltpu.VMEM((8, SB), F32),
                        pltpu.VMEM((2, T // RUNS, SB), F32), pltpu.VMEM((2, RUNS, SB), F32)],
        compiler_params=_params(2), name="s5_fwd_rev" if rev else "s5_fwd")(u, a, b_re, b_im, c_re, c_im)


def _s5_bwd(u, dy, a, b_re, b_im, c_re, c_im, carries, T, rev, riders=()):
    S, W = u.shape
    NK = W // CB
    nb = S // T
    usp, asp, wsp, csp = _s5_specs(nb, T, rev, True)
    n = T // RUNS

    def body(u_ref, dy_ref, a_ref, bre_ref, bim_ref, cre_ref, cim_ref, car_ref,
             du_ref, dbre_ref, dbim_ref, dcre_ref, dcim_ref, da_ref, wr, wi, gr, gi, lam, pw, ent):
        a_r, a_i = a_ref[0:1, :], a_ref[1:2, :]

        @pl.when(pl.program_id(1) == 0)
        def _():
            lam[...] = jnp.zeros_like(lam)
            for r in (dbre_ref, dbim_ref, dcre_ref, dcim_ref, da_ref):
                r[...] = jnp.zeros_like(r)
            _powers(pw, a_r, a_i, n)

        u32 = _load_run_major(u_ref, T)
        ub = u32.astype(BF16)
        dyk = _load_run_major(dy_ref, T)
        dyb = dyk.astype(BF16)
        wr[...] = jnp.dot(ub, bre_ref[...], preferred_element_type=F32)
        wi[...] = jnp.dot(ub, bim_ref[...], preferred_element_type=F32)
        x0r, x0i = car_ref[0:1, :], car_ref[1:2, :]
        _run_scan(wr, wi, T, a_r, a_i, pw, x0r, x0i, ent, rev, False)

        gr[...] = jnp.dot(dyb, cre_ref[...], preferred_element_type=F32)
        gi[...] = -jnp.dot(dyb, cim_ref[...], preferred_element_type=F32)
        l_r, l_i = _run_scan(gr, gi, T, a_r, a_i, pw, lam[0:1, :], lam[1:2, :], ent, not rev, True)
        lam[0:1, :] = l_r
        lam[1:2, :] = l_i

        sub = lax.broadcasted_iota(jnp.int32, (RUNS, SB), 0)

        def before(buf, x0, j):
            if rev:
                if j < n - 1:
                    return buf[_rows8(j + 1), :]
                return jnp.where(sub == RUNS - 1, x0, pltpu.roll(buf[_rows8(0), :], RUNS - 1, 0))
            if j > 0:
                return buf[_rows8(j - 1), :]
            return jnp.where(sub == 0, x0, pltpu.roll(buf[_rows8(n - 1), :], 1, 0))

        acc_r = jnp.zeros((RUNS, SB), F32)
        acc_i = jnp.zeros((RUNS, SB), F32)
        for j in range(n):
            g_r, g_i = gr[_rows8(j), :], gi[_rows8(j), :]
            p_r, p_i = before(wr, x0r, j), before(wi, x0i, j)
            acc_r += g_r * p_r + g_i * p_i
            acc_i += g_i * p_r - g_r * p_i
        da_ref[0:1, :] += jnp.sum(acc_r, axis=0, keepdims=True)
        da_ref[1:2, :] += jnp.sum(acc_i, axis=0, keepdims=True)

        lrb, lib = gr[...].astype(BF16), gi[...].astype(BF16)
        du = (lax.dot_general(lrb, bre_ref[...], _NT, preferred_element_type=F32)
              + lax.dot_general(lib, bim_ref[...], _NT, preferred_element_type=F32))
        _store_row_major(du_ref, du, T)
        ut = u32.T.astype(BF16)
        dbre_ref[...] += jnp.dot(ut, lrb, preferred_element_type=F32)
        dbim_ref[...] += jnp.dot(ut, lib, preferred_element_type=F32)
        dyt = dyk.T.astype(BF16)
        dcre_ref[...] += jnp.dot(dyt, wr[...].astype(BF16), preferred_element_type=F32)
        dcim_ref[...] += jnp.dot(dyt, wi[...].astype(BF16), preferred_element_type=F32)

    blk = jax.ShapeDtypeStruct((NK, CB, SB), F32)
    return _carrier_call(
        body, (NK, nb), [usp, usp, asp, wsp, wsp, wsp, wsp, csp], [usp, wsp, wsp, wsp, wsp, asp],
        [jax.ShapeDtypeStruct((S, W), F32), blk, blk, blk, blk, jax.ShapeDtypeStruct((NK, 2, SB), F32)],
        [pltpu.VMEM((T, SB), F32), pltpu.VMEM((T, SB), F32), pltpu.VMEM((T, SB), F32), pltpu.VMEM((T, SB), F32),
         pltpu.VMEM((8, SB), F32), pltpu.VMEM((2, n, SB), F32), pltpu.VMEM((2, RUNS, SB), F32)],
        "s5_bwd_rev" if rev else "s5_bwd", (u, dy, a, b_re, b_im, c_re, c_im, carries), list(riders))


def _s5_discretize(lam_re, lam_im, log_step, b_re, b_im):
    step = jnp.exp(log_step)[..., None]
    mag = jnp.exp(lam_re * step)
    a_re, a_im = mag * jnp.cos(lam_im * step), mag * jnp.sin(lam_im * step)
    den = lam_re * lam_re + lam_im * lam_im
    coef_re = ((a_re - 1.0) * lam_re + a_im * lam_im) / den
    coef_im = (a_im * lam_re - (a_re - 1.0) * lam_im) / den
    bb_re = coef_re[..., None] * b_re - coef_im[..., None] * b_im
    bb_im = coef_re[..., None] * b_im + coef_im[..., None] * b_re
    return a_re, a_im, bb_re, bb_im


def _to_blocks(w_gcp):
    two, G, C, P = w_gcp.shape
    nk = G // GROUPS_PER_BLOCK
    x = w_gcp.reshape(two, nk, GROUPS_PER_BLOCK, C, P)
    eye = jnp.eye(GROUPS_PER_BLOCK, dtype=w_gcp.dtype)
    return jnp.einsum('dkgcp,gh->dkgchp', x, eye).reshape(two, nk, GROUPS_PER_BLOCK * C, GROUPS_PER_BLOCK * P)


def _from_blocks(blk):
    two, nk, cb, sb = blk.shape
    C, P = cb // GROUPS_PER_BLOCK, sb // GROUPS_PER_BLOCK
    x = blk.reshape(two, nk, GROUPS_PER_BLOCK, C, GROUPS_PER_BLOCK, P)
    eye = jnp.eye(GROUPS_PER_BLOCK, dtype=blk.dtype)
    return jnp.einsum('dkgchp,gh->dkgcp', x, eye).reshape(two, nk * GROUPS_PER_BLOCK, C, P)


def _me():
    return lax.axis_index("x"), lax.axis_index("y"), lax.axis_index("c")


def _peer(k):
    x, y, c = _me()
    return (1 - x if k & 4 else x, 1 - y if k & 2 else y, 1 - c if k & 1 else c)


def _logical(dev):
    return 4 * dev[0] + 2 * dev[1] + dev[2]


class _Transfer:
    def __init__(self, ks, src_of, dst_of, out_shape, ins, split=1):
        self.ks, self.src_of, self.dst_of, self.out_shape, self.ins, self.split = ks, src_of, dst_of, out_shape, ins, split

    def semaphores(self):
        n = len(self.ks) * self.split
        return [pltpu.SemaphoreType.DMA((n,)), pltpu.SemaphoreType.DMA((n,))]

    def copies(self, in_refs, out_ref, send, recv):
        out = []
        for k in self.ks:
            src, dst = self.src_of(k, in_refs, out_ref), self.dst_of(k, in_refs, out_ref)
            rows = src.shape[0] // self.split
            for q in range(self.split):
                part = pl.ds(q * rows, rows)
                j = len(out)
                out.append(pltpu.make_async_remote_copy(
                    src_ref=src if self.split == 1 else src.at[part],
                    dst_ref=dst if self.split == 1 else dst.at[part],
                    send_sem=send.at[j], recv_sem=recv.at[j], device_id=_peer(k), device_id_type=MESH))
        return out


def _exchange(name, t):
    n_in = len(t.ins)

    def body(*refs):
        copies = t.copies(refs[:n_in], refs[n_in], *refs[n_in + 1:])
        for cp in copies:
            cp.start()
        for cp in copies:
            cp.wait_recv()
        for cp in copies:
            cp.wait_send()

    return pl.pallas_call(body, in_specs=[ANY] * n_in, out_specs=ANY, out_shape=t.out_shape,
                          scratch_shapes=t.semaphores(), name=name)(*t.ins)


def _hosted(body, n_in, n_out, n_scratch, riders, is_first, is_last):
    n_rin = sum(len(t.ins) for t in riders)

    def wrapped(*refs):
        host_in, rider_in = refs[:n_in], refs[n_in:n_in + n_rin]
        pos = n_in + n_rin
        host_out, rider_out = refs[pos:pos + n_out], refs[pos + n_out:pos + n_out + len(riders)]
        pos += n_out + len(riders)
        host_scratch, sems = refs[pos:pos + n_scratch], refs[pos + n_scratch:]

        def copies():
            out, at = [], 0
            for i, t in enumerate(riders):
                out += t.copies(rider_in[at:at + len(t.ins)], rider_out[i], sems[2 * i], sems[2 * i + 1])
                at += len(t.ins)
            return out

        @pl.when(is_first())
        def _():
            for cp in copies():
                cp.start()

        body(*host_in, *host_out, *host_scratch)

        @pl.when(is_last())
        def _():
            for cp in copies():
                cp.wait_recv()
            for cp in copies():
                cp.wait_send()

    return wrapped


def _gather8_transfer(v):
    out = jax.ShapeDtypeStruct((N_DEV,) + v.shape, v.dtype)
    slot = lambda k, ins, o: o.at[_logical(_me())]
    return _Transfer(list(range(1, 8)), lambda k, ins, o: ins[0], slot, out, [v])


def _all_gather8(v, name):
    got = _exchange(name, _gather8_transfer(v))
    return lax.dynamic_update_index_in_dim(got, v, _logical(_me()), 0)


def _halves_transfer(v):
    R, C = v.shape
    half = lambda ins: ins[0].at[pl.ds(lax.axis_index("c") * (R // 2), R // 2)]
    chip = lambda: 2 * lax.axis_index("x") + lax.axis_index("y")
    return _Transfer([2, 4, 6], lambda k, ins, o: half(ins), lambda k, ins, o: o.at[chip()],
                     jax.ShapeDtypeStruct((4, R // 2, C), v.dtype), [v])


def _join_halves(v, halves, name):
    R, C = v.shape
    c = lax.axis_index("c")
    mine = lax.dynamic_slice_in_dim(v, c * (R // 2), R // 2, axis=0)
    halves = lax.dynamic_update_index_in_dim(halves, mine, 2 * lax.axis_index("x") + lax.axis_index("y"), 0)
    core = lambda o: o.at[:, lax.axis_index("c")]
    both = _exchange(name + "_join", _Transfer(
        [1], lambda k, ins, o: ins[0], lambda k, ins, o: core(o),
        jax.ShapeDtypeStruct((4, 2, R // 2, C), v.dtype), [halves], split=4))
    return lax.dynamic_update_index_in_dim(both, halves, c, 1).reshape(4, R, C)


def _scatter_transfer(g):
    four, two, R2, C = g.shape
    piece = lambda dev, ins: ins[0].at[2 * dev[0] + dev[1], dev[2]]
    slot = lambda k, ins, o: o.at[_logical(_me())]
    return _Transfer(list(range(1, 8)), lambda k, ins, o: piece(_peer(k), ins), slot,
                     jax.ShapeDtypeStruct((N_DEV, R2, C), g.dtype), [g])


def _reduce_scattered(g, landed, name):
    x, y, c = _me()
    mine = lax.dynamic_index_in_dim(lax.dynamic_index_in_dim(g, 2 * x + y, 0, False), c, 0, False)
    landed = lax.dynamic_update_index_in_dim(landed, mine, _logical(_me()), 0)
    part = _sum_leading(landed, "reduce_" + name)
    return _swap_c(part, "swap_" + name).reshape(2 * part.shape[0], part.shape[1])


def _swap_c(v, name):
    out = jax.ShapeDtypeStruct((2,) + v.shape, v.dtype)
    slot = lambda: lax.axis_index("c")
    got = _exchange(name, _Transfer([1], lambda k, ins, o: ins[0], lambda k, ins, o: o.at[slot()], out, [v],
                                    split=8))
    return lax.dynamic_update_index_in_dim(got, v, slot(), 0)


def _sum_leading(v, name):
    n, R, C = v.shape
    tr = _tile(R, max(8, (1 << 19) // (C * n)))

    def body(v_ref, o_ref):
        acc = v_ref[0].astype(F32)
        for s in range(1, n):
            acc = acc + v_ref[s].astype(F32)
        o_ref[...] = acc

    return pl.pallas_call(body, grid=(R // tr,), in_specs=[pl.BlockSpec((n, tr, C), lambda i: (0, i, 0))],
                          out_specs=pl.BlockSpec((tr, C), lambda i: (i, 0)),
                          out_shape=jax.ShapeDtypeStruct((R, C), F32), compiler_params=_params(1), name=name)(v)


def _adamw(w, g, m, v, name):
    R, C = w.shape
    tr = _tile(R, max(8, (1 << 18) // C))
    c1 = 1.0 - ADAM_B1 ** ADAM_STEP
    c2 = 1.0 - ADAM_B2 ** ADAM_STEP

    def body(w_ref, g_ref, m_ref, v_ref, d_ref, nm_ref, nv_ref):
        gv = g_ref[...]
        nm = ADAM_B1 * m_ref[...] + (1.0 - ADAM_B1) * gv
        nv = ADAM_B2 * v_ref[...] + (1.0 - ADAM_B2) * (gv * gv)
        nm_ref[...] = nm
        nv_ref[...] = nv
        d_ref[...] = -ADAM_LR * ((nm / c1) / (jnp.sqrt(nv / c2) + ADAM_EPS) + ADAM_WD * w_ref[...])

    sp = pl.BlockSpec((tr, C), lambda i: (i, 0))
    return pl.pallas_call(body, grid=(R // tr,), in_specs=[sp] * 4, out_specs=[sp] * 3,
                          out_shape=[jax.ShapeDtypeStruct((R, C), F32)] * 3, compiler_params=_params(1),
                          name=name)(w, g, m, v)


def _forward_layer(x, lw, w_in_full, shards, next_w_in, alpha, tables, tq, ts):
    S, D = x.shape
    W = D // 2
    reach = _attn_reach()
    m6 = lw['m6']
    gw = {'w_in': w_in_full}
    FF = shards['w_mlp1'].shape[1] * 4

    h = _modulate(x, m6, 0, 1, "modulate1")
    qkv = _mm(h, gw['w_in'], M=S, N=3 * W, K=D, b_st='n', o_st='n', ns=3, out_dtype=BF16, cap_k=2048, name="proj_qkv")
    u = _mm(h, gw['w_in'][3], M=S, N=W, K=D, out_dtype=F32, cap_k=2048, name="proj_u")
    pad = ((reach, reach), (0, 0))
    kp, vp = jnp.pad(qkv[1], pad), jnp.pad(qkv[2], pad)
    carried = [(n, shards[n]) for n in BIG[1:]] + ([('w_in', next_w_in)] if next_w_in is not None else [])
    (attn, lse), halves = _attention_fwd(qkv, kp, vp, tables, tq, [_halves_transfer(v) for _, v in carried])
    joined = [_join_halves(v, hv, "gather_" + n) for (n, v), hv in zip(carried, halves)]
    gw.update({n: j for (n, _), j in zip(carried[:len(BIG) - 1], joined)})
    w_in_next = joined[-1] if next_w_in is not None else None
    y, carries = zip(*[_s5_fwd(u, lw['a'][d], lw['b_re'][d], lw['b_im'][d], lw['c_re'][d], lw['c_im'][d], ts,
                               rev=bool(d)) for d in range(2)])
    pre, g = _ssm_act(u, y[0], y[1], lw['ssm_d'])
    z = _mm(g, gw['w_glu'].reshape(W, W), M=S, N=W, K=W, out_dtype=F32, name="glu")
    merged = _merge(attn, g, z, lw['b_glu'], lw['g_attn'], lw['g_ssm'])
    mix = _mm(merged, gw['w_out'].reshape(D, D), M=S, N=D, K=D, out_dtype=F32, cap_k=2048, name="out_proj")
    x1 = _post_ln(x, mix, m6, 2, lw['ln1_g'], lw['ln1_b'], alpha, "post_ln1")
    h2 = _modulate(x1, m6, 3, 4, "modulate2")
    a4 = _mm(h2, gw['w_mlp1'], M=S, N=FF, K=D, b_st='n', o_st='n', ns=4, out_dtype=F32, cap_k=2048, name="mlp1")
    ff = _mm(a4, gw['w_mlp2'], M=S, N=D, K=FF, a_st='k', b_st='k', ns=4, a_tf=_relu2, out_dtype=F32, name="mlp2")
    x2 = _post_ln(x1, ff, m6, 5, lw['ln2_g'], lw['ln2_b'], alpha, "post_ln2")
    saved = dict(x=x, h=h, qkv=qkv, u=u, attn=attn, lse=lse, carries=carries, pre=pre, g=g, z=z, merged=merged,
                 mix=mix, x1=x1, h2=h2, a4=a4, ff=ff, gw=gw)
    return x2, saved, w_in_next


SMALL_GRADS = ['a', 'bb_re', 'bb_im', 'cc_re', 'cc_im', 'ssm_d', 'b_glu', 'g_attn', 'g_ssm',
               'ln1_g', 'ln1_b', 'ln2_g', 'ln2_b']


def _as_halves(g):
    C = g.shape[-1]
    return g.reshape(4, 2, g.size // (8 * C), C)


def _backward_layer(dx2, lw, sv, pending_w_in, alpha, tables, tq, ts):
    S, D = dx2.shape
    W = D // 2
    reach = _attn_reach()
    m6, gw = lw['m6'], sv['gw']
    FF = gw['w_mlp1'].shape[2] * 4
    gr = {}

    ds2, dff, gr['ln2_g'], gr['ln2_b'], d_g2 = _ln_bwd(dx2, sv['x1'], sv['ff'], m6, 5, lw['ln2_g'], alpha, "ln2_bwd")
    da4 = _mm(dff, gw['w_mlp2'], M=S, N=FF, K=D, tb=True, b_st='n', o_st='n', ns=4, extra=sv['a4'],
              epi=lambda acc, a: 2.0 * jnp.maximum(a, 0.0) * acc, out_dtype=BF16, cap_k=2048, name="mlp2_bwd_x")
    gr['w_mlp2'] = _mm(sv['a4'], dff, M=FF, N=D, K=S, ta=True, a_st='m', o_st='m', ns=4, a_tf=_relu2,
                       out_dtype=BF16, name="mlp2_bwd_w")
    dh2 = _mm(da4, gw['w_mlp1'], M=S, N=D, K=FF, tb=True, a_st='k', b_st='k', ns=4, out_dtype=F32, name="mlp1_bwd_x")
    gr['w_mlp1'] = _mm(sv['h2'], da4, M=D, N=FF, K=S, ta=True, b_st='n', o_st='n', ns=4, out_dtype=BF16,
                       name="mlp1_bwd_w")
    dx1, d_sc2, d_sh2 = _mod_bwd(ds2, dh2, sv['x1'], m6, 4, alpha, "mod2_bwd")

    ds1, dmix, gr['ln1_g'], gr['ln1_b'], d_g1 = _ln_bwd(dx1, sv['x'], sv['mix'], m6, 2, lw['ln1_g'], alpha, "ln1_bwd")
    dmerged = _mm(dmix, gw['w_out'].reshape(D, D), M=S, N=D, K=D, tb=True, out_dtype=F32, cap_k=2048, name="out_proj_bwd_x")
    gr['w_out'] = _mm(sv['merged'], dmix, M=D, N=D, K=S, ta=True, out_dtype=BF16, name="out_proj_bwd_w")
    dattn, dgp, dz, gr['g_attn'], gr['g_ssm'], gr['b_glu'] = _merge_bwd(
        dmerged, sv['attn'], sv['g'], sv['z'], lw['b_glu'], lw['g_attn'], lw['g_ssm'])
    dgl = _mm(dz, gw['w_glu'].reshape(W, W), M=S, N=W, K=W, tb=True, out_dtype=F32, name="glu_bwd_x")
    gr['w_glu'] = _mm(sv['g'], dz, M=W, N=W, K=S, ta=True, out_dtype=BF16, name="glu_bwd_w")
    dy, du_skip, gr['ssm_d'] = _act_bwd(dgp, dgl, sv['pre'], sv['u'], lw['ssm_d'])
    mlp = [_as_halves(gr['w_mlp2']), _as_halves(gr['w_mlp1'])]
    s5 = [_s5_bwd(sv['u'], dy, lw['a'][d], lw['b_re'][d], lw['b_im'][d], lw['c_re'][d], lw['c_im'][d],
                  sv['carries'][d], ts, rev=bool(d), riders=[_scatter_transfer(mlp[d])]) for d in range(2)]
    gr['w_mlp2'] = _reduce_scattered(mlp[0], s5[0][1][0], 'w_mlp2')
    gr['w_mlp1'] = _reduce_scattered(mlp[1], s5[1][1][0], 'w_mlp1')
    du = [s5[d][0][0] for d in range(2)]
    for pos, n in enumerate(['bb_re', 'bb_im', 'cc_re', 'cc_im', 'a']):
        gr[n] = jnp.stack([s5[d][0][pos + 1] for d in range(2)])
    pad = ((reach, reach), (0, 0))
    kp, vp = jnp.pad(sv['qkv'][1], pad), jnp.pad(sv['qkv'][2], pad)
    for n in ['bb_re', 'bb_im', 'cc_re', 'cc_im']:
        gr[n] = _from_blocks(gr[n])
    small, _ = _pack([gr[n] for n in SMALL_GRADS], 1024)
    scattered = [(n, _as_halves(gr[n])) for n in ('w_glu', 'w_out')]
    if pending_w_in is not None:
        scattered.append(('w_in', pending_w_in))
    riders = [_scatter_transfer(g) for _, g in scattered] + [_gather8_transfer(small)]
    (dq, dkT, dvT), landed = _attention_bwd(sv['qkv'], kp, vp, sv['attn'], sv['lse'], dattn, tables, tq, riders)
    reduced = [_reduce_scattered(g, lv, n) for (n, g), lv in zip(scattered, landed)]
    gr.update({n: r for (n, _), r in zip(scattered[:2], reduced)})
    w_in_above = reduced[-1] if pending_w_in is not None else None
    own = lax.dynamic_update_index_in_dim(landed[-1], small, _logical(_me()), 0)
    summed = _unpack(_sum_leading(own, "reduce_small"), [gr[n] for n in SMALL_GRADS])
    gr.update(dict(zip(SMALL_GRADS, summed)))
    dproj = _dproj(dq, dkT, dvT, du_skip, du[0], du[1], reach)
    dh = _mm(dproj, gw['w_in'], M=S, N=D, K=4 * W, tb=True, a_st='k', b_st='k', ns=4, out_dtype=F32, cap_n=2048, name="proj_bwd_x")
    gr['w_in'] = _mm(sv['h'], dproj, M=D, N=4 * W, K=S, ta=True, b_st='n', o_st='n', ns=4, out_dtype=BF16,
                     name="proj_bwd_w")
    dx, d_sc1, d_sh1 = _mod_bwd(ds1, dh, sv['x'], m6, 1, alpha, "mod1_bwd")
    gr['m6'] = jnp.concatenate([d_sh1, d_sc1, d_g1, d_sh2, d_sc2, d_g2], axis=0)
    return dx, gr, w_in_above


def _pack(arrays, width):
    flat = jnp.concatenate([a.reshape(-1) for a in arrays])
    n = flat.shape[0]
    rows = -(-n // (8 * width)) * 8
    return jnp.pad(flat, (0, rows * width - n)).reshape(rows, width), n


def _unpack(packed, like):
    flat = packed.reshape(-1)
    out, pos = [], 0
    for a in like:
        out.append(flat[pos:pos + a.size].reshape(a.shape))
        pos += a.size
    return out


def kernel(x, c, w_ada, b_ada, w_in, ssm_lam_re, ssm_lam_im, ssm_log_step, ssm_b_re, ssm_b_im, ssm_c_re, ssm_c_im, ssm_d, w_glu, b_glu, g_attn, g_ssm, w_out, ln1_g, ln1_b, w_mlp1, w_mlp2, ln2_g, ln2_b, loss_target, m_w_ada, m_b_ada, m_w_in, m_ssm_lam_re, m_ssm_lam_im, m_ssm_log_step, m_ssm_b_re, m_ssm_b_im, m_ssm_c_re, m_ssm_c_im, m_ssm_d, m_w_glu, m_b_glu, m_g_attn, m_g_ssm, m_w_out, m_ln1_g, m_ln1_b, m_w_mlp1, m_w_mlp2, m_ln2_g, m_ln2_b, v_w_ada, v_b_ada, v_w_in, v_ssm_lam_re, v_ssm_lam_im, v_ssm_log_step, v_ssm_b_re, v_ssm_b_im, v_ssm_c_re, v_ssm_c_im, v_ssm_d, v_w_glu, v_b_glu, v_g_attn, v_g_ssm, v_w_out, v_ln1_g, v_ln1_b, v_w_mlp1, v_w_mlp2, v_ln2_g, v_ln2_b):
    args = locals()
    w = {n: args[n] for n in WEIGHTS}
    mom = {n: args['m_' + n] for n in WEIGHTS}
    var = {n: args['v_' + n] for n in WEIGHTS}
    L, D, ada_cols = w_ada.shape
    S = x.shape[1]
    W = D // 2
    n_heads = W // HEAD_DIM
    alpha = (2 * L) ** 0.25
    tq = _tile(S, 512)
    ts = _tile(S, 512)
    tables = _bias_tables(n_heads, tq)
    xi, yi, ci = _me()
    shard = 2 * xi + yi
    me = 4 * xi + 2 * yi + ci

    c_all = _all_gather8(c, "gather_c").reshape(N_DEV, D)
    c16 = jnp.pad(c_all, ((0, 16 - N_DEV), (0, 0)))
    b_sh = lax.dynamic_slice_in_dim(b_ada, shard * ada_cols, ada_cols, axis=1)
    mods_sh = jnp.stack([_mm(c16, w_ada[l], M=16, N=ada_cols, K=D, a_tf=_silu, bias=b_sh[l:l + 1], out_dtype=F32,
                             name="ada") for l in range(L)])
    mods_all = _all_gather8(mods_sh, "gather_mods")
    mods = jnp.concatenate([mods_all[2 * j] for j in range(4)], axis=-1)
    m6_all = lax.dynamic_index_in_dim(mods, me, axis=1, keepdims=False).reshape(L, 6, D)

    (a_re, a_im, bb_re, bb_im), disc_vjp = jax.vjp(
        _s5_discretize, ssm_lam_re, ssm_lam_im, ssm_log_step, ssm_b_re, ssm_b_im)
    N = a_re.shape[2] * a_re.shape[3]
    per_layer = dict(
        m6=m6_all,
        a=jnp.stack([a_re.reshape(L, 2, N // SB, SB), a_im.reshape(L, 2, N // SB, SB)], axis=3),
        b_re=jax.vmap(_to_blocks)(jnp.swapaxes(bb_re, -1, -2)).astype(BF16),
        b_im=jax.vmap(_to_blocks)(jnp.swapaxes(bb_im, -1, -2)).astype(BF16),
        c_re=jax.vmap(_to_blocks)(ssm_c_re).astype(BF16),
        c_im=jax.vmap(_to_blocks)(ssm_c_im).astype(BF16),
        ssm_d=ssm_d[:, None], b_glu=b_glu[:, None], g_attn=g_attn[:, None], g_ssm=g_ssm[:, None],
        ln1_g=ln1_g[:, None], ln1_b=ln1_b[:, None], ln2_g=ln2_g[:, None], ln2_b=ln2_b[:, None],
        w_in=w_in, w_glu=w_glu, w_out=w_out, w_mlp1=w_mlp1, w_mlp2=w_mlp2)

    layers = [{n: v[l] for n, v in per_layer.items()} for l in range(L)]
    shards = [{n: layers[l][n].astype(BF16) for n in BIG} for l in range(L)]
    first = shards[0]['w_in']
    w_in_full = _join_halves(first, _exchange("gather_w_in", _halves_transfer(first)), "gather_w_in")
    xc, saved = x[0], []
    for l in range(L):
        xc, sv, w_in_full = _forward_layer(xc, layers[l], w_in_full, shards[l],
                                           shards[l + 1]['w_in'] if l + 1 < L else None, alpha, tables, tq, ts)
        saved.append(sv)
    dx, sq = _loss_head(xc, loss_target[0])
    loss = lax.psum(0.5 * jnp.sum(sq) / D, ("x", "y", "c"))

    per_layer_grads = [None] * L
    pending = None
    for l in reversed(range(L)):
        dx, per_layer_grads[l], above = _backward_layer(dx, layers[l], saved[l], pending, alpha, tables, tq, ts)
        if pending is not None:
            per_layer_grads[l + 1]['w_in'] = above
        pending = _as_halves(per_layer_grads[l]['w_in'])
    per_layer_grads[0]['w_in'] = _reduce_scattered(
        pending, _exchange("scatter_w_in", _scatter_transfer(pending)), 'w_in')
    gr = {n: jnp.stack([g[n] for g in per_layer_grads]) for n in per_layer_grads[0]}
    grads = {n: gr[n].reshape(w[n].shape) for n in BIG}

    dm_all = _all_gather8(gr['m6'].reshape(L, 6 * D), "gather_dmods")
    grads['b_ada'] = _sum_leading(jnp.pad(dm_all, ((0, 0), (0, 8 - L), (0, 0))), "reduce_b_ada")[:L]
    dm16 = jnp.pad(lax.dynamic_slice_in_dim(dm_all, shard * ada_cols, ada_cols, axis=2), ((0, 16 - N_DEV), (0, 0), (0, 0)))
    grads['w_ada'] = jnp.stack([_mm(c16, dm16[:, l], M=D, N=ada_cols, K=16, ta=True, a_tf=_silu, out_dtype=F32,
                                    name="ada_bwd_w") for l in range(L)])

    red = gr
    G, P = ssm_lam_re.shape[2], ssm_lam_re.shape[3]
    d_a = jnp.swapaxes(red['a'], 2, 3).reshape(L, 2, 2, G, P)
    (grads['ssm_lam_re'], grads['ssm_lam_im'], grads['ssm_log_step'], grads['ssm_b_re'],
     grads['ssm_b_im']) = disc_vjp((d_a[:, :, 0], d_a[:, :, 1], jnp.swapaxes(red['bb_re'], -1, -2),
                                    jnp.swapaxes(red['bb_im'], -1, -2)))
    grads['ssm_c_re'] = red['cc_re']
    grads['ssm_c_im'] = -red['cc_im']
    for n in ['ssm_d', 'b_glu', 'g_attn', 'g_ssm', 'ln1_g', 'ln1_b', 'ln2_g', 'ln2_b']:
        grads[n] = red[n].reshape(w[n].shape)

    delta, new_m, new_v = {}, {}, {}
    for n in ['w_ada'] + BIG:
        two_d = lambda t: t.reshape(-1, t.shape[-1])
        d_, m_, v_ = _adamw(two_d(w[n]), two_d(grads[n]), two_d(mom[n]), two_d(var[n]), "adamw_" + n)
        delta[n], new_m[n], new_v[n] = d_.reshape(w[n].shape), m_.reshape(w[n].shape), v_.reshape(w[n].shape)
    pk = lambda d: _pack([d[n] for n in SMALL], 1024)[0]
    d_, m_, v_ = _adamw(pk(w), pk(grads), pk(mom), pk(var), "adamw_small")
    like = [w[n] for n in SMALL]
    for n, dn, mn, vn in zip(SMALL, _unpack(d_, like), _unpack(m_, like), _unpack(v_, like)):
        delta[n], new_m[n], new_v[n] = dn, mn, vn

    return (loss, dx[None], *[grads[n] for n in WEIGHTS], *[delta[n] for n in WEIGHTS],
            *[new_m[n] for n in WEIGHTS], *[new_v[n] for n in WEIGHTS])
```

```python
import functools
import math

import jax
import jax.numpy as jnp
from jax import lax
from jax.experimental import pallas as pl
from jax.experimental.pallas import tpu as pltpu

F32 = jnp.float32
BF16 = jnp.bfloat16
MESH = pl.DeviceIdType.MESH
ANY = pl.BlockSpec(memory_space=pl.ANY)

HEAD_DIM = 64
SSM_GROUP = 16
SSM_STATE = 64
GROUPS_PER_BLOCK = 8
ATTN_PATTERNS = ((128, 1), (512, 4), (2048, 16))
LN_EPS = 1e-5
NEG_BIG = -1e30
ADAM_LR, ADAM_B1, ADAM_B2, ADAM_EPS, ADAM_WD, ADAM_STEP = 0.001, 0.9, 0.999, 1e-08, 0.01, 10
VMEM_LIMIT_V7X = 56 * 1024 * 1024
N_DEV = 8

WEIGHTS = ['w_ada', 'b_ada', 'w_in', 'ssm_lam_re', 'ssm_lam_im', 'ssm_log_step', 'ssm_b_re', 'ssm_b_im',
           'ssm_c_re', 'ssm_c_im', 'ssm_d', 'w_glu', 'b_glu', 'g_attn', 'g_ssm', 'w_out', 'ln1_g', 'ln1_b',
           'w_mlp1', 'w_mlp2', 'ln2_g', 'ln2_b']
BIG = ['w_in', 'w_glu', 'w_out', 'w_mlp1', 'w_mlp2']
SMALL = [n for n in WEIGHTS if n not in BIG and n != 'w_ada']


def _params(n_grid):
    return pltpu.CompilerParams(dimension_semantics=("arbitrary",) * n_grid, vmem_limit_bytes=VMEM_LIMIT_V7X)


def _tile(n, cap):
    t = 1 << (max(1, min(n, cap)).bit_length() - 1)
    while n % t:
        t //= 2
    return t


def _operand_spec(stack, transposed, tr, tc, nrb, ncb, pick):
    def index(i, j, k):
        r, c = pick(i, j, k)
        s = None
        if stack == 'r':
            s, r = r // nrb, r % nrb
        elif stack == 'c':
            s, c = c // ncb, c % ncb
        idx = (c, r) if transposed else (r, c)
        return idx if s is None else (s,) + idx
    blk = (tc, tr) if transposed else (tr, tc)
    return pl.BlockSpec(blk if stack is None else (None,) + blk, index)


def _mm(a, b, *, M, N, K, name, out_dtype, ta=False, tb=False, a_st=None, b_st=None, o_st=None, ns=1,
        a_tf=None, epi=None, extra=None, bias=None, cap_m=1024, cap_n=1024, cap_k=2048):
    pm = M // ns if 'm' in (a_st, o_st) else M
    pn = N // ns if 'n' in (b_st, o_st) else N
    pk = K // ns if 'k' in (a_st, b_st) else K
    tm, tn, tk = _tile(pm, cap_m), _tile(pn, cap_n), _tile(pk, cap_k)
    gm, gn, gk = M // tm, N // tn, K // tk
    nmb, nnb, nkb = pm // tm, pn // tn, pk // tk
    a_spec = _operand_spec({None: None, 'm': 'r', 'k': 'c'}[a_st], ta, tm, tk, nmb, nkb, lambda i, j, k: (i, k))
    b_spec = _operand_spec({None: None, 'k': 'r', 'n': 'c'}[b_st], tb, tk, tn, nkb, nnb, lambda i, j, k: (k, j))
    o_spec = _operand_spec({None: None, 'm': 'r', 'n': 'c'}[o_st], False, tm, tn, nmb, nnb, lambda i, j, k: (i, j))
    dn = (((0 if ta else 1,), (1 if tb else 0,)), ((), ()))
    ins, specs = [a, b], [a_spec, b_spec]
    if extra is not None:
        ins.append(extra)
        specs.append(o_spec)
    if bias is not None:
        ins.append(bias)
        specs.append(pl.BlockSpec((1, tn), lambda i, j, k: (0, j)))
    n_in = len(ins)
    if o_st is None:
        o_shape = (M, N)
    elif o_st == 'm':
        o_shape = (ns, pm, N)
    else:
        o_shape = (ns, M, pn)

    def body(*refs):
        a_ref, b_ref = refs[0], refs[1]
        o_ref, acc = refs[n_in], refs[n_in + 1]
        k = pl.program_id(2)

        @pl.when(k == 0)
        def _():
            acc[...] = jnp.zeros_like(acc)

        av = a_ref[...]
        if a_tf is not None:
            av = a_tf(av.astype(F32))
        acc[...] += lax.dot_general(av.astype(BF16), b_ref[...].astype(BF16), dn, preferred_element_type=F32)

        @pl.when(k == gk - 1)
        def _():
            r = acc[...]
            pos = 2
            if extra is not None:
                r = epi(r, refs[pos][...].astype(F32))
                pos += 1
            elif epi is not None:
                r = epi(r)
            if bias is not None:
                r = r + refs[pos][...]
            o_ref[...] = r.astype(o_ref.dtype)

    return pl.pallas_call(
        body, grid=(gm, gn, gk), in_specs=specs, out_specs=o_spec,
        out_shape=jax.ShapeDtypeStruct(o_shape, out_dtype),
        scratch_shapes=[pltpu.VMEM((tm, tn), F32)], compiler_params=_params(3), name=name)(*ins)


def _relu2(v):
    r = jnp.maximum(v, 0.0)
    return r * r


def _silu(v):
    return v / (1.0 + jnp.exp(-v))


def _rb(tr, w):
    return pl.BlockSpec((tr, w), lambda i: (i, 0))


def _pb(r, w):
    return pl.BlockSpec((r, w), lambda i: (0, 0))


def _row_call(body, n_rows, tr, ins, in_specs, outs, out_specs, name):
    return pl.pallas_call(body, grid=(n_rows // tr,), in_specs=in_specs, out_specs=out_specs,
                          out_shape=outs, compiler_params=_params(1), name=name)(*ins)


def _accumulate(ref, value):
    @pl.when(pl.program_id(0) == 0)
    def _():
        ref[...] = jnp.zeros_like(ref)
    ref[...] += jnp.sum(value, axis=0, keepdims=True)


def _modulate(x, m6, row_shift, row_scale, name):
    S, D = x.shape
    tr = _tile(S, 256)

    def body(x_ref, m_ref, h_ref):
        sh = m_ref[row_shift:row_shift + 1, :]
        sc = m_ref[row_scale:row_scale + 1, :]
        h_ref[...] = (x_ref[...] * (1.0 + sc) + sh).astype(BF16)

    return _row_call(body, S, tr, [x, m6], [_rb(tr, D), _pb(6, D)],
                     jax.ShapeDtypeStruct((S, D), BF16), _rb(tr, D), name)


def _gelu(v):
    c = math.sqrt(2.0 / math.pi)
    return 0.5 * v * (1.0 + jnp.tanh(c * (v + 0.044715 * v * v * v)))


def _gelu_grad(v):
    c = math.sqrt(2.0 / math.pi)
    t = jnp.tanh(c * (v + 0.044715 * v * v * v))
    return 0.5 * (1.0 + t) + 0.5 * v * (1.0 - t * t) * c * (1.0 + 3.0 * 0.044715 * v * v)


def _ssm_act(u, y0, y1, dskip):
    S, W = u.shape
    tr = _tile(S, 256)

    def body(u_ref, y0_ref, y1_ref, d_ref, pre_ref, g_ref):
        pre = u_ref[...] * d_ref[...] + y0_ref[...] + y1_ref[...]
        pre_ref[...] = pre
        g_ref[...] = _gelu(pre)

    return _row_call(body, S, tr, [u, y0, y1, dskip], [_rb(tr, W)] * 3 + [_pb(1, W)],
                     [jax.ShapeDtypeStruct((S, W), F32)] * 2, [_rb(tr, W)] * 2, "ssm_act")


def _merge(attn, g, z, b_glu, g_attn, g_ssm):
    S, W = attn.shape
    tr = _tile(S, 256)

    def body(a_ref, g_ref, z_ref, b_ref, ga_ref, gs_ref, o_ref):
        a = a_ref[...]
        ra = lax.rsqrt(jnp.mean(a * a, axis=-1, keepdims=True) + LN_EPS)
        o_ref[:, 0:W] = (a * ra * ga_ref[...]).astype(BF16)
        s = g_ref[...] / (1.0 + jnp.exp(-(z_ref[...] + b_ref[...])))
        rs = lax.rsqrt(jnp.mean(s * s, axis=-1, keepdims=True) + LN_EPS)
        o_ref[:, W:2 * W] = (s * rs * gs_ref[...]).astype(BF16)

    return _row_call(body, S, tr, [attn, g, z, b_glu, g_attn, g_ssm],
                     [_rb(tr, W)] * 3 + [_pb(1, W)] * 3,
                     jax.ShapeDtypeStruct((S, 2 * W), BF16), _rb(tr, 2 * W), "merge")


def _post_ln(x, branch, m6, row_gate, ln_g, ln_b, alpha, name):
    S, D = x.shape
    tr = _tile(S, 256)

    def body(x_ref, br_ref, m_ref, g_ref, b_ref, o_ref):
        gate = m_ref[row_gate:row_gate + 1, :]
        s = alpha * x_ref[...] + (1.0 + gate) * br_ref[...]
        mu = jnp.mean(s, axis=-1, keepdims=True)
        d = s - mu
        var = jnp.mean(d * d, axis=-1, keepdims=True)
        o_ref[...] = d * lax.rsqrt(var + LN_EPS) * g_ref[...] + b_ref[...]

    return _row_call(body, S, tr, [x, branch, m6, ln_g, ln_b],
                     [_rb(tr, D), _rb(tr, D), _pb(6, D), _pb(1, D), _pb(1, D)],
                     jax.ShapeDtypeStruct((S, D), F32), _rb(tr, D), name)


def _loss_head(x, target):
    S, D = x.shape
    tr = _tile(S, 256)

    def body(x_ref, t_ref, dx_ref, acc_ref):
        e = x_ref[...] - t_ref[...]
        dx_ref[...] = e * (1.0 / D)
        _accumulate(acc_ref, e * e)

    return _row_call(body, S, tr, [x, target], [_rb(tr, D)] * 2,
                     [jax.ShapeDtypeStruct((S, D), F32), jax.ShapeDtypeStruct((1, D), F32)],
                     [_rb(tr, D), _pb(1, D)], "loss_head")


def _ln_bwd(dxo, x_in, branch, m6, row_gate, ln_g, alpha, name):
    S, D = dxo.shape
    tr = _tile(S, 256)

    def body(dxo_ref, x_ref, br_ref, m_ref, g_ref, ds_ref, dbr_ref, dg_ref, db_ref, dgate_ref):
        gate = m_ref[row_gate:row_gate + 1, :]
        br = br_ref[...]
        s = alpha * x_ref[...] + (1.0 + gate) * br
        mu = jnp.mean(s, axis=-1, keepdims=True)
        d = s - mu
        var = jnp.mean(d * d, axis=-1, keepdims=True)
        rstd = lax.rsqrt(var + LN_EPS)
        xhat = d * rstd
        dxo = dxo_ref[...]
        dxh = dxo * g_ref[...]
        ds = rstd * (dxh - jnp.mean(dxh, axis=-1, keepdims=True)
                     - xhat * jnp.mean(dxh * xhat, axis=-1, keepdims=True))
        ds_ref[...] = ds
        dbr_ref[...] = ((1.0 + gate) * ds).astype(BF16)
        _accumulate(dg_ref, dxo * xhat)
        _accumulate(db_ref, dxo)
        _accumulate(dgate_ref, ds * br)

    vec = jax.ShapeDtypeStruct((1, D), F32)
    return _row_call(body, S, tr, [dxo, x_in, branch, m6, ln_g],
                     [_rb(tr, D)] * 3 + [_pb(6, D), _pb(1, D)],
                     [jax.ShapeDtypeStruct((S, D), F32), jax.ShapeDtypeStruct((S, D), BF16), vec, vec, vec],
                     [_rb(tr, D), _rb(tr, D), _pb(1, D), _pb(1, D), _pb(1, D)], name)


def _mod_bwd(ds, dh, x_in, m6, row_scale, alpha, name):
    S, D = ds.shape
    tr = _tile(S, 256)

    def body(ds_ref, dh_ref, x_ref, m_ref, dx_ref, dsc_ref, dsh_ref):
        sc = m_ref[row_scale:row_scale + 1, :]
        dh = dh_ref[...]
        dx_ref[...] = alpha * ds_ref[...] + dh * (1.0 + sc)
        _accumulate(dsc_ref, dh * x_ref[...])
        _accumulate(dsh_ref, dh)

    vec = jax.ShapeDtypeStruct((1, D), F32)
    return _row_call(body, S, tr, [ds, dh, x_in, m6], [_rb(tr, D)] * 3 + [_pb(6, D)],
                     [jax.ShapeDtypeStruct((S, D), F32), vec, vec],
                     [_rb(tr, D), _pb(1, D), _pb(1, D)], name)


def _merge_bwd(dmerged, attn, g, z, b_glu, g_attn, g_ssm):
    S, W = attn.shape
    tr = _tile(S, 256)

    def rms_bwd(x, gamma, dy):
        r = lax.rsqrt(jnp.mean(x * x, axis=-1, keepdims=True) + LN_EPS)
        gdy = gamma * dy
        dx = gdy * r - x * (r * r * r) * jnp.mean(gdy * x, axis=-1, keepdims=True)
        return dx, dy * x * r

    def body(dm_ref, a_ref, g_ref, z_ref, b_ref, ga_ref, gs_ref, da_ref, dgp_ref, dz_ref, dga_ref, dgs_ref, db_ref):
        da, dga = rms_bwd(a_ref[...], ga_ref[...], dm_ref[:, 0:W])
        da_ref[...] = da
        _accumulate(dga_ref, dga)
        gv = g_ref[...]
        sig = 1.0 / (1.0 + jnp.exp(-(z_ref[...] + b_ref[...])))
        dssm, dgs = rms_bwd(gv * sig, gs_ref[...], dm_ref[:, W:2 * W])
        _accumulate(dgs_ref, dgs)
        dgp_ref[...] = dssm * sig
        dz = dssm * gv * sig * (1.0 - sig)
        dz_ref[...] = dz.astype(BF16)
        _accumulate(db_ref, dz)

    vec = jax.ShapeDtypeStruct((1, W), F32)
    mat = jax.ShapeDtypeStruct((S, W), F32)
    return _row_call(body, S, tr, [dmerged, attn, g, z, b_glu, g_attn, g_ssm],
                     [_rb(tr, 2 * W)] + [_rb(tr, W)] * 3 + [_pb(1, W)] * 3,
                     [mat, mat, jax.ShapeDtypeStruct((S, W), BF16), vec, vec, vec],
                     [_rb(tr, W)] * 3 + [_pb(1, W)] * 3, "merge_bwd")


def _act_bwd(dgp, dgl, pre, u, dskip):
    S, W = pre.shape
    tr = _tile(S, 256)

    def body(a_ref, b_ref, pre_ref, u_ref, d_ref, dy_ref, du_ref, dd_ref):
        dy = (a_ref[...] + b_ref[...]) * _gelu_grad(pre_ref[...])
        dy_ref[...] = dy
        du_ref[...] = dy * d_ref[...]
        _accumulate(dd_ref, dy * u_ref[...])

    mat = jax.ShapeDtypeStruct((S, W), F32)
    return _row_call(body, S, tr, [dgp, dgl, pre, u, dskip], [_rb(tr, W)] * 4 + [_pb(1, W)],
                     [mat, mat, jax.ShapeDtypeStruct((1, W), F32)], [_rb(tr, W)] * 2 + [_pb(1, W)], "act_bwd")


def _dproj(dq, dkT, dvT, du_skip, du0, du1, reach):
    S, W = dq.shape
    tr = _tile(S, 256)
    lanes = 2 * HEAD_DIM
    n_pairs, nkb = W // lanes, tr // lanes
    assert tr % lanes == 0 and reach % tr == 0

    def body(dq_ref, dk_ref, dv_ref, a_ref, b_ref, c_ref, o_ref):
        o_ref[0] = dq_ref[...].astype(BF16)
        for h in range(n_pairs):
            for b in range(nkb):
                rows, cols = slice(b * lanes, (b + 1) * lanes), slice(h * lanes, (h + 1) * lanes)
                o_ref[1, rows, cols] = dk_ref[h, b].T.astype(BF16)
                o_ref[2, rows, cols] = dv_ref[h, b].T.astype(BF16)
        o_ref[3] = (a_ref[...] + b_ref[...] + c_ref[...]).astype(BF16)

    tsp = pl.BlockSpec((n_pairs, nkb, lanes, lanes), lambda i: (0, i + reach // tr, 0, 0))
    return _row_call(body, S, tr, [dq, dkT, dvT, du_skip, du0, du1],
                     [_rb(tr, W), tsp, tsp, _rb(tr, W), _rb(tr, W), _rb(tr, W)],
                     jax.ShapeDtypeStruct((4, S, W), BF16), pl.BlockSpec((4, tr, W), lambda i: (0, i, 0)), "dproj")


def _attn_reach():
    return max(w // 2 for w, _ in ATTN_PATTERNS)


def _bias_table(tq, width, reach, head):
    i = lax.broadcasted_iota(jnp.int32, (tq, width), 0)
    j = lax.broadcasted_iota(jnp.int32, (tq, width), 1)
    ad = jnp.abs(j - i - reach)
    mult = jnp.zeros((tq, width), jnp.int32)
    for window, dil in ATTN_PATTERNS:
        assert dil & (dil - 1) == 0
        mult += ((jnp.bitwise_and(ad, dil - 1) == 0) & (ad <= window // 2)).astype(jnp.int32)
    logm = jnp.zeros((tq, width), F32)
    for n in range(2, len(ATTN_PATTERNS) + 1):
        logm = jnp.where(mult == n, math.log(n), logm)
    return logm, ad.astype(F32), mult > 0


def _bias_tables(n_heads, tq):
    reach = _attn_reach()
    width = tq + 2 * reach

    def body(o_ref):
        logm, ad, ok = _bias_table(tq, width, reach, None)
        head = (pl.program_id(0) + 1).astype(F32)
        slope = jnp.exp(jnp.full((tq, width), -8.0 * math.log(2.0) / n_heads, F32) * head)
        o_ref[...] = jnp.where(ok, logm - slope * ad, NEG_BIG)

    return pl.pallas_call(body, grid=(n_heads,), out_specs=pl.BlockSpec((None, tq, width), lambda h: (h, 0, 0)),
                          out_shape=jax.ShapeDtypeStruct((n_heads, tq, width), F32), compiler_params=_params(1),
                          name="attn_bias")()


def _attn_setup(tables_ref, bias_ref, sem, tq):
    hp, qi = pl.program_id(0), pl.program_id(1)

    @pl.when(qi == 0)
    def _():
        cp = pltpu.make_async_copy(tables_ref.at[pl.ds(2 * hp, 2)], bias_ref, sem)
        cp.start()
        cp.wait()

    lane = lax.broadcasted_iota(jnp.int32, (1, 2 * HEAD_DIM), 1)
    return pl.multiple_of(qi * tq, tq), [lane < HEAD_DIM, lane >= HEAD_DIM]


def _key_validity(first_key, n, seq):
    kpos = first_key + lax.broadcasted_iota(jnp.int32, (1, n), 1)
    return jnp.where((kpos >= 0) & (kpos < seq), 0.0, NEG_BIG).astype(F32)


_NT = (((1,), (1,)), ((), ()))
assert math.log2(HEAD_DIM) % 2 == 0


def _carrier_call(body, grid, in_specs, out_specs, out_shape, scratch, name, ins, riders):
    if riders:
        first = lambda: (pl.program_id(0) == 0) & (pl.program_id(1) == 0)
        last = lambda: (pl.program_id(0) == grid[0] - 1) & (pl.program_id(1) == grid[1] - 1)
        body = _hosted(body, len(in_specs), len(out_specs), len(scratch), riders, first, last)
        name += "_carrier"
    rider_ins = [a for t in riders for a in t.ins]
    res = pl.pallas_call(
        body, grid=grid, in_specs=in_specs + [ANY] * len(rider_ins), out_specs=out_specs + [ANY] * len(riders),
        out_shape=out_shape + [t.out_shape for t in riders],
        scratch_shapes=scratch + [s for t in riders for s in t.semaphores()],
        compiler_params=_params(2), name=name)(*ins, *rider_ins)
    return res[:len(out_specs)], res[len(out_specs):]


def _attention_fwd(qkv, kp, vp, tables, tq, riders=()):
    _, S, W = qkv.shape
    reach = _attn_reach()
    width = tq + 2 * reach
    Sp = S + 2 * reach
    scale = HEAD_DIM ** -0.5
    lanes = 2 * HEAD_DIM

    kc = tq
    assert width % kc == 0

    def body(q_ref, k_ref, v_ref, tables_ref, o_ref, lse_ref, bias_ref, sem):
        start, masks = _attn_setup(tables_ref, bias_ref, sem, tq)
        qi = pl.program_id(1)
        q = q_ref[...] * scale
        out = jnp.zeros((tq, lanes), F32)
        lse = jnp.zeros((tq, lanes), F32)
        for a in range(2):
            qa = jnp.where(masks[a], q, jnp.zeros_like(q))
            m = jnp.full((tq, 1), 3.0 * NEG_BIG, F32)
            l = jnp.zeros((tq, 1), F32)
            acc = jnp.zeros((tq, lanes), F32)
            for c in range(width // kc):
                kw = k_ref[pl.ds(start + c * kc, kc), :]
                vw = v_ref[pl.ds(start + c * kc, kc), :]
                s = (lax.dot_general(qa, kw, _NT, preferred_element_type=F32) + bias_ref[a, :, c * kc:(c + 1) * kc]
                     + _key_validity(qi * tq - reach + c * kc, kc, S))
                m_new = jnp.maximum(m, jnp.max(s, axis=-1, keepdims=True))
                alpha = jnp.exp(m - m_new)
                p = jnp.exp(s - m_new)
                l = alpha * l + jnp.sum(p, axis=-1, keepdims=True)
                acc = alpha * acc + jnp.dot(p.astype(BF16), vw, preferred_element_type=F32)
                m = m_new
            out = jnp.where(masks[a], acc / l, out)
            lse = jnp.where(masks[a], m + jnp.log(l), lse)
        o_ref[...] = out
        lse_ref[...] = lse

    qsp = pl.BlockSpec((None, tq, lanes), lambda h, i: (0, i, h))
    ksp = pl.BlockSpec((Sp, lanes), lambda h, i: (0, h))
    osp = pl.BlockSpec((tq, lanes), lambda h, i: (i, h))
    mat = jax.ShapeDtypeStruct((S, W), F32)
    return _carrier_call(body, (W // lanes, S // tq), [qsp, ksp, ksp, ANY], [osp, osp], [mat, mat],
                           [pltpu.VMEM((2, tq, width), F32), pltpu.SemaphoreType.DMA], "attn_fwd",
                           (qkv, kp, vp, tables), list(riders))


def _attention_bwd(qkv, kp, vp, out, lse, dout, tables, tq, riders=()):
    _, S, W = qkv.shape
    reach = _attn_reach()
    width = tq + 2 * reach
    Sp = S + 2 * reach
    scale = HEAD_DIM ** -0.5
    lanes = 2 * HEAD_DIM
    kc = tq
    assert width % kc == 0 and kc % lanes == 0

    def body(q_ref, k_ref, v_ref, tables_ref, o_ref, lse_ref, do_ref, dq_ref, dk_ref, dv_ref, bias_ref, sem):
        start, masks = _attn_setup(tables_ref, bias_ref, sem, tq)
        qi = pl.program_id(1)

        @pl.when(qi == 0)
        def _():
            dk_ref[...] = jnp.zeros_like(dk_ref)
            dv_ref[...] = jnp.zeros_like(dv_ref)

        q = q_ref[...] * scale
        do = do_ref[...]
        prod = do * o_ref[...]
        lse_all = lse_ref[...]
        dq = jnp.zeros((tq, lanes), F32)
        for a in range(2):
            qa = jnp.where(masks[a], q, jnp.zeros_like(q))
            doa = jnp.where(masks[a], do, 0.0).astype(BF16)
            qat = qa.astype(F32).T.astype(BF16)
            doat = jnp.where(masks[a], do, 0.0).T.astype(BF16)
            lse_a = lse_all[:, a * HEAD_DIM:a * HEAD_DIM + 1]
            delta = jnp.sum(jnp.where(masks[a], prod, 0.0), axis=-1, keepdims=True)
            dq_a = jnp.zeros((tq, lanes), F32)
            for c in range(width // kc):
                kw = k_ref[pl.ds(start + c * kc, kc), :]
                vw = v_ref[pl.ds(start + c * kc, kc), :]
                s = (lax.dot_general(qa, kw, _NT, preferred_element_type=F32) + bias_ref[a, :, c * kc:(c + 1) * kc]
                     + _key_validity(qi * tq - reach + c * kc, kc, S))
                p = jnp.exp(s - lse_a)
                dp = lax.dot_general(doa, vw, _NT, preferred_element_type=F32)
                ds = (p * (dp - delta)).astype(BF16)
                dvt = jnp.dot(doat, p.astype(BF16), preferred_element_type=F32)
                dkt = jnp.dot(qat, ds, preferred_element_type=F32)
                dq_a += jnp.dot(ds, kw, preferred_element_type=F32)
                first = qi * (tq // lanes) + c * (kc // lanes)
                for j in range(kc // lanes):
                    dk_ref[first + j] += dkt[:, j * lanes:(j + 1) * lanes]
                    dv_ref[first + j] += dvt[:, j * lanes:(j + 1) * lanes]
            dq = jnp.where(masks[a], dq_a * scale, dq)
        dq_ref[...] = dq

    assert tq % lanes == 0 and reach % lanes == 0
    qsp = pl.BlockSpec((None, tq, lanes), lambda h, i: (0, i, h))
    ksp = pl.BlockSpec((Sp, lanes), lambda h, i: (0, h))
    osp = pl.BlockSpec((tq, lanes), lambda h, i: (i, h))
    tsp = pl.BlockSpec((None, Sp // lanes, lanes, lanes), lambda h, i: (h, 0, 0, 0))
    tiles = jax.ShapeDtypeStruct((W // lanes, Sp // lanes, lanes, lanes), F32)
    return _carrier_call(body, (W // lanes, S // tq), [qsp, ksp, ksp, ANY, osp, osp, osp], [osp, tsp, tsp],
                           [jax.ShapeDtypeStruct((S, W), F32), tiles, tiles],
                           [pltpu.VMEM((2, tq, width), F32), pltpu.SemaphoreType.DMA], "attn_bwd",
                           (qkv, kp, vp, tables, out, lse, dout), list(riders))


SB = GROUPS_PER_BLOCK * SSM_STATE
CB = GROUPS_PER_BLOCK * SSM_GROUP
RUNS = 8


def _powers(pw, a_r, a_i, n):
    p_r, p_i = a_r, a_i
    for j in range(n):
        pw[0, j:j + 1, :] = p_r
        pw[1, j:j + 1, :] = p_i
        p_r, p_i = p_r * a_r - p_i * a_i, p_r * a_i + p_i * a_r


def _rows8(j):
    return slice(j * RUNS, (j + 1) * RUNS)


def _run_scan(br, bi, T, a_r, a_i, pw, e_r, e_i, ent, down, conj):
    n = T // RUNS
    sg = -1.0 if conj else 1.0
    A_r = jnp.broadcast_to(a_r, (RUNS, SB))
    A_i = jnp.broadcast_to(sg * a_i, (RUNS, SB))
    x_r = x_i = None
    for j in (range(n - 1, -1, -1) if down else range(n)):
        b_r, b_i = br[_rows8(j), :], bi[_rows8(j), :]
        if x_r is None:
            x_r, x_i = b_r, b_i
        else:
            x_r, x_i = A_r * x_r - A_i * x_i + b_r, A_r * x_i + A_i * x_r + b_i
            br[_rows8(j), :] = x_r
            bi[_rows8(j), :] = x_i
    al_r, al_i = pw[0, n - 1:n, :], sg * pw[1, n - 1:n, :]
    s_r, s_i = e_r, e_i
    for c in (range(RUNS - 1, -1, -1) if down else range(RUNS)):
        ent[0, c:c + 1, :] = s_r
        ent[1, c:c + 1, :] = s_i
        s_r, s_i = (al_r * s_r - al_i * s_i + x_r[c:c + 1, :], al_r * s_i + al_i * s_r + x_i[c:c + 1, :])
    E_r, E_i = ent[0], ent[1]
    for j in range(n):
        p = n - 1 - j if down else j
        p_r, p_i = pw[0, p:p + 1, :], sg * pw[1, p:p + 1, :]
        v_r, v_i = br[_rows8(j), :], bi[_rows8(j), :]
        br[_rows8(j), :] = v_r + p_r * E_r - p_i * E_i
        bi[_rows8(j), :] = v_i + p_r * E_i + p_i * E_r
    return s_r, s_i


def _load_run_major(ref, T):
    n = T // RUNS
    return jnp.concatenate([ref[pl.ds(j, RUNS, stride=n), :] for j in range(n)], axis=0)


def _store_row_major(ref, value, T):
    n = T // RUNS
    for j in range(n):
        ref[pl.ds(j, RUNS, stride=n), :] = value[_rows8(j), :]


def _s5_specs(nb, T, rev, adjoint):
    flip = rev != adjoint
    blk = (lambda i: nb - 1 - i) if flip else (lambda i: i)
    usp = pl.BlockSpec((T, CB), lambda k, i: (blk(i), k))
    asp = pl.BlockSpec((None, 2, SB), lambda k, i: (k, 0, 0))
    wsp = pl.BlockSpec((None, CB, SB), lambda k, i: (k, 0, 0))
    csp = pl.BlockSpec((None, None, 2, SB), lambda k, i: (k, blk(i), 0, 0))
    return usp, asp, wsp, csp


def _s5_fwd(u, a, b_re, b_im, c_re, c_im, T, rev):
    S, W = u.shape
    NK = W // CB
    nb = S // T
    usp, asp, wsp, csp = _s5_specs(nb, T, rev, False)

    def body(u_ref, a_ref, bre_ref, bim_ref, cre_ref, cim_ref, y_ref, car_ref, wr, wi, st, pw, ent):
        a_r, a_i = a_ref[0:1, :], a_ref[1:2, :]

        @pl.when(pl.program_id(1) == 0)
        def _():
            st[...] = jnp.zeros_like(st)
            _powers(pw, a_r, a_i, T // RUNS)

        car_ref[...] = st[0:2, :]
        ub = _load_run_major(u_ref, T).astype(BF16)
        wr[...] = jnp.dot(ub, bre_ref[...], preferred_element_type=F32)
        wi[...] = jnp.dot(ub, bim_ref[...], preferred_element_type=F32)
        s_r, s_i = _run_scan(wr, wi, T, a_r, a_i, pw, st[0:1, :], st[1:2, :], ent, rev, False)
        st[0:1, :] = s_r
        st[1:2, :] = s_i
        y = (lax.dot_general(wr[...].astype(BF16), cre_ref[...], _NT, preferred_element_type=F32)
             - lax.dot_general(wi[...].astype(BF16), cim_ref[...], _NT, preferred_element_type=F32))
        _store_row_major(y_ref, y, T)

    return pl.pallas_call(
        body, grid=(NK, nb), in_specs=[usp, asp, wsp, wsp, wsp, wsp], out_specs=[usp, csp],
        out_shape=[jax.ShapeDtypeStruct((S, W), F32), jax.ShapeDtypeStruct((NK, nb, 2, SB), F32)],
        scratch_shapes=[pltpu.VMEM((T, SB), F32), pltpu.VMEM((T, SB), F32), pltpu.VMEM((8, SB), F32),
                        pltpu.VMEM((2, T // RUNS, SB), F32), pltpu.VMEM((2, RUNS, SB), F32)],
        compiler_params=_params(2), name="s5_fwd_rev" if rev else "s5_fwd")(u, a, b_re, b_im, c_re, c_im)


def _s5_bwd(u, dy, a, b_re, b_im, c_re, c_im, carries, T, rev, riders=()):
    S, W = u.shape
    NK = W // CB
    nb = S // T
    usp, asp, wsp, csp = _s5_specs(nb, T, rev, True)
    n = T // RUNS

    def body(u_ref, dy_ref, a_ref, bre_ref, bim_ref, cre_ref, cim_ref, car_ref,
             du_ref, dbre_ref, dbim_ref, dcre_ref, dcim_ref, da_ref, wr, wi, gr, gi, lam, pw, ent):
        a_r, a_i = a_ref[0:1, :], a_ref[1:2, :]

        @pl.when(pl.program_id(1) == 0)
        def _():
            lam[...] = jnp.zeros_like(lam)
            for r in (dbre_ref, dbim_ref, dcre_ref, dcim_ref, da_ref):
                r[...] = jnp.zeros_like(r)
            _powers(pw, a_r, a_i, n)

        u32 = _load_run_major(u_ref, T)
        ub = u32.astype(BF16)
        dyk = _load_run_major(dy_ref, T)
        dyb = dyk.astype(BF16)
        wr[...] = jnp.dot(ub, bre_ref[...], preferred_element_type=F32)
        wi[...] = jnp.dot(ub, bim_ref[...], preferred_element_type=F32)
        x0r, x0i = car_ref[0:1, :], car_ref[1:2, :]
        _run_scan(wr, wi, T, a_r, a_i, pw, x0r, x0i, ent, rev, False)

        gr[...] = jnp.dot(dyb, cre_ref[...], preferred_element_type=F32)
        gi[...] = -jnp.dot(dyb, cim_ref[...], preferred_element_type=F32)
        l_r, l_i = _run_scan(gr, gi, T, a_r, a_i, pw, lam[0:1, :], lam[1:2, :], ent, not rev, True)
        lam[0:1, :] = l_r
        lam[1:2, :] = l_i

        sub = lax.broadcasted_iota(jnp.int32, (RUNS, SB), 0)

        def before(buf, x0, j):
            if rev:
                if j < n - 1:
                    return buf[_rows8(j + 1), :]
                return jnp.where(sub == RUNS - 1, x0, pltpu.roll(buf[_rows8(0), :], RUNS - 1, 0))
            if j > 0:
                return buf[_rows8(j - 1), :]
            return jnp.where(sub == 0, x0, pltpu.roll(buf[_rows8(n - 1), :], 1, 0))

        acc_r = jnp.zeros((RUNS, SB), F32)
        acc_i = jnp.zeros((RUNS, SB), F32)
        for j in range(n):
            g_r, g_i = gr[_rows8(j), :], gi[_rows8(j), :]
            p_r, p_i = before(wr, x0r, j), before(wi, x0i, j)
            acc_r += g_r * p_r + g_i * p_i
            acc_i += g_i * p_r - g_r * p_i
        da_ref[0:1, :] += jnp.sum(acc_r, axis=0, keepdims=True)
        da_ref[1:2, :] += jnp.sum(acc_i, axis=0, keepdims=True)

        lrb, lib = gr[...].astype(BF16), gi[...].astype(BF16)
        du = (lax.dot_general(lrb, bre_ref[...], _NT, preferred_element_type=F32)
              + lax.dot_general(lib, bim_ref[...], _NT, preferred_element_type=F32))
        _store_row_major(du_ref, du, T)
        ut = u32.T.astype(BF16)
        dbre_ref[...] += jnp.dot(ut, lrb, preferred_element_type=F32)
        dbim_ref[...] += jnp.dot(ut, lib, preferred_element_type=F32)
        dyt = dyk.T.astype(BF16)
        dcre_ref[...] += jnp.dot(dyt, wr[...].astype(BF16), preferred_element_type=F32)
        dcim_ref[...] += jnp.dot(dyt, wi[...].astype(BF16), preferred_element_type=F32)

    blk = jax.ShapeDtypeStruct((NK, CB, SB), F32)
    return _carrier_call(
        body, (NK, nb), [usp, usp, asp, wsp, wsp, wsp, wsp, csp], [usp, wsp, wsp, wsp, wsp, asp],
        [jax.ShapeDtypeStruct((S, W), F32), blk, blk, blk, blk, jax.ShapeDtypeStruct((NK, 2, SB), F32)],
        [pltpu.VMEM((T, SB), F32), pltpu.VMEM((T, SB), F32), pltpu.VMEM((T, SB), F32), pltpu.VMEM((T, SB), F32),
         pltpu.VMEM((8, SB), F32), pltpu.VMEM((2, n, SB), F32), pltpu.VMEM((2, RUNS, SB), F32)],
        "s5_bwd_rev" if rev else "s5_bwd", (u, dy, a, b_re, b_im, c_re, c_im, carries), list(riders))


def _s5_discretize(lam_re, lam_im, log_step, b_re, b_im):
    step = jnp.exp(log_step)[..., None]
    mag = jnp.exp(lam_re * step)
    a_re, a_im = mag * jnp.cos(lam_im * step), mag * jnp.sin(lam_im * step)
    den = lam_re * lam_re + lam_im * lam_im
    coef_re = ((a_re - 1.0) * lam_re + a_im * lam_im) / den
    coef_im = (a_im * lam_re - (a_re - 1.0) * lam_im) / den
    bb_re = coef_re[..., None] * b_re - coef_im[..., None] * b_im
    bb_im = coef_re[..., None] * b_im + coef_im[..., None] * b_re
    return a_re, a_im, bb_re, bb_im


def _to_blocks(w_gcp):
    two, G, C, P = w_gcp.shape
    nk = G // GROUPS_PER_BLOCK
    x = w_gcp.reshape(two, nk, GROUPS_PER_BLOCK, C, P)
    eye = jnp.eye(GROUPS_PER_BLOCK, dtype=w_gcp.dtype)
    return jnp.einsum('dkgcp,gh->dkgchp', x, eye).reshape(two, nk, GROUPS_PER_BLOCK * C, GROUPS_PER_BLOCK * P)


def _from_blocks(blk):
    two, nk, cb, sb = blk.shape
    C, P = cb // GROUPS_PER_BLOCK, sb // GROUPS_PER_BLOCK
    x = blk.reshape(two, nk, GROUPS_PER_BLOCK, C, GROUPS_PER_BLOCK, P)
    eye = jnp.eye(GROUPS_PER_BLOCK, dtype=blk.dtype)
    return jnp.einsum('dkgchp,gh->dkgcp', x, eye).reshape(two, nk * GROUPS_PER_BLOCK, C, P)


def _me():
    return lax.axis_index("x"), lax.axis_index("y"), lax.axis_index("c")


def _peer(k):
    x, y, c = _me()
    return (1 - x if k & 4 else x, 1 - y if k & 2 else y, 1 - c if k & 1 else c)


def _logical(dev):
    return 4 * dev[0] + 2 * dev[1] + dev[2]


class _Transfer:
    def __init__(self, ks, src_of, dst_of, out_shape, ins, split=1):
        self.ks, self.src_of, self.dst_of, self.out_shape, self.ins, self.split = ks, src_of, dst_of, out_shape, ins, split

    def semaphores(self):
        n = len(self.ks) * self.split
        return [pltpu.SemaphoreType.DMA((n,)), pltpu.SemaphoreType.DMA((n,))]

    def copies(self, in_refs, out_ref, send, recv):
        out = []
        for k in self.ks:
            src, dst = self.src_of(k, in_refs, out_ref), self.dst_of(k, in_refs, out_ref)
            rows = src.shape[0] // self.split
            for q in range(self.split):
                part = pl.ds(q * rows, rows)
                j = len(out)
                out.append(pltpu.make_async_remote_copy(
                    src_ref=src if self.split == 1 else src.at[part],
                    dst_ref=dst if self.split == 1 else dst.at[part],
                    send_sem=send.at[j], recv_sem=recv.at[j], device_id=_peer(k), device_id_type=MESH))
        return out


def _exchange(name, t):
    n_in = len(t.ins)

    def body(*refs):
        copies = t.copies(refs[:n_in], refs[n_in], *refs[n_in + 1:])
        for cp in copies:
            cp.start()
        for cp in copies:
            cp.wait_recv()
        for cp in copies:
            cp.wait_send()

    return pl.pallas_call(body, in_specs=[ANY] * n_in, out_specs=ANY, out_shape=t.out_shape,
                          scratch_shapes=t.semaphores(), name=name)(*t.ins)


def _hosted(body, n_in, n_out, n_scratch, riders, is_first, is_last):
    n_rin = sum(len(t.ins) for t in riders)

    def wrapped(*refs):
        host_in, rider_in = refs[:n_in], refs[n_in:n_in + n_rin]
        pos = n_in + n_rin
        host_out, rider_out = refs[pos:pos + n_out], refs[pos + n_out:pos + n_out + len(riders)]
        pos += n_out + len(riders)
        host_scratch, sems = refs[pos:pos + n_scratch], refs[pos + n_scratch:]

        def copies():
            out, at = [], 0
            for i, t in enumerate(riders):
                out += t.copies(rider_in[at:at + len(t.ins)], rider_out[i], sems[2 * i], sems[2 * i + 1])
                at += len(t.ins)
            return out

        @pl.when(is_first())
        def _():
            for cp in copies():
                cp.start()

        body(*host_in, *host_out, *host_scratch)

        @pl.when(is_last())
        def _():
            for cp in copies():
                cp.wait_recv()
            for cp in copies():
                cp.wait_send()

    return wrapped


def _gather8_transfer(v):
    out = jax.ShapeDtypeStruct((N_DEV,) + v.shape, v.dtype)
    slot = lambda k, ins, o: o.at[_logical(_me())]
    return _Transfer(list(range(1, 8)), lambda k, ins, o: ins[0], slot, out, [v])


def _all_gather8(v, name):
    got = _exchange(name, _gather8_transfer(v))
    return lax.dynamic_update_index_in_dim(got, v, _logical(_me()), 0)


def _halves_transfer(v):
    R, C = v.shape
    half = lambda ins: ins[0].at[pl.ds(lax.axis_index("c") * (R // 2), R // 2)]
    chip = lambda: 2 * lax.axis_index("x") + lax.axis_index("y")
    return _Transfer([2, 4, 6], lambda k, ins, o: half(ins), lambda k, ins, o: o.at[chip()],
                     jax.ShapeDtypeStruct((4, R // 2, C), v.dtype), [v])


def _join_halves(v, halves, name):
    R, C = v.shape
    c = lax.axis_index("c")
    mine = lax.dynamic_slice_in_dim(v, c * (R // 2), R // 2, axis=0)
    halves = lax.dynamic_update_index_in_dim(halves, mine, 2 * lax.axis_index("x") + lax.axis_index("y"), 0)
    core = lambda o: o.at[:, lax.axis_index("c")]
    both = _exchange(name + "_join", _Transfer(
        [1], lambda k, ins, o: ins[0], lambda k, ins, o: core(o),
        jax.ShapeDtypeStruct((4, 2, R // 2, C), v.dtype), [halves], split=4))
    return lax.dynamic_update_index_in_dim(both, halves, c, 1).reshape(4, R, C)


def _scatter_transfer(g):
    four, two, R2, C = g.shape
    piece = lambda dev, ins: ins[0].at[2 * dev[0] + dev[1], dev[2]]
    slot = lambda k, ins, o: o.at[_logical(_me())]
    return _Transfer(list(range(1, 8)), lambda k, ins, o: piece(_peer(k), ins), slot,
                     jax.ShapeDtypeStruct((N_DEV, R2, C), g.dtype), [g])


def _reduce_scattered(g, landed, name):
    x, y, c = _me()
    mine = lax.dynamic_index_in_dim(lax.dynamic_index_in_dim(g, 2 * x + y, 0, False), c, 0, False)
    landed = lax.dynamic_update_index_in_dim(landed, mine, _logical(_me()), 0)
    part = _sum_leading(landed, "reduce_" + name)
    return _swap_c(part, "swap_" + name).reshape(2 * part.shape[0], part.shape[1])


def _swap_c(v, name):
    out = jax.ShapeDtypeStruct((2,) + v.shape, v.dtype)
    slot = lambda: lax.axis_index("c")
    got = _exchange(name, _Transfer([1], lambda k, ins, o: ins[0], lambda k, ins, o: o.at[slot()], out, [v],
                                    split=8))
    return lax.dynamic_update_index_in_dim(got, v, slot(), 0)


def _sum_leading(v, name):
    n, R, C = v.shape
    tr = _tile(R, max(8, (1 << 19) // (C * n)))

    def body(v_ref, o_ref):
        acc = v_ref[0].astype(F32)
        for s in range(1, n):
            acc = acc + v_ref[s].astype(F32)
        o_ref[...] = acc

    return pl.pallas_call(body, grid=(R // tr,), in_specs=[pl.BlockSpec((n, tr, C), lambda i: (0, i, 0))],
                          out_specs=pl.BlockSpec((tr, C), lambda i: (i, 0)),
                          out_shape=jax.ShapeDtypeStruct((R, C), F32), compiler_params=_params(1), name=name)(v)


def _adamw(w, g, m, v, name):
    R, C = w.shape
    tr = _tile(R, max(8, (1 << 18) // C))
    c1 = 1.0 - ADAM_B1 ** ADAM_STEP
    c2 = 1.0 - ADAM_B2 ** ADAM_STEP

    def body(w_ref, g_ref, m_ref, v_ref, d_ref, nm_ref, nv_ref):
        gv = g_ref[...]
        nm = ADAM_B1 * m_ref[...] + (1.0 - ADAM_B1) * gv
        nv = ADAM_B2 * v_ref[...] + (1.0 - ADAM_B2) * (gv * gv)
        nm_ref[...] = nm
        nv_ref[...] = nv
        d_ref[...] = -ADAM_LR * ((nm / c1) / (jnp.sqrt(nv / c2) + ADAM_EPS) + ADAM_WD * w_ref[...])

    sp = pl.BlockSpec((tr, C), lambda i: (i, 0))
    return pl.pallas_call(body, grid=(R // tr,), in_specs=[sp] * 4, out_specs=[sp] * 3,
                          out_shape=[jax.ShapeDtypeStruct((R, C), F32)] * 3, compiler_params=_params(1),
                          name=name)(w, g, m, v)


def _forward_layer(x, lw, w_in_full, shards, next_w_in, alpha, tables, tq, ts):
    S, D = x.shape
    W = D // 2
    reach = _attn_reach()
    m6 = lw['m6']
    gw = {'w_in': w_in_full}
    FF = shards['w_mlp1'].shape[1] * 4

    h = _modulate(x, m6, 0, 1, "modulate1")
    qkv = _mm(h, gw['w_in'], M=S, N=3 * W, K=D, b_st='n', o_st='n', ns=3, out_dtype=BF16, cap_k=2048, name="proj_qkv")
    u = _mm(h, gw['w_in'][3], M=S, N=W, K=D, out_dtype=F32, cap_k=2048, name="proj_u")
    pad = ((reach, reach), (0, 0))
    kp, vp = jnp.pad(qkv[1], pad), jnp.pad(qkv[2], pad)
    carried = [(n, shards[n]) for n in BIG[1:]] + ([('w_in', next_w_in)] if next_w_in is not None else [])
    (attn, lse), halves = _attention_fwd(qkv, kp, vp, tables, tq, [_halves_transfer(v) for _, v in carried])
    joined = [_join_halves(v, hv, "gather_" + n) for (n, v), hv in zip(carried, halves)]
    gw.update({n: j for (n, _), j in zip(carried[:len(BIG) - 1], joined)})
    w_in_next = joined[-1] if next_w_in is not None else None
    y, carries = zip(*[_s5_fwd(u, lw['a'][d], lw['b_re'][d], lw['b_im'][d], lw['c_re'][d], lw['c_im'][d], ts,
                               rev=bool(d)) for d in range(2)])
    pre, g = _ssm_act(u, y[0], y[1], lw['ssm_d'])
    z = _mm(g, gw['w_glu'].reshape(W, W), M=S, N=W, K=W, out_dtype=F32, name="glu")
    merged = _merge(attn, g, z, lw['b_glu'], lw['g_attn'], lw['g_ssm'])
    mix = _mm(merged, gw['w_out'].reshape(D, D), M=S, N=D, K=D, out_dtype=F32, cap_k=2048, name="out_proj")
    x1 = _post_ln(x, mix, m6, 2, lw['ln1_g'], lw['ln1_b'], alpha, "post_ln1")
    h2 = _modulate(x1, m6, 3, 4, "modulate2")
    a4 = _mm(h2, gw['w_mlp1'], M=S, N=FF, K=D, b_st='n', o_st='n', ns=4, out_dtype=F32, cap_k=2048, name="mlp1")
    ff = _mm(a4, gw['w_mlp2'], M=S, N=D, K=FF, a_st='k', b_st='k', ns=4, a_tf=_relu2, out_dtype=F32, name="mlp2")
    x2 = _post_ln(x1, ff, m6, 5, lw['ln2_g'], lw['ln2_b'], alpha, "post_ln2")
    saved = dict(x=x, h=h, qkv=qkv, u=u, attn=attn, lse=lse, carries=carries, pre=pre, g=g, z=z, merged=merged,
                 mix=mix, x1=x1, h2=h2, a4=a4, ff=ff, gw=gw)
    return x2, saved, w_in_next


SMALL_GRADS = ['a', 'bb_re', 'bb_im', 'cc_re', 'cc_im', 'ssm_d', 'b_glu', 'g_attn', 'g_ssm',
               'ln1_g', 'ln1_b', 'ln2_g', 'ln2_b']


def _as_halves(g):
    C = g.shape[-1]
    return g.reshape(4, 2, g.size // (8 * C), C)


def _backward_layer(dx2, lw, sv, pending_w_in, alpha, tables, tq, ts):
    S, D = dx2.shape
    W = D // 2
    reach = _attn_reach()
    m6, gw = lw['m6'], sv['gw']
    FF = gw['w_mlp1'].shape[2] * 4
    gr = {}

    ds2, dff, gr['ln2_g'], gr['ln2_b'], d_g2 = _ln_bwd(dx2, sv['x1'], sv['ff'], m6, 5, lw['ln2_g'], alpha, "ln2_bwd")
    da4 = _mm(dff, gw['w_mlp2'], M=S, N=FF, K=D, tb=True, b_st='n', o_st='n', ns=4, extra=sv['a4'],
              epi=lambda acc, a: 2.0 * jnp.maximum(a, 0.0) * acc, out_dtype=BF16, cap_k=2048, name="mlp2_bwd_x")
    gr['w_mlp2'] = _mm(sv['a4'], dff, M=FF, N=D, K=S, ta=True, a_st='m', o_st='m', ns=4, a_tf=_relu2,
                       out_dtype=BF16, name="mlp2_bwd_w")
    dh2 = _mm(da4, gw['w_mlp1'], M=S, N=D, K=FF, tb=True, a_st='k', b_st='k', ns=4, out_dtype=F32, name="mlp1_bwd_x")
    gr['w_mlp1'] = _mm(sv['h2'], da4, M=D, N=FF, K=S, ta=True, b_st='n', o_st='n', ns=4, out_dtype=BF16,
                       name="mlp1_bwd_w")
    dx1, d_sc2, d_sh2 = _mod_bwd(ds2, dh2, sv['x1'], m6, 4, alpha, "mod2_bwd")

    ds1, dmix, gr['ln1_g'], gr['ln1_b'], d_g1 = _ln_bwd(dx1, sv['x'], sv['mix'], m6, 2, lw['ln1_g'], alpha, "ln1_bwd")
    dmerged = _mm(dmix, gw['w_out'].reshape(D, D), M=S, N=D, K=D, tb=True, out_dtype=F32, cap_k=2048, name="out_proj_bwd_x")
    gr['w_out'] = _mm(sv['merged'], dmix, M=D, N=D, K=S, ta=True, out_dtype=BF16, name="out_proj_bwd_w")
    dattn, dgp, dz, gr['g_attn'], gr['g_ssm'], gr['b_glu'] = _merge_bwd(
        dmerged, sv['attn'], sv['g'], sv['z'], lw['b_glu'], lw['g_attn'], lw['g_ssm'])
    dgl = _mm(dz, gw['w_glu'].reshape(W, W), M=S, N=W, K=W, tb=True, out_dtype=F32, name="glu_bwd_x")
    gr['w_glu'] = _mm(sv['g'], dz, M=W, N=W, K=S, ta=True, out_dtype=BF16, name="glu_bwd_w")
    dy, du_skip, gr['ssm_d'] = _act_bwd(dgp, dgl, sv['pre'], sv['u'], lw['ssm_d'])
    mlp = [_as_halves(gr['w_mlp2']), _as_halves(gr['w_mlp1'])]
    s5 = [_s5_bwd(sv['u'], dy, lw['a'][d], lw['b_re'][d], lw['b_im'][d], lw['c_re'][d], lw['c_im'][d],
                  sv['carries'][d], ts, rev=bool(d), riders=[_scatter_transfer(mlp[d])]) for d in range(2)]
    gr['w_mlp2'] = _reduce_scattered(mlp[0], s5[0][1][0], 'w_mlp2')
    gr['w_mlp1'] = _reduce_scattered(mlp[1], s5[1][1][0], 'w_mlp1')
    du = [s5[d][0][0] for d in range(2)]
    for pos, n in enumerate(['bb_re', 'bb_im', 'cc_re', 'cc_im', 'a']):
        gr[n] = jnp.stack([s5[d][0][pos + 1] for d in range(2)])
    pad = ((reach, reach), (0, 0))
    kp, vp = jnp.pad(sv['qkv'][1], pad), jnp.pad(sv['qkv'][2], pad)
    for n in ['bb_re', 'bb_im', 'cc_re', 'cc_im']:
        gr[n] = _from_blocks(gr[n])
    small, _ = _pack([gr[n] for n in SMALL_GRADS], 1024)
    scattered = [(n, _as_halves(gr[n])) for n in ('w_glu', 'w_out')]
    if pending_w_in is not None:
        scattered.append(('w_in', pending_w_in))
    riders = [_scatter_transfer(g) for _, g in scattered] + [_gather8_transfer(small)]
    (dq, dkT, dvT), landed = _attention_bwd(sv['qkv'], kp, vp, sv['attn'], sv['lse'], dattn, tables, tq, riders)
    reduced = [_reduce_scattered(g, lv, n) for (n, g), lv in zip(scattered, landed)]
    gr.update({n: r for (n, _), r in zip(scattered[:2], reduced)})
    w_in_above = reduced[-1] if pending_w_in is not None else None
    own = lax.dynamic_update_index_in_dim(landed[-1], small, _logical(_me()), 0)
    summed = _unpack(_sum_leading(own, "reduce_small"), [gr[n] for n in SMALL_GRADS])
    gr.update(dict(zip(SMALL_GRADS, summed)))
    dproj = _dproj(dq, dkT, dvT, du_skip, du[0], du[1], reach)
    dh = _mm(dproj, gw['w_in'], M=S, N=D, K=4 * W, tb=True, a_st='k', b_st='k', ns=4, out_dtype=F32, cap_n=2048, name="proj_bwd_x")
    gr['w_in'] = _mm(sv['h'], dproj, M=D, N=4 * W, K=S, ta=True, b_st='n', o_st='n', ns=4, out_dtype=BF16,
                     name="proj_bwd_w")
    dx, d_sc1, d_sh1 = _mod_bwd(ds1, dh, sv['x'], m6, 1, alpha, "mod1_bwd")
    gr['m6'] = jnp.concatenate([d_sh1, d_sc1, d_g1, d_sh2, d_sc2, d_g2], axis=0)
    return dx, gr, w_in_above


def _pack(arrays, width):
    flat = jnp.concatenate([a.reshape(-1) for a in arrays])
    n = flat.shape[0]
    rows = -(-n // (8 * width)) * 8
    return jnp.pad(flat, (0, rows * width - n)).reshape(rows, width), n


def _unpack(packed, like):
    flat = packed.reshape(-1)
    out, pos = [], 0
    for a in like:
        out.append(flat[pos:pos + a.size].reshape(a.shape))
        pos += a.size
    return out


def kernel(x, c, w_ada, b_ada, w_in, ssm_lam_re, ssm_lam_im, ssm_log_step, ssm_b_re, ssm_b_im, ssm_c_re, ssm_c_im, ssm_d, w_glu, b_glu, g_attn, g_ssm, w_out, ln1_g, ln1_b, w_mlp1, w_mlp2, ln2_g, ln2_b, loss_target, m_w_ada, m_b_ada, m_w_in, m_ssm_lam_re, m_ssm_lam_im, m_ssm_log_step, m_ssm_b_re, m_ssm_b_im, m_ssm_c_re, m_ssm_c_im, m_ssm_d, m_w_glu, m_b_glu, m_g_attn, m_g_ssm, m_w_out, m_ln1_g, m_ln1_b, m_w_mlp1, m_w_mlp2, m_ln2_g, m_ln2_b, v_w_ada, v_b_ada, v_w_in, v_ssm_lam_re, v_ssm_lam_im, v_ssm_log_step, v_ssm_b_re, v_ssm_b_im, v_ssm_c_re, v_ssm_c_im, v_ssm_d, v_w_glu, v_b_glu, v_g_attn, v_g_ssm, v_w_out, v_ln1_g, v_ln1_b, v_w_mlp1, v_w_mlp2, v_ln2_g, v_ln2_b):
    args = locals()
    w = {n: args[n] for n in WEIGHTS}
    mom = {n: args['m_' + n] for n in WEIGHTS}
    var = {n: args['v_' + n] for n in WEIGHTS}
    L, D, ada_cols = w_ada.shape
    S = x.shape[1]
    W = D // 2
    n_heads = W // HEAD_DIM
    alpha = (2 * L) ** 0.25
    tq = _tile(S, 512)
    ts = _tile(S, 512)
    tables = _bias_tables(n_heads, tq)
    xi, yi, ci = _me()
    shard = 2 * xi + yi
    me = 4 * xi + 2 * yi + ci

    c_all = _all_gather8(c, "gather_c").reshape(N_DEV, D)
    c16 = jnp.pad(c_all, ((0, 16 - N_DEV), (0, 0)))
    b_sh = lax.dynamic_slice_in_dim(b_ada, shard * ada_cols, ada_cols, axis=1)
    mods_sh = jnp.stack([_mm(c16, w_ada[l], M=16, N=ada_cols, K=D, a_tf=_silu, bias=b_sh[l:l + 1], out_dtype=F32,
                             name="ada") for l in range(L)])
    mods_all = _all_gather8(mods_sh, "gather_mods")
    mods = jnp.concatenate([mods_all[2 * j] for j in range(4)], axis=-1)
    m6_all = lax.dynamic_index_in_dim(mods, me, axis=1, keepdims=False).reshape(L, 6, D)

    (a_re, a_im, bb_re, bb_im), disc_vjp = jax.vjp(
        _s5_discretize, ssm_lam_re, ssm_lam_im, ssm_log_step, ssm_b_re, ssm_b_im)
    N = a_re.shape[2] * a_re.shape[3]
    per_layer = dict(
        m6=m6_all,
        a=jnp.stack([a_re.reshape(L, 2, N // SB, SB), a_im.reshape(L, 2, N // SB, SB)], axis=3),
        b_re=jax.vmap(_to_blocks)(jnp.swapaxes(bb_re, -1, -2)).astype(BF16),
        b_im=jax.vmap(_to_blocks)(jnp.swapaxes(bb_im, -1, -2)).astype(BF16),
        c_re=jax.vmap(_to_blocks)(ssm_c_re).astype(BF16),
        c_im=jax.vmap(_to_blocks)(ssm_c_im).astype(BF16),
        ssm_d=ssm_d[:, None], b_glu=b_glu[:, None], g_attn=g_attn[:, None], g_ssm=g_ssm[:, None],
        ln1_g=ln1_g[:, None], ln1_b=ln1_b[:, None], ln2_g=ln2_g[:, None], ln2_b=ln2_b[:, None],
        w_in=w_in, w_glu=w_glu, w_out=w_out, w_mlp1=w_mlp1, w_mlp2=w_mlp2)

    layers = [{n: v[l] for n, v in per_layer.items()} for l in range(L)]
    shards = [{n: layers[l][n].astype(BF16) for n in BIG} for l in range(L)]
    first = shards[0]['w_in']
    w_in_full = _join_halves(first, _exchange("gather_w_in", _halves_transfer(first)), "gather_w_in")
    xc, saved = x[0], []
    for l in range(L):
        xc, sv, w_in_full = _forward_layer(xc, layers[l], w_in_full, shards[l],
                                           shards[l + 1]['w_in'] if l + 1 < L else None, alpha, tables, tq, ts)
        saved.append(sv)
    dx, sq = _loss_head(xc, loss_target[0])
    loss = lax.psum(0.5 * jnp.sum(sq) / D, ("x", "y", "c"))

    per_layer_grads = [None] * L
    pending = None
    for l in reversed(range(L)):
        dx, per_layer_grads[l], above = _backward_layer(dx, layers[l], saved[l], pending, alpha, tables, tq, ts)
        if pending is not None:
            per_layer_grads[l + 1]['w_in'] = above
        pending = _as_halves(per_layer_grads[l]['w_in'])
    per_layer_grads[0]['w_in'] = _reduce_scattered(
        pending, _exchange("scatter_w_in", _scatter_transfer(pending)), 'w_in')
    gr = {n: jnp.stack([g[n] for g in per_layer_grads]) for n in per_layer_grads[0]}
    grads = {n: gr[n].reshape(w[n].shape) for n in BIG}

    dm_all = _all_gather8(gr['m6'].reshape(L, 6 * D), "gather_dmods")
    grads['b_ada'] = _sum_leading(jnp.pad(dm_all, ((0, 0), (0, 8 - L), (0, 0))), "reduce_b_ada")[:L]
    dm16 = jnp.pad(lax.dynamic_slice_in_dim(dm_all, shard * ada_cols, ada_cols, axis=2), ((0, 16 - N_DEV), (0, 0), (0, 0)))
    grads['w_ada'] = jnp.stack([_mm(c16, dm16[:, l], M=D, N=ada_cols, K=16, ta=True, a_tf=_silu, out_dtype=F32,
                                    name="ada_bwd_w") for l in range(L)])

    red = gr
    G, P = ssm_lam_re.shape[2], ssm_lam_re.shape[3]
    d_a = jnp.swapaxes(red['a'], 2, 3).reshape(L, 2, 2, G, P)
    (grads['ssm_lam_re'], grads['ssm_lam_im'], grads['ssm_log_step'], grads['ssm_b_re'],
     grads['ssm_b_im']) = disc_vjp((d_a[:, :, 0], d_a[:, :, 1], jnp.swapaxes(red['bb_re'], -1, -2),
                                    jnp.swapaxes(red['bb_im'], -1, -2)))
    grads['ssm_c_re'] = red['cc_re']
    grads['ssm_c_im'] = -red['cc_im']
    for n in ['ssm_d', 'b_glu', 'g_attn', 'g_ssm', 'ln1_g', 'ln1_b', 'ln2_g', 'ln2_b']:
        grads[n] = red[n].reshape(w[n].shape)

    delta, new_m, new_v = {}, {}, {}
    for n in ['w_ada'] + BIG:
        two_d = lambda t: t.reshape(-1, t.shape[-1])
        d_, m_, v_ = _adamw(two_d(w[n]), two_d(grads[n]), two_d(mom[n]), two_d(var[n]), "adamw_" + n)
        delta[n], new_m[n], new_v[n] = d_.reshape(w[n].shape), m_.reshape(w[n].shape), v_.reshape(w[n].shape)
    pk = lambda d: _pack([d[n] for n in SMALL], 1024)[0]
    d_, m_, v_ = _adamw(pk(w), pk(grads), pk(mom), pk(var), "adamw_small")
    like = [w[n] for n in SMALL]
    for n, dn, mn, vn in zip(SMALL, _unpack(d_, like), _unpack(m_, like), _unpack(v_, like)):
        delta[n], new_m[n], new_v[n] = dn, mn, vn

    return (loss, dx[None], *[grads[n] for n in WEIGHTS], *[delta[n] for n in WEIGHTS],
            *[new_m[n] for n in WEIGHTS], *[new_v[n] for n in WEIGHTS])
```

```python
import functools
import math

import jax
import jax.numpy as jnp
from jax import lax
from jax.experimental import pallas as pl
from jax.experimental.pallas import tpu as pltpu

F32 = jnp.float32
BF16 = jnp.bfloat16
MESH = pl.DeviceIdType.MESH
ANY = pl.BlockSpec(memory_space=pl.ANY)

HEAD_DIM = 64
SSM_GROUP = 16
SSM_STATE = 64
GROUPS_PER_BLOCK = 8
ATTN_PATTERNS = ((128, 1), (512, 4), (2048, 16))
LN_EPS = 1e-5
NEG_BIG = -1e30
ADAM_LR, ADAM_B1, ADAM_B2, ADAM_EPS, ADAM_WD, ADAM_STEP = 0.001, 0.9, 0.999, 1e-08, 0.01, 10
VMEM_LIMIT_V7X = 56 * 1024 * 1024
N_DEV = 8

WEIGHTS = ['w_ada', 'b_ada', 'w_in', 'ssm_lam_re', 'ssm_lam_im', 'ssm_log_step', 'ssm_b_re', 'ssm_b_im',
           'ssm_c_re', 'ssm_c_im', 'ssm_d', 'w_glu', 'b_glu', 'g_attn', 'g_ssm', 'w_out', 'ln1_g', 'ln1_b',
           'w_mlp1', 'w_mlp2', 'ln2_g', 'ln2_b']
BIG = ['w_in', 'w_glu', 'w_out', 'w_mlp1', 'w_mlp2']
SMALL = [n for n in WEIGHTS if n not in BIG and n != 'w_ada']


def _params(n_grid):
    return pltpu.CompilerParams(dimension_semantics=("arbitrary",) * n_grid, vmem_limit_bytes=VMEM_LIMIT_V7X)


def _tile(n, cap):
    t = 1 << (max(1, min(n, cap)).bit_length() - 1)
    while n % t:
        t //= 2
    return t


def _operand_spec(stack, transposed, tr, tc, nrb, ncb, pick):
    def index(i, j, k):
        r, c = pick(i, j, k)
        s = None
        if stack == 'r':
            s, r = r // nrb, r % nrb
        elif stack == 'c':
            s, c = c // ncb, c % ncb
        idx = (c, r) if transposed else (r, c)
        return idx if s is None else (s,) + idx
    blk = (tc, tr) if transposed else (tr, tc)
    return pl.BlockSpec(blk if stack is None else (None,) + blk, index)


def _mm(a, b, *, M, N, K, name, out_dtype, ta=False, tb=False, a_st=None, b_st=None, o_st=None, ns=1,
        a_tf=None, epi=None, extra=None, bias=None, cap_m=1024, cap_n=1024, cap_k=2048):
    pm = M // ns if 'm' in (a_st, o_st) else M
    pn = N // ns if 'n' in (b_st, o_st) else N
    pk = K // ns if 'k' in (a_st, b_st) else K
    tm, tn, tk = _tile(pm, cap_m), _tile(pn, cap_n), _tile(pk, cap_k)
    gm, gn, gk = M // tm, N // tn, K // tk
    nmb, nnb, nkb = pm // tm, pn // tn, pk // tk
    a_spec = _operand_spec({None: None, 'm': 'r', 'k': 'c'}[a_st], ta, tm, tk, nmb, nkb, lambda i, j, k: (i, k))
    b_spec = _operand_spec({None: None, 'k': 'r', 'n': 'c'}[b_st], tb, tk, tn, nkb, nnb, lambda i, j, k: (k, j))
    o_spec = _operand_spec({None: None, 'm': 'r', 'n': 'c'}[o_st], False, tm, tn, nmb, nnb, lambda i, j, k: (i, j))
    dn = (((0 if ta else 1,), (1 if tb else 0,)), ((), ()))
    ins, specs = [a, b], [a_spec, b_spec]
    if extra is not None:
        ins.append(extra)
        specs.append(o_spec)
    if bias is not None:
        ins.append(bias)
        specs.append(pl.BlockSpec((1, tn), lambda i, j, k: (0, j)))
    n_in = len(ins)
    if o_st is None:
        o_shape = (M, N)
    elif o_st == 'm':
        o_shape = (ns, pm, N)
    else:
        o_shape = (ns, M, pn)

    def body(*refs):
        a_ref, b_ref = refs[0], refs[1]
        o_ref, acc = refs[n_in], refs[n_in + 1]
        k = pl.program_id(2)

        @pl.when(k == 0)
        def _():
            acc[...] = jnp.zeros_like(acc)

        av = a_ref[...]
        if a_tf is not None:
            av = a_tf(av.astype(F32))
        acc[...] += lax.dot_general(av.astype(BF16), b_ref[...].astype(BF16), dn, preferred_element_type=F32)

        @pl.when(k == gk - 1)
        def _():
            r = acc[...]
            pos = 2
            if extra is not None:
                r = epi(r, refs[pos][...].astype(F32))
                pos += 1
            elif epi is not None:
                r = epi(r)
            if bias is not None:
                r = r + refs[pos][...]
            o_ref[...] = r.astype(o_ref.dtype)

    return pl.pallas_call(
        body, grid=(gm, gn, gk), in_specs=specs, out_specs=o_spec,
        out_shape=jax.ShapeDtypeStruct(o_shape, out_dtype),
        scratch_shapes=[pltpu.VMEM((tm, tn), F32)], compiler_params=_params(3), name=name)(*ins)


def _relu2(v):
    r = jnp.maximum(v, 0.0)
    return r * r


def _silu(v):
    return v / (1.0 + jnp.exp(-v))


def _rb(tr, w):
    return pl.BlockSpec((tr, w), lambda i: (i, 0))


def _pb(r, w):
    return pl.BlockSpec((r, w), lambda i: (0, 0))


def _row_call(body, n_rows, tr, ins, in_specs, outs, out_specs, name):
    return pl.pallas_call(body, grid=(n_rows // tr,), in_specs=in_specs, out_specs=out_specs,
                          out_shape=outs, compiler_params=_params(1), name=name)(*ins)


def _accumulate(ref, value):
    @pl.when(pl.program_id(0) == 0)
    def _():
        ref[...] = jnp.zeros_like(ref)
    ref[...] += jnp.sum(value, axis=0, keepdims=True)


def _modulate(x, m6, row_shift, row_scale, name):
    S, D = x.shape
    tr = _tile(S, 256)

    def body(x_ref, m_ref, h_ref):
        sh = m_ref[row_shift:row_shift + 1, :]
        sc = m_ref[row_scale:row_scale + 1, :]
        h_ref[...] = (x_ref[...] * (1.0 + sc) + sh).astype(BF16)

    return _row_call(body, S, tr, [x, m6], [_rb(tr, D), _pb(6, D)],
                     jax.ShapeDtypeStruct((S, D), BF16), _rb(tr, D), name)


def _gelu(v):
    c = math.sqrt(2.0 / math.pi)
    return 0.5 * v * (1.0 + jnp.tanh(c * (v + 0.044715 * v * v * v)))


def _gelu_grad(v):
    c = math.sqrt(2.0 / math.pi)
    t = jnp.tanh(c * (v + 0.044715 * v * v * v))
    return 0.5 * (1.0 + t) + 0.5 * v * (1.0 - t * t) * c * (1.0 + 3.0 * 0.044715 * v * v)


def _ssm_act(u, y0, y1, dskip):
    S, W = u.shape
    tr = _tile(S, 256)

    def body(u_ref, y0_ref, y1_ref, d_ref, pre_ref, g_ref):
        pre = u_ref[...] * d_ref[...] + y0_ref[...] + y1_ref[...]
        pre_ref[...] = pre
        g_ref[...] = _gelu(pre)

    return _row_call(body, S, tr, [u, y0, y1, dskip], [_rb(tr, W)] * 3 + [_pb(1, W)],
                     [jax.ShapeDtypeStruct((S, W), F32)] * 2, [_rb(tr, W)] * 2, "ssm_act")


def _merge(attn, g, z, b_glu, g_attn, g_ssm):
    S, W = attn.shape
    tr = _tile(S, 256)

    def body(a_ref, g_ref, z_ref, b_ref, ga_ref, gs_ref, o_ref):
        a = a_ref[...]
        ra = lax.rsqrt(jnp.mean(a * a, axis=-1, keepdims=True) + LN_EPS)
        o_ref[:, 0:W] = (a * ra * ga_ref[...]).astype(BF16)
        s = g_ref[...] / (1.0 + jnp.exp(-(z_ref[...] + b_ref[...])))
        rs = lax.rsqrt(jnp.mean(s * s, axis=-1, keepdims=True) + LN_EPS)
        o_ref[:, W:2 * W] = (s * rs * gs_ref[...]).astype(BF16)

    return _row_call(body, S, tr, [attn, g, z, b_glu, g_attn, g_ssm],
                     [_rb(tr, W)] * 3 + [_pb(1, W)] * 3,
                     jax.ShapeDtypeStruct((S, 2 * W), BF16), _rb(tr, 2 * W), "merge")


def _post_ln(x, branch, m6, row_gate, ln_g, ln_b, alpha, name):
    S, D = x.shape
    tr = _tile(S, 256)

    def body(x_ref, br_ref, m_ref, g_ref, b_ref, o_ref):
        gate = m_ref[row_gate:row_gate + 1, :]
        s = alpha * x_ref[...] + (1.0 + gate) * br_ref[...]
        mu = jnp.mean(s, axis=-1, keepdims=True)
        d = s - mu
        var = jnp.mean(d * d, axis=-1, keepdims=True)
        o_ref[...] = d * lax.rsqrt(var + LN_EPS) * g_ref[...] + b_ref[...]

    return _row_call(body, S, tr, [x, branch, m6, ln_g, ln_b],
                     [_rb(tr, D), _rb(tr, D), _pb(6, D), _pb(1, D), _pb(1, D)],
                     jax.ShapeDtypeStruct((S, D), F32), _rb(tr, D), name)


def _loss_head(x, target):
    S, D = x.shape
    tr = _tile(S, 256)

    def body(x_ref, t_ref, dx_ref, acc_ref):
        e = x_ref[...] - t_ref[...]
        dx_ref[...] = e * (1.0 / D)
        _accumulate(acc_ref, e * e)

    return _row_call(body, S, tr, [x, target], [_rb(tr, D)] * 2,
                     [jax.ShapeDtypeStruct((S, D), F32), jax.ShapeDtypeStruct((1, D), F32)],
                     [_rb(tr, D), _pb(1, D)], "loss_head")


def _ln_bwd(dxo, x_in, branch, m6, row_gate, ln_g, alpha, name):
    S, D = dxo.shape
    tr = _tile(S, 256)

    def body(dxo_ref, x_ref, br_ref, m_ref, g_ref, ds_ref, dbr_ref, dg_ref, db_ref, dgate_ref):
        gate = m_ref[row_gate:row_gate + 1, :]
        br = br_ref[...]
        s = alpha * x_ref[...] + (1.0 + gate) * br
        mu = jnp.mean(s, axis=-1, keepdims=True)
        d = s - mu
        var = jnp.mean(d * d, axis=-1, keepdims=True)
        rstd = lax.rsqrt(var + LN_EPS)
        xhat = d * rstd
        dxo = dxo_ref[...]
        dxh = dxo * g_ref[...]
        ds = rstd * (dxh - jnp.mean(dxh, axis=-1, keepdims=True)
                     - xhat * jnp.mean(dxh * xhat, axis=-1, keepdims=True))
        ds_ref[...] = ds
        dbr_ref[...] = ((1.0 + gate) * ds).astype(BF16)
        _accumulate(dg_ref, dxo * xhat)
        _accumulate(db_ref, dxo)
        _accumulate(dgate_ref, ds * br)

    vec = jax.ShapeDtypeStruct((1, D), F32)
    return _row_call(body, S, tr, [dxo, x_in, branch, m6, ln_g],
                     [_rb(tr, D)] * 3 + [_pb(6, D), _pb(1, D)],
                     [jax.ShapeDtypeStruct((S, D), F32), jax.ShapeDtypeStruct((S, D), BF16), vec, vec, vec],
                     [_rb(tr, D), _rb(tr, D), _pb(1, D), _pb(1, D), _pb(1, D)], name)


def _mod_bwd(ds, dh, x_in, m6, row_scale, alpha, name):
    S, D = ds.shape
    tr = _tile(S, 256)

    def body(ds_ref, dh_ref, x_ref, m_ref, dx_ref, dsc_ref, dsh_ref):
        sc = m_ref[row_scale:row_scale + 1, :]
        dh = dh_ref[...]
        dx_ref[...] = alpha * ds_ref[...] + dh * (1.0 + sc)
        _accumulate(dsc_ref, dh * x_ref[...])
        _accumulate(dsh_ref, dh)

    vec = jax.ShapeDtypeStruct((1, D), F32)
    return _row_call(body, S, tr, [ds, dh, x_in, m6], [_rb(tr, D)] * 3 + [_pb(6, D)],
                     [jax.ShapeDtypeStruct((S, D), F32), vec, vec],
                     [_rb(tr, D), _pb(1, D), _pb(1, D)], name)


def _merge_bwd(dmerged, attn, g, z, b_glu, g_attn, g_ssm):
    S, W = attn.shape
    tr = _tile(S, 256)

    def rms_bwd(x, gamma, dy):
        r = lax.rsqrt(jnp.mean(x * x, axis=-1, keepdims=True) + LN_EPS)
        gdy = gamma * dy
        dx = gdy * r - x * (r * r * r) * jnp.mean(gdy * x, axis=-1, keepdims=True)
        return dx, dy * x * r

    def body(dm_ref, a_ref, g_ref, z_ref, b_ref, ga_ref, gs_ref, da_ref, dgp_ref, dz_ref, dga_ref, dgs_ref, db_ref):
        da, dga = rms_bwd(a_ref[...], ga_ref[...], dm_ref[:, 0:W])
        da_ref[...] = da
        _accumulate(dga_ref, dga)
        gv = g_ref[...]
        sig = 1.0 / (1.0 + jnp.exp(-(z_ref[...] + b_ref[...])))
        dssm, dgs = rms_bwd(gv * sig, gs_ref[...], dm_ref[:, W:2 * W])
        _accumulate(dgs_ref, dgs)
        dgp_ref[...] = dssm * sig
        dz = dssm * gv * sig * (1.0 - sig)
        dz_ref[...] = dz.astype(BF16)
        _accumulate(db_ref, dz)

    vec = jax.ShapeDtypeStruct((1, W), F32)
    mat = jax.ShapeDtypeStruct((S, W), F32)
    return _row_call(body, S, tr, [dmerged, attn, g, z, b_glu, g_attn, g_ssm],
                     [_rb(tr, 2 * W)] + [_rb(tr, W)] * 3 + [_pb(1, W)] * 3,
                     [mat, mat, jax.ShapeDtypeStruct((S, W), BF16), vec, vec, vec],
                     [_rb(tr, W)] * 3 + [_pb(1, W)] * 3, "merge_bwd")


def _act_bwd(dgp, dgl, pre, u, dskip):
    S, W = pre.shape
    tr = _tile(S, 256)

    def body(a_ref, b_ref, pre_ref, u_ref, d_ref, dy_ref, du_ref, dd_ref):
        dy = (a_ref[...] + b_ref[...]) * _gelu_grad(pre_ref[...])
        dy_ref[...] = dy
        du_ref[...] = dy * d_ref[...]
        _accumulate(dd_ref, dy * u_ref[...])

    mat = jax.ShapeDtypeStruct((S, W), F32)
    return _row_call(body, S, tr, [dgp, dgl, pre, u, dskip], [_rb(tr, W)] * 4 + [_pb(1, W)],
                     [mat, mat, jax.ShapeDtypeStruct((1, W), F32)], [_rb(tr, W)] * 2 + [_pb(1, W)], "act_bwd")


def _dproj(dq, dkT, dvT, du_skip, du0, du1, reach):
    S, W = dq.shape
    tr = _tile(S, 256)
    lanes = 2 * HEAD_DIM
    n_pairs, nkb = W // lanes, tr // lanes
    assert tr % lanes == 0 and reach % tr == 0

    def body(dq_ref, dk_ref, dv_ref, a_ref, b_ref, c_ref, o_ref):
        o_ref[0] = dq_ref[...].astype(BF16)
        for h in range(n_pairs):
            for b in range(nkb):
                rows, cols = slice(b * lanes, (b + 1) * lanes), slice(h * lanes, (h + 1) * lanes)
                o_ref[1, rows, cols] = dk_ref[h, b].T.astype(BF16)
                o_ref[2, rows, cols] = dv_ref[h, b].T.astype(BF16)
        o_ref[3] = (a_ref[...] + b_ref[...] + c_ref[...]).astype(BF16)

    tsp = pl.BlockSpec((n_pairs, nkb, lanes, lanes), lambda i: (0, i + reach // tr, 0, 0))
    return _row_call(body, S, tr, [dq, dkT, dvT, du_skip, du0, du1],
                     [_rb(tr, W), tsp, tsp, _rb(tr, W), _rb(tr, W), _rb(tr, W)],
                     jax.ShapeDtypeStruct((4, S, W), BF16), pl.BlockSpec((4, tr, W), lambda i: (0, i, 0)), "dproj")


def _attn_reach():
    return max(w // 2 for w, _ in ATTN_PATTERNS)


def _bias_table(tq, width, reach, head):
    i = lax.broadcasted_iota(jnp.int32, (tq, width), 0)
    j = lax.broadcasted_iota(jnp.int32, (tq, width), 1)
    ad = jnp.abs(j - i - reach)
    mult = jnp.zeros((tq, width), jnp.int32)
    for window, dil in ATTN_PATTERNS:
        assert dil & (dil - 1) == 0
        mult += ((jnp.bitwise_and(ad, dil - 1) == 0) & (ad <= window // 2)).astype(jnp.int32)
    logm = jnp.zeros((tq, width), F32)
    for n in range(2, len(ATTN_PATTERNS) + 1):
        logm = jnp.where(mult == n, math.log(n), logm)
    return logm, ad.astype(F32), mult > 0


def _bias_tables(n_heads, tq):
    reach = _attn_reach()
    width = tq + 2 * reach

    def body(o_ref):
        logm, ad, ok = _bias_table(tq, width, reach, None)
        head = (pl.program_id(0) + 1).astype(F32)
        slope = jnp.exp(jnp.full((tq, width), -8.0 * math.log(2.0) / n_heads, F32) * head)
        o_ref[...] = jnp.where(ok, logm - slope * ad, NEG_BIG)

    return pl.pallas_call(body, grid=(n_heads,), out_specs=pl.BlockSpec((None, tq, width), lambda h: (h, 0, 0)),
                          out_shape=jax.ShapeDtypeStruct((n_heads, tq, width), F32), compiler_params=_params(1),
                          name="attn_bias")()


def _attn_setup(tables_ref, bias_ref, sem, tq):
    hp, qi = pl.program_id(0), pl.program_id(1)

    @pl.when(qi == 0)
    def _():
        cp = pltpu.make_async_copy(tables_ref.at[pl.ds(2 * hp, 2)], bias_ref, sem)
        cp.start()
        cp.wait()

    lane = lax.broadcasted_iota(jnp.int32, (1, 2 * HEAD_DIM), 1)
    return pl.multiple_of(qi * tq, tq), [lane < HEAD_DIM, lane >= HEAD_DIM]


def _key_validity(first_key, n, seq):
    kpos = first_key + lax.broadcasted_iota(jnp.int32, (1, n), 1)
    return jnp.where((kpos >= 0) & (kpos < seq), 0.0, NEG_BIG).astype(F32)


_NT = (((1,), (1,)), ((), ()))
assert math.log2(HEAD_DIM) % 2 == 0


def _carrier_call(body, grid, in_specs, out_specs, out_shape, scratch, name, ins, riders):
    if riders:
        first = lambda: (pl.program_id(0) == 0) & (pl.program_id(1) == 0)
        last = lambda: (pl.program_id(0) == grid[0] - 1) & (pl.program_id(1) == grid[1] - 1)
        body = _hosted(body, len(in_specs), len(out_specs), len(scratch), riders, first, last)
        name += "_carrier"
    rider_ins = [a for t in riders for a in t.ins]
    res = pl.pallas_call(
        body, grid=grid, in_specs=in_specs + [ANY] * len(rider_ins), out_specs=out_specs + [ANY] * len(riders),
        out_shape=out_shape + [t.out_shape for t in riders],
        scratch_shapes=scratch + [s for t in riders for s in t.semaphores()],
        compiler_params=_params(2), name=name)(*ins, *rider_ins)
    return res[:len(out_specs)], res[len(out_specs):]


def _attention_fwd(qkv, kp, vp, tables, tq, riders=()):
    _, S, W = qkv.shape
    reach = _attn_reach()
    width = tq + 2 * reach
    Sp = S + 2 * reach
    scale = HEAD_DIM ** -0.5
    lanes = 2 * HEAD_DIM

    kc = tq
    assert width % kc == 0

    def body(q_ref, k_ref, v_ref, tables_ref, o_ref, lse_ref, bias_ref, sem):
        start, masks = _attn_setup(tables_ref, bias_ref, sem, tq)
        qi = pl.program_id(1)
        q = q_ref[...] * scale
        out = jnp.zeros((tq, lanes), F32)
        lse = jnp.zeros((tq, lanes), F32)
        for a in range(2):
            qa = jnp.where(masks[a], q, jnp.zeros_like(q))
            m = jnp.full((tq, 1), 3.0 * NEG_BIG, F32)
            l = jnp.zeros((tq, 1), F32)
            acc = jnp.zeros((tq, lanes), F32)
            for c in range(width // kc):
                kw = k_ref[pl.ds(start + c * kc, kc), :]
                vw = v_ref[pl.ds(start + c * kc, kc), :]
                s = (lax.dot_general(qa, kw, _NT, preferred_element_type=F32) + bias_ref[a, :, c * kc:(c + 1) * kc]
                     + _key_validity(qi * tq - reach + c * kc, kc, S))
                m_new = jnp.maximum(m, jnp.max(s, axis=-1, keepdims=True))
                alpha = jnp.exp(m - m_new)
                p = jnp.exp(s - m_new)
                l = alpha * l + jnp.sum(p, axis=-1, keepdims=True)
                acc = alpha * acc + jnp.dot(p.astype(BF16), vw, preferred_element_type=F32)
                m = m_new
            out = jnp.where(masks[a], acc / l, out)
            lse = jnp.where(masks[a], m + jnp.log(l), lse)
        o_ref[...] = out
        lse_ref[...] = lse

    qsp = pl.BlockSpec((None, tq, lanes), lambda h, i: (0, i, h))
    ksp = pl.BlockSpec((Sp, lanes), lambda h, i: (0, h))
    osp = pl.BlockSpec((tq, lanes), lambda h, i: (i, h))
    mat = jax.ShapeDtypeStruct((S, W), F32)
    return _carrier_call(body, (W // lanes, S // tq), [qsp, ksp, ksp, ANY], [osp, osp], [mat, mat],
                           [pltpu.VMEM((2, tq, width), F32), pltpu.SemaphoreType.DMA], "attn_fwd",
                           (qkv, kp, vp, tables), list(riders))


def _attention_bwd(qkv, kp, vp, out, lse, dout, tables, tq, riders=()):
    _, S, W = qkv.shape
    reach = _attn_reach()
    width = tq + 2 * reach
    Sp = S + 2 * reach
    scale = HEAD_DIM ** -0.5
    lanes = 2 * HEAD_DIM
    kc = tq
    assert width % kc == 0 and kc % lanes == 0

    def body(q_ref, k_ref, v_ref, tables_ref, o_ref, lse_ref, do_ref, dq_ref, dk_ref, dv_ref, bias_ref, sem):
        start, masks = _attn_setup(tables_ref, bias_ref, sem, tq)
        qi = pl.program_id(1)

        @pl.when(qi == 0)
        def _():
            dk_ref[...] = jnp.zeros_like(dk_ref)
            dv_ref[...] = jnp.zeros_like(dv_ref)

        q = q_ref[...] * scale
        do = do_ref[...]
        prod = do * o_ref[...]
        lse_all = lse_ref[...]
        dq = jnp.zeros((tq, lanes), F32)
        for a in range(2):
            qa = jnp.where(masks[a], q, jnp.zeros_like(q))
            doa = jnp.where(masks[a], do, 0.0).astype(BF16)
            qat = qa.astype(F32).T.astype(BF16)
            doat = jnp.where(masks[a], do, 0.0).T.astype(BF16)
            lse_a = lse_all[:, a * HEAD_DIM:a * HEAD_DIM + 1]
            delta = jnp.sum(jnp.where(masks[a], prod, 0.0), axis=-1, keepdims=True)
            dq_a = jnp.zeros((tq, lanes), F32)
            for c in range(width // kc):
                kw = k_ref[pl.ds(start + c * kc, kc), :]
                vw = v_ref[pl.ds(start + c * kc, kc), :]
                s = (lax.dot_general(qa, kw, _NT, preferred_element_type=F32) + bias_ref[a, :, c * kc:(c + 1) * kc]
                     + _key_validity(qi * tq - reach + c * kc, kc, S))
                p = jnp.exp(s - lse_a)
                dp = lax.dot_general(doa, vw, _NT, preferred_element_type=F32)
                ds = (p * (dp - delta)).astype(BF16)
                dvt = jnp.dot(doat, p.astype(BF16), preferred_element_type=F32)
                dkt = jnp.dot(qat, ds, preferred_element_type=F32)
                dq_a += jnp.dot(ds, kw, preferred_element_type=F32)
                first = qi * (tq // lanes) + c * (kc // lanes)
                for j in range(kc // lanes):
                    dk_ref[first + j] += dkt[:, j * lanes:(j + 1) * lanes]
                    dv_ref[first + j] += dvt[:, j * lanes:(j + 1) * lanes]
            dq = jnp.where(masks[a], dq_a * scale, dq)
        dq_ref[...] = dq

    assert tq % lanes == 0 and reach % lanes == 0
    qsp = pl.BlockSpec((None, tq, lanes), lambda h, i: (0, i, h))
    ksp = pl.BlockSpec((Sp, lanes), lambda h, i: (0, h))
    osp = pl.BlockSpec((tq, lanes), lambda h, i: (i, h))
    tsp = pl.BlockSpec((None, Sp // lanes, lanes, lanes), lambda h, i: (h, 0, 0, 0))
    tiles = jax.ShapeDtypeStruct((W // lanes, Sp // lanes, lanes, lanes), F32)
    return _carrier_call(body, (W // lanes, S // tq), [qsp, ksp, ksp, ANY, osp, osp, osp], [osp, tsp, tsp],
                           [jax.ShapeDtypeStruct((S, W), F32), tiles, tiles],
                           [pltpu.VMEM((2, tq, width), F32), pltpu.SemaphoreType.DMA], "attn_bwd",
                           (qkv, kp, vp, tables, out, lse, dout), list(riders))


SB = GROUPS_PER_BLOCK * SSM_STATE
CB = GROUPS_PER_BLOCK * SSM_GROUP
RUNS = 8


def _powers(pw, a_r, a_i, n):
    p_r, p_i = a_r, a_i
    for j in range(n):
        pw[0, j:j + 1, :] = p_r
        pw[1, j:j + 1, :] = p_i
        p_r, p_i = p_r * a_r - p_i * a_i, p_r * a_i + p_i * a_r


def _rows8(j):
    return slice(j * RUNS, (j + 1) * RUNS)


def _run_scan(br, bi, T, a_r, a_i, pw, e_r, e_i, ent, down, conj):
    n = T // RUNS
    sg = -1.0 if conj else 1.0
    A_r = jnp.broadcast_to(a_r, (RUNS, SB))
    A_i = jnp.broadcast_to(sg * a_i, (RUNS, SB))
    x_r = x_i = None
    for j in (range(n - 1, -1, -1) if down else range(n)):
        b_r, b_i = br[_rows8(j), :], bi[_rows8(j), :]
        if x_r is None:
            x_r, x_i = b_r, b_i
        else:
            x_r, x_i = A_r * x_r - A_i * x_i + b_r, A_r * x_i + A_i * x_r + b_i
            br[_rows8(j), :] = x_r
            bi[_rows8(j), :] = x_i
    al_r, al_i = pw[0, n - 1:n, :], sg * pw[1, n - 1:n, :]
    s_r, s_i = e_r, e_i
    for c in (range(RUNS - 1, -1, -1) if down else range(RUNS)):
        ent[0, c:c + 1, :] = s_r
        ent[1, c:c + 1, :] = s_i
        s_r, s_i = (al_r * s_r - al_i * s_i + x_r[c:c + 1, :], al_r * s_i + al_i * s_r + x_i[c:c + 1, :])
    E_r, E_i = ent[0], ent[1]
    for j in range(n):
        p = n - 1 - j if down else j
        p_r, p_i = pw[0, p:p + 1, :], sg * pw[1, p:p + 1, :]
        v_r, v_i = br[_rows8(j), :], bi[_rows8(j), :]
        br[_rows8(j), :] = v_r + p_r * E_r - p_i * E_i
        bi[_rows8(j), :] = v_i + p_r * E_i + p_i * E_r
    return s_r, s_i


def _load_run_major(ref, T):
    n = T // RUNS
    return jnp.concatenate([ref[pl.ds(j, RUNS, stride=n), :] for j in range(n)], axis=0)


def _store_row_major(ref, value, T):
    n = T // RUNS
    for j in range(n):
        ref[pl.ds(j, RUNS, stride=n), :] = value[_rows8(j), :]


def _s5_specs(nb, T, rev, adjoint):
    flip = rev != adjoint
    blk = (lambda i: nb - 1 - i) if flip else (lambda i: i)
    usp = pl.BlockSpec((T, CB), lambda k, i: (blk(i), k))
    asp = pl.BlockSpec((None, 2, SB), lambda k, i: (k, 0, 0))
    wsp = pl.BlockSpec((None, CB, SB), lambda k, i: (k, 0, 0))
    csp = pl.BlockSpec((None, None, 2, SB), lambda k, i: (k, blk(i), 0, 0))
    return usp, asp, wsp, csp


def _s5_fwd(u, a, b_re, b_im, c_re, c_im, T, rev, riders=()):
    S, W = u.shape
    NK = W // CB
    nb = S // T
    usp, asp, wsp, csp = _s5_specs(nb, T, rev, False)

    def body(u_ref, a_ref, bre_ref, bim_ref, cre_ref, cim_ref, y_ref, car_ref, wr, wi, st, pw, ent):
        a_r, a_i = a_ref[0:1, :], a_ref[1:2, :]

        @pl.when(pl.program_id(1) == 0)
        def _():
            st[...] = jnp.zeros_like(st)
            _powers(pw, a_r, a_i, T // RUNS)

        car_ref[...] = st[0:2, :]
        ub = _load_run_major(u_ref, T).astype(BF16)
        wr[...] = jnp.dot(ub, bre_ref[...], preferred_element_type=F32)
        wi[...] = jnp.dot(ub, bim_ref[...], preferred_element_type=F32)
        s_r, s_i = _run_scan(wr, wi, T, a_r, a_i, pw, st[0:1, :], st[1:2, :], ent, rev, False)
        st[0:1, :] = s_r
        st[1:2, :] = s_i
        y = (lax.dot_general(wr[...].astype(BF16), cre_ref[...], _NT, preferred_element_type=F32)
             - lax.dot_general(wi[...].astype(BF16), cim_ref[...], _NT, preferred_element_type=F32))
        _store_row_major(y_ref, y, T)

    return _carrier_call(
        body, (NK, nb), [usp, asp, wsp, wsp, wsp, wsp], [usp, csp],
        [jax.ShapeDtypeStruct((S, W), F32), jax.ShapeDtypeStruct((NK, nb, 2, SB), F32)],
        [pltpu.VMEM((T, SB), F32), pltpu.VMEM((T, SB), F32), pltpu.VMEM((8, SB), F32),
         pltpu.VMEM((2, T // RUNS, SB), F32), pltpu.VMEM((2, RUNS, SB), F32)],
        "s5_fwd_rev" if rev else "s5_fwd", (u, a, b_re, b_im, c_re, c_im), list(riders))


def _s5_bwd(u, dy, a, b_re, b_im, c_re, c_im, carries, T, rev, riders=()):
    S, W = u.shape
    NK = W // CB
    nb = S // T
    usp, asp, wsp, csp = _s5_specs(nb, T, rev, True)
    n = T // RUNS

    def body(u_ref, dy_ref, a_ref, bre_ref, bim_ref, cre_ref, cim_ref, car_ref,
             du_ref, dbre_ref, dbim_ref, dcre_ref, dcim_ref, da_ref, wr, wi, gr, gi, lam, pw, ent):
        a_r, a_i = a_ref[0:1, :], a_ref[1:2, :]

        @pl.when(pl.program_id(1) == 0)
        def _():
            lam[...] = jnp.zeros_like(lam)
            for r in (dbre_ref, dbim_ref, dcre_ref, dcim_ref, da_ref):
                r[...] = jnp.zeros_like(r)
            _powers(pw, a_r, a_i, n)

        u32 = _load_run_major(u_ref, T)
        ub = u32.astype(BF16)
        dyk = _load_run_major(dy_ref, T)
        dyb = dyk.astype(BF16)
        wr[...] = jnp.dot(ub, bre_ref[...], preferred_element_type=F32)
        wi[...] = jnp.dot(ub, bim_ref[...], preferred_element_type=F32)
        x0r, x0i = car_ref[0:1, :], car_ref[1:2, :]
        _run_scan(wr, wi, T, a_r, a_i, pw, x0r, x0i, ent, rev, False)

        gr[...] = jnp.dot(dyb, cre_ref[...], preferred_element_type=F32)
        gi[...] = -jnp.dot(dyb, cim_ref[...], preferred_element_type=F32)
        l_r, l_i = _run_scan(gr, gi, T, a_r, a_i, pw, lam[0:1, :], lam[1:2, :], ent, not rev, True)
        lam[0:1, :] = l_r
        lam[1:2, :] = l_i

        sub = lax.broadcasted_iota(jnp.int32, (RUNS, SB), 0)

        def before(buf, x0, j):
            if rev:
                if j < n - 1:
                    return buf[_rows8(j + 1), :]
                return jnp.where(sub == RUNS - 1, x0, pltpu.roll(buf[_rows8(0), :], RUNS - 1, 0))
            if j > 0:
                return buf[_rows8(j - 1), :]
            return jnp.where(sub == 0, x0, pltpu.roll(buf[_rows8(n - 1), :], 1, 0))

        acc_r = jnp.zeros((RUNS, SB), F32)
        acc_i = jnp.zeros((RUNS, SB), F32)
        for j in range(n):
            g_r, g_i = gr[_rows8(j), :], gi[_rows8(j), :]
            p_r, p_i = before(wr, x0r, j), before(wi, x0i, j)
            acc_r += g_r * p_r + g_i * p_i
            acc_i += g_i * p_r - g_r * p_i
        da_ref[0:1, :] += jnp.sum(acc_r, axis=0, keepdims=True)
        da_ref[1:2, :] += jnp.sum(acc_i, axis=0, keepdims=True)

        lrb, lib = gr[...].astype(BF16), gi[...].astype(BF16)
        du = (lax.dot_general(lrb, bre_ref[...], _NT, preferred_element_type=F32)
              + lax.dot_general(lib, bim_ref[...], _NT, preferred_element_type=F32))
        _store_row_major(du_ref, du, T)
        ut = u32.T.astype(BF16)
        dbre_ref[...] += jnp.dot(ut, lrb, preferred_element_type=F32)
        dbim_ref[...] += jnp.dot(ut, lib, preferred_element_type=F32)
        dyt = dyk.T.astype(BF16)
        dcre_ref[...] += jnp.dot(dyt, wr[...].astype(BF16), preferred_element_type=F32)
        dcim_ref[...] += jnp.dot(dyt, wi[...].astype(BF16), preferred_element_type=F32)

    blk = jax.ShapeDtypeStruct((NK, CB, SB), F32)
    return _carrier_call(
        body, (NK, nb), [usp, usp, asp, wsp, wsp, wsp, wsp, csp], [usp, wsp, wsp, wsp, wsp, asp],
        [jax.ShapeDtypeStruct((S, W), F32), blk, blk, blk, blk, jax.ShapeDtypeStruct((NK, 2, SB), F32)],
        [pltpu.VMEM((T, SB), F32), pltpu.VMEM((T, SB), F32), pltpu.VMEM((T, SB), F32), pltpu.VMEM((T, SB), F32),
         pltpu.VMEM((8, SB), F32), pltpu.VMEM((2, n, SB), F32), pltpu.VMEM((2, RUNS, SB), F32)],
        "s5_bwd_rev" if rev else "s5_bwd", (u, dy, a, b_re, b_im, c_re, c_im, carries), list(riders))


def _s5_discretize(lam_re, lam_im, log_step, b_re, b_im):
    step = jnp.exp(log_step)[..., None]
    mag = jnp.exp(lam_re * step)
    a_re, a_im = mag * jnp.cos(lam_im * step), mag * jnp.sin(lam_im * step)
    den = lam_re * lam_re + lam_im * lam_im
    coef_re = ((a_re - 1.0) * lam_re + a_im * lam_im) / den
    coef_im = (a_im * lam_re - (a_re - 1.0) * lam_im) / den
    bb_re = coef_re[..., None] * b_re - coef_im[..., None] * b_im
    bb_im = coef_re[..., None] * b_im + coef_im[..., None] * b_re
    return a_re, a_im, bb_re, bb_im


def _to_blocks(w_gcp):
    two, G, C, P = w_gcp.shape
    nk = G // GROUPS_PER_BLOCK
    x = w_gcp.reshape(two, nk, GROUPS_PER_BLOCK, C, P)
    eye = jnp.eye(GROUPS_PER_BLOCK, dtype=w_gcp.dtype)
    return jnp.einsum('dkgcp,gh->dkgchp', x, eye).reshape(two, nk, GROUPS_PER_BLOCK * C, GROUPS_PER_BLOCK * P)


def _from_blocks(blk):
    two, nk, cb, sb = blk.shape
    C, P = cb // GROUPS_PER_BLOCK, sb // GROUPS_PER_BLOCK
    x = blk.reshape(two, nk, GROUPS_PER_BLOCK, C, GROUPS_PER_BLOCK, P)
    eye = jnp.eye(GROUPS_PER_BLOCK, dtype=blk.dtype)
    return jnp.einsum('dkgchp,gh->dkgcp', x, eye).reshape(two, nk * GROUPS_PER_BLOCK, C, P)


def _me():
    return lax.axis_index("x"), lax.axis_index("y"), lax.axis_index("c")


def _peer(k):
    x, y, c = _me()
    return (1 - x if k & 4 else x, 1 - y if k & 2 else y, 1 - c if k & 1 else c)


def _logical(dev):
    return 4 * dev[0] + 2 * dev[1] + dev[2]


class _Transfer:
    def __init__(self, ks, src_of, dst_of, out_shape, ins, split=1):
        self.ks, self.src_of, self.dst_of, self.out_shape, self.ins, self.split = ks, src_of, dst_of, out_shape, ins, split

    def semaphores(self):
        n = len(self.ks) * self.split
        return [pltpu.SemaphoreType.DMA((n,)), pltpu.SemaphoreType.DMA((n,))]

    def copies(self, in_refs, out_ref, send, recv):
        out = []
        for k in self.ks:
            src, dst = self.src_of(k, in_refs, out_ref), self.dst_of(k, in_refs, out_ref)
            rows = src.shape[0] // self.split
            for q in range(self.split):
                part = pl.ds(q * rows, rows)
                j = len(out)
                out.append(pltpu.make_async_remote_copy(
                    src_ref=src if self.split == 1 else src.at[part],
                    dst_ref=dst if self.split == 1 else dst.at[part],
                    send_sem=send.at[j], recv_sem=recv.at[j], device_id=_peer(k), device_id_type=MESH))
        return out


def _exchange(name, t):
    n_in = len(t.ins)

    def body(*refs):
        copies = t.copies(refs[:n_in], refs[n_in], *refs[n_in + 1:])
        for cp in copies:
            cp.start()
        for cp in copies:
            cp.wait_recv()
        for cp in copies:
            cp.wait_send()

    return pl.pallas_call(body, in_specs=[ANY] * n_in, out_specs=ANY, out_shape=t.out_shape,
                          scratch_shapes=t.semaphores(), name=name)(*t.ins)


def _hosted(body, n_in, n_out, n_scratch, riders, is_first, is_last):
    n_rin = sum(len(t.ins) for t in riders)

    def wrapped(*refs):
        host_in, rider_in = refs[:n_in], refs[n_in:n_in + n_rin]
        pos = n_in + n_rin
        host_out, rider_out = refs[pos:pos + n_out], refs[pos + n_out:pos + n_out + len(riders)]
        pos += n_out + len(riders)
        host_scratch, sems = refs[pos:pos + n_scratch], refs[pos + n_scratch:]

        def copies():
            out, at = [], 0
            for i, t in enumerate(riders):
                out += t.copies(rider_in[at:at + len(t.ins)], rider_out[i], sems[2 * i], sems[2 * i + 1])
                at += len(t.ins)
            return out

        @pl.when(is_first())
        def _():
            for cp in copies():
                cp.start()

        body(*host_in, *host_out, *host_scratch)

        @pl.when(is_last())
        def _():
            for cp in copies():
                cp.wait_recv()
            for cp in copies():
                cp.wait_send()

    return wrapped


def _gather8_transfer(v):
    out = jax.ShapeDtypeStruct((N_DEV,) + v.shape, v.dtype)
    slot = lambda k, ins, o: o.at[_logical(_me())]
    return _Transfer(list(range(1, 8)), lambda k, ins, o: ins[0], slot, out, [v])


def _all_gather8(v, name):
    got = _exchange(name, _gather8_transfer(v))
    return lax.dynamic_update_index_in_dim(got, v, _logical(_me()), 0)


def _halves_transfer(v):
    R, C = v.shape
    half = lambda ins: ins[0].at[pl.ds(lax.axis_index("c") * (R // 2), R // 2)]
    chip = lambda: 2 * lax.axis_index("x") + lax.axis_index("y")
    return _Transfer([2, 4, 6], lambda k, ins, o: half(ins), lambda k, ins, o: o.at[chip()],
                     jax.ShapeDtypeStruct((4, R // 2, C), v.dtype), [v])


def _join_halves(v, halves, name):
    R, C = v.shape
    c = lax.axis_index("c")
    mine = lax.dynamic_slice_in_dim(v, c * (R // 2), R // 2, axis=0)
    halves = lax.dynamic_update_index_in_dim(halves, mine, 2 * lax.axis_index("x") + lax.axis_index("y"), 0)
    core = lambda o: o.at[:, lax.axis_index("c")]
    both = _exchange(name + "_join", _Transfer(
        [1], lambda k, ins, o: ins[0], lambda k, ins, o: core(o),
        jax.ShapeDtypeStruct((4, 2, R // 2, C), v.dtype), [halves], split=4))
    return lax.dynamic_update_index_in_dim(both, halves, c, 1).reshape(4, R, C)


def _scatter_transfer(g):
    four, two, R2, C = g.shape
    piece = lambda dev, ins: ins[0].at[2 * dev[0] + dev[1], dev[2]]
    slot = lambda k, ins, o: o.at[_logical(_me())]
    return _Transfer(list(range(1, 8)), lambda k, ins, o: piece(_peer(k), ins), slot,
                     jax.ShapeDtypeStruct((N_DEV, R2, C), g.dtype), [g])


def _reduce_scattered(g, landed, name):
    x, y, c = _me()
    mine = lax.dynamic_index_in_dim(lax.dynamic_index_in_dim(g, 2 * x + y, 0, False), c, 0, False)
    landed = lax.dynamic_update_index_in_dim(landed, mine, _logical(_me()), 0)
    part = _sum_leading(landed, "reduce_" + name)
    return _swap_c(part, "swap_" + name).reshape(2 * part.shape[0], part.shape[1])


def _swap_c(v, name):
    out = jax.ShapeDtypeStruct((2,) + v.shape, v.dtype)
    slot = lambda: lax.axis_index("c")
    got = _exchange(name, _Transfer([1], lambda k, ins, o: ins[0], lambda k, ins, o: o.at[slot()], out, [v],
                                    split=8))
    return lax.dynamic_update_index_in_dim(got, v, slot(), 0)


def _sum_leading(v, name):
    n, R, C = v.shape
    tr = _tile(R, max(8, (1 << 19) // (C * n)))

    def body(v_ref, o_ref):
        acc = v_ref[0].astype(F32)
        for s in range(1, n):
            acc = acc + v_ref[s].astype(F32)
        o_ref[...] = acc

    return pl.pallas_call(body, grid=(R // tr,), in_specs=[pl.BlockSpec((n, tr, C), lambda i: (0, i, 0))],
                          out_specs=pl.BlockSpec((tr, C), lambda i: (i, 0)),
                          out_shape=jax.ShapeDtypeStruct((R, C), F32), compiler_params=_params(1), name=name)(v)


def _adamw(w, g, m, v, name):
    R, C = w.shape
    tr = _tile(R, max(8, (1 << 18) // C))
    c1 = 1.0 - ADAM_B1 ** ADAM_STEP
    c2 = 1.0 - ADAM_B2 ** ADAM_STEP

    def body(w_ref, g_ref, m_ref, v_ref, d_ref, nm_ref, nv_ref):
        gv = g_ref[...]
        nm = ADAM_B1 * m_ref[...] + (1.0 - ADAM_B1) * gv
        nv = ADAM_B2 * v_ref[...] + (1.0 - ADAM_B2) * (gv * gv)
        nm_ref[...] = nm
        nv_ref[...] = nv
        d_ref[...] = -ADAM_LR * ((nm / c1) / (jnp.sqrt(nv / c2) + ADAM_EPS) + ADAM_WD * w_ref[...])

    sp = pl.BlockSpec((tr, C), lambda i: (i, 0))
    return pl.pallas_call(body, grid=(R // tr,), in_specs=[sp] * 4, out_specs=[sp] * 3,
                          out_shape=[jax.ShapeDtypeStruct((R, C), F32)] * 3, compiler_params=_params(1),
                          name=name)(w, g, m, v)


def _forward_layer(x, lw, w_in_full, shards, next_w_in, alpha, tables, tq, ts):
    S, D = x.shape
    W = D // 2
    reach = _attn_reach()
    m6 = lw['m6']
    gw = {'w_in': w_in_full}
    FF = shards['w_mlp1'].shape[1] * 4

    h = _modulate(x, m6, 0, 1, "modulate1")
    qkv = _mm(h, gw['w_in'], M=S, N=3 * W, K=D, b_st='n', o_st='n', ns=3, out_dtype=BF16, cap_k=2048, name="proj_qkv")
    u = _mm(h, gw['w_in'][3], M=S, N=W, K=D, out_dtype=F32, cap_k=2048, name="proj_u")
    pad = ((reach, reach), (0, 0))
    kp, vp = jnp.pad(qkv[1], pad), jnp.pad(qkv[2], pad)
    carried = [(n, shards[n]) for n in ('w_glu', 'w_out')] + ([('w_in', next_w_in)] if next_w_in is not None else [])
    (attn, lse), halves = _attention_fwd(qkv, kp, vp, tables, tq, [_halves_transfer(v) for _, v in carried])
    joined = [_join_halves(v, hv, "gather_" + n) for (n, v), hv in zip(carried, halves)]
    gw.update({n: j for (n, _), j in zip(carried[:2], joined)})
    w_in_next = joined[-1] if next_w_in is not None else None
    mlp = ['w_mlp1', 'w_mlp2']
    s5 = [_s5_fwd(u, lw['a'][d], lw['b_re'][d], lw['b_im'][d], lw['c_re'][d], lw['c_im'][d], ts, rev=bool(d),
                  riders=[_halves_transfer(shards[mlp[d]])]) for d in range(2)]
    gw.update({mlp[d]: _join_halves(shards[mlp[d]], s5[d][1][0], "gather_" + mlp[d]) for d in range(2)})
    y, carries = zip(*[s5[d][0] for d in range(2)])
    pre, g = _ssm_act(u, y[0], y[1], lw['ssm_d'])
    z = _mm(g, gw['w_glu'].reshape(W, W), M=S, N=W, K=W, out_dtype=F32, name="glu")
    merged = _merge(attn, g, z, lw['b_glu'], lw['g_attn'], lw['g_ssm'])
    mix = _mm(merged, gw['w_out'].reshape(D, D), M=S, N=D, K=D, out_dtype=F32, cap_k=2048, name="out_proj")
    x1 = _post_ln(x, mix, m6, 2, lw['ln1_g'], lw['ln1_b'], alpha, "post_ln1")
    h2 = _modulate(x1, m6, 3, 4, "modulate2")
    a4 = _mm(h2, gw['w_mlp1'], M=S, N=FF, K=D, b_st='n', o_st='n', ns=4, out_dtype=F32, cap_k=2048, name="mlp1")
    ff = _mm(a4, gw['w_mlp2'], M=S, N=D, K=FF, a_st='k', b_st='k', ns=4, a_tf=_relu2, out_dtype=F32, name="mlp2")
    x2 = _post_ln(x1, ff, m6, 5, lw['ln2_g'], lw['ln2_b'], alpha, "post_ln2")
    saved = dict(x=x, h=h, qkv=qkv, u=u, attn=attn, lse=lse, carries=carries, pre=pre, g=g, z=z, merged=merged,
                 mix=mix, x1=x1, h2=h2, a4=a4, ff=ff, gw=gw)
    return x2, saved, w_in_next


SMALL_GRADS = ['a', 'bb_re', 'bb_im', 'cc_re', 'cc_im', 'ssm_d', 'b_glu', 'g_attn', 'g_ssm',
               'ln1_g', 'ln1_b', 'ln2_g', 'ln2_b']


def _as_halves(g):
    C = g.shape[-1]
    return g.reshape(4, 2, g.size // (8 * C), C)


def _backward_layer(dx2, lw, sv, pending_w_in, alpha, tables, tq, ts):
    S, D = dx2.shape
    W = D // 2
    reach = _attn_reach()
    m6, gw = lw['m6'], sv['gw']
    FF = gw['w_mlp1'].shape[2] * 4
    gr = {}

    ds2, dff, gr['ln2_g'], gr['ln2_b'], d_g2 = _ln_bwd(dx2, sv['x1'], sv['ff'], m6, 5, lw['ln2_g'], alpha, "ln2_bwd")
    da4 = _mm(dff, gw['w_mlp2'], M=S, N=FF, K=D, tb=True, b_st='n', o_st='n', ns=4, extra=sv['a4'],
              epi=lambda acc, a: 2.0 * jnp.maximum(a, 0.0) * acc, out_dtype=BF16, cap_k=2048, name="mlp2_bwd_x")
    gr['w_mlp2'] = _mm(sv['a4'], dff, M=FF, N=D, K=S, ta=True, a_st='m', o_st='m', ns=4, a_tf=_relu2,
                       out_dtype=BF16, name="mlp2_bwd_w")
    dh2 = _mm(da4, gw['w_mlp1'], M=S, N=D, K=FF, tb=True, a_st='k', b_st='k', ns=4, out_dtype=F32, name="mlp1_bwd_x")
    gr['w_mlp1'] = _mm(sv['h2'], da4, M=D, N=FF, K=S, ta=True, b_st='n', o_st='n', ns=4, out_dtype=BF16,
                       name="mlp1_bwd_w")
    dx1, d_sc2, d_sh2 = _mod_bwd(ds2, dh2, sv['x1'], m6, 4, alpha, "mod2_bwd")

    ds1, dmix, gr['ln1_g'], gr['ln1_b'], d_g1 = _ln_bwd(dx1, sv['x'], sv['mix'], m6, 2, lw['ln1_g'], alpha, "ln1_bwd")
    dmerged = _mm(dmix, gw['w_out'].reshape(D, D), M=S, N=D, K=D, tb=True, out_dtype=F32, cap_k=2048, name="out_proj_bwd_x")
    gr['w_out'] = _mm(sv['merged'], dmix, M=D, N=D, K=S, ta=True, out_dtype=BF16, name="out_proj_bwd_w")
    dattn, dgp, dz, gr['g_attn'], gr['g_ssm'], gr['b_glu'] = _merge_bwd(
        dmerged, sv['attn'], sv['g'], sv['z'], lw['b_glu'], lw['g_attn'], lw['g_ssm'])
    dgl = _mm(dz, gw['w_glu'].reshape(W, W), M=S, N=W, K=W, tb=True, out_dtype=F32, name="glu_bwd_x")
    gr['w_glu'] = _mm(sv['g'], dz, M=W, N=W, K=S, ta=True, out_dtype=BF16, name="glu_bwd_w")
    dy, du_skip, gr['ssm_d'] = _act_bwd(dgp, dgl, sv['pre'], sv['u'], lw['ssm_d'])
    mlp = [_as_halves(gr['w_mlp2']), _as_halves(gr['w_mlp1'])]
    s5 = [_s5_bwd(sv['u'], dy, lw['a'][d], lw['b_re'][d], lw['b_im'][d], lw['c_re'][d], lw['c_im'][d],
                  sv['carries'][d], ts, rev=bool(d), riders=[_scatter_transfer(mlp[d])]) for d in range(2)]
    gr['w_mlp2'] = _reduce_scattered(mlp[0], s5[0][1][0], 'w_mlp2')
    gr['w_mlp1'] = _reduce_scattered(mlp[1], s5[1][1][0], 'w_mlp1')
    du = [s5[d][0][0] for d in range(2)]
    for pos, n in enumerate(['bb_re', 'bb_im', 'cc_re', 'cc_im', 'a']):
        gr[n] = jnp.stack([s5[d][0][pos + 1] for d in range(2)])
    pad = ((reach, reach), (0, 0))
    kp, vp = jnp.pad(sv['qkv'][1], pad), jnp.pad(sv['qkv'][2], pad)
    for n in ['bb_re', 'bb_im', 'cc_re', 'cc_im']:
        gr[n] = _from_blocks(gr[n])
    small, _ = _pack([gr[n] for n in SMALL_GRADS], 1024)
    scattered = [(n, _as_halves(gr[n])) for n in ('w_glu', 'w_out')]
    if pending_w_in is not None:
        scattered.append(('w_in', pending_w_in))
    riders = [_scatter_transfer(g) for _, g in scattered] + [_gather8_transfer(small)]
    (dq, dkT, dvT), landed = _attention_bwd(sv['qkv'], kp, vp, sv['attn'], sv['lse'], dattn, tables, tq, riders)
    reduced = [_reduce_scattered(g, lv, n) for (n, g), lv in zip(scattered, landed)]
    gr.update({n: r for (n, _), r in zip(scattered[:2], reduced)})
    w_in_above = reduced[-1] if pending_w_in is not None else None
    own = lax.dynamic_update_index_in_dim(landed[-1], small, _logical(_me()), 0)
    summed = _unpack(_sum_leading(own, "reduce_small"), [gr[n] for n in SMALL_GRADS])
    gr.update(dict(zip(SMALL_GRADS, summed)))
    dproj = _dproj(dq, dkT, dvT, du_skip, du[0], du[1], reach)
    dh = _mm(dproj, gw['w_in'], M=S, N=D, K=4 * W, tb=True, a_st='k', b_st='k', ns=4, out_dtype=F32, cap_n=2048, name="proj_bwd_x")
    gr['w_in'] = _mm(sv['h'], dproj, M=D, N=4 * W, K=S, ta=True, b_st='n', o_st='n', ns=4, out_dtype=BF16,
                     name="proj_bwd_w")
    dx, d_sc1, d_sh1 = _mod_bwd(ds1, dh, sv['x'], m6, 1, alpha, "mod1_bwd")
    gr['m6'] = jnp.concatenate([d_sh1, d_sc1, d_g1, d_sh2, d_sc2, d_g2], axis=0)
    return dx, gr, w_in_above


def _pack(arrays, width):
    flat = jnp.concatenate([a.reshape(-1) for a in arrays])
    n = flat.shape[0]
    rows = -(-n // (8 * width)) * 8
    return jnp.pad(flat, (0, rows * width - n)).reshape(rows, width), n


def _unpack(packed, like):
    flat = packed.reshape(-1)
    out, pos = [], 0
    for a in like:
        out.append(flat[pos:pos + a.size].reshape(a.shape))
        pos += a.size
    return out


def kernel(x, c, w_ada, b_ada, w_in, ssm_lam_re, ssm_lam_im, ssm_log_step, ssm_b_re, ssm_b_im, ssm_c_re, ssm_c_im, ssm_d, w_glu, b_glu, g_attn, g_ssm, w_out, ln1_g, ln1_b, w_mlp1, w_mlp2, ln2_g, ln2_b, loss_target, m_w_ada, m_b_ada, m_w_in, m_ssm_lam_re, m_ssm_lam_im, m_ssm_log_step, m_ssm_b_re, m_ssm_b_im, m_ssm_c_re, m_ssm_c_im, m_ssm_d, m_w_glu, m_b_glu, m_g_attn, m_g_ssm, m_w_out, m_ln1_g, m_ln1_b, m_w_mlp1, m_w_mlp2, m_ln2_g, m_ln2_b, v_w_ada, v_b_ada, v_w_in, v_ssm_lam_re, v_ssm_lam_im, v_ssm_log_step, v_ssm_b_re, v_ssm_b_im, v_ssm_c_re, v_ssm_c_im, v_ssm_d, v_w_glu, v_b_glu, v_g_attn, v_g_ssm, v_w_out, v_ln1_g, v_ln1_b, v_w_mlp1, v_w_mlp2, v_ln2_g, v_ln2_b):
    args = locals()
    w = {n: args[n] for n in WEIGHTS}
    mom = {n: args['m_' + n] for n in WEIGHTS}
    var = {n: args['v_' + n] for n in WEIGHTS}
    L, D, ada_cols = w_ada.shape
    S = x.shape[1]
    W = D // 2
    n_heads = W // HEAD_DIM
    alpha = (2 * L) ** 0.25
    tq = _tile(S, 512)
    ts = _tile(S, 512)
    tables = _bias_tables(n_heads, tq)
    xi, yi, ci = _me()
    shard = 2 * xi + yi
    me = 4 * xi + 2 * yi + ci

    c_all = _all_gather8(c, "gather_c").reshape(N_DEV, D)
    c16 = jnp.pad(c_all, ((0, 16 - N_DEV), (0, 0)))
    b_sh = lax.dynamic_slice_in_dim(b_ada, shard * ada_cols, ada_cols, axis=1)
    mods_sh = jnp.stack([_mm(c16, w_ada[l], M=16, N=ada_cols, K=D, a_tf=_silu, bias=b_sh[l:l + 1], out_dtype=F32,
                             name="ada") for l in range(L)])
    mods_all = _all_gather8(mods_sh, "gather_mods")
    mods = jnp.concatenate([mods_all[2 * j] for j in range(4)], axis=-1)
    m6_all = lax.dynamic_index_in_dim(mods, me, axis=1, keepdims=False).reshape(L, 6, D)

    (a_re, a_im, bb_re, bb_im), disc_vjp = jax.vjp(
        _s5_discretize, ssm_lam_re, ssm_lam_im, ssm_log_step, ssm_b_re, ssm_b_im)
    N = a_re.shape[2] * a_re.shape[3]
    per_layer = dict(
        m6=m6_all,
        a=jnp.stack([a_re.reshape(L, 2, N // SB, SB), a_im.reshape(L, 2, N // SB, SB)], axis=3),
        b_re=jax.vmap(_to_blocks)(jnp.swapaxes(bb_re, -1, -2)).astype(BF16),
        b_im=jax.vmap(_to_blocks)(jnp.swapaxes(bb_im, -1, -2)).astype(BF16),
        c_re=jax.vmap(_to_blocks)(ssm_c_re).astype(BF16),
        c_im=jax.vmap(_to_blocks)(ssm_c_im).astype(BF16),
        ssm_d=ssm_d[:, None], b_glu=b_glu[:, None], g_attn=g_attn[:, None], g_ssm=g_ssm[:, None],
        ln1_g=ln1_g[:, None], ln1_b=ln1_b[:, None], ln2_g=ln2_g[:, None], ln2_b=ln2_b[:, None],
        w_in=w_in, w_glu=w_glu, w_out=w_out, w_mlp1=w_mlp1, w_mlp2=w_mlp2)

    layers = [{n: v[l] for n, v in per_layer.items()} for l in range(L)]
    shards = [{n: layers[l][n].astype(BF16) for n in BIG} for l in range(L)]
    first = shards[0]['w_in']
    w_in_full = _join_halves(first, _exchange("gather_w_in", _halves_transfer(first)), "gather_w_in")
    xc, saved = x[0], []
    for l in range(L):
        xc, sv, w_in_full = _forward_layer(xc, layers[l], w_in_full, shards[l],
                                           shards[l + 1]['w_in'] if l + 1 < L else None, alpha, tables, tq, ts)
        saved.append(sv)
    dx, sq = _loss_head(xc, loss_target[0])
    loss = lax.psum(0.5 * jnp.sum(sq) / D, ("x", "y", "c"))

    per_layer_grads = [None] * L
    pending = None
    for l in reversed(range(L)):
        dx, per_layer_grads[l], above = _backward_layer(dx, layers[l], saved[l], pending, alpha, tables, tq, ts)
        if pending is not None:
            per_layer_grads[l + 1]['w_in'] = above
        pending = _as_halves(per_layer_grads[l]['w_in'])
    per_layer_grads[0]['w_in'] = _reduce_scattered(
        pending, _exchange("scatter_w_in", _scatter_transfer(pending)), 'w_in')
    gr = {n: jnp.stack([g[n] for g in per_layer_grads]) for n in per_layer_grads[0]}
    grads = {n: gr[n].reshape(w[n].shape) for n in BIG}

    dm_all = _all_gather8(gr['m6'].reshape(L, 6 * D), "gather_dmods")
    grads['b_ada'] = _sum_leading(jnp.pad(dm_all, ((0, 0), (0, 8 - L), (0, 0))), "reduce_b_ada")[:L]
    dm16 = jnp.pad(lax.dynamic_slice_in_dim(dm_all, shard * ada_cols, ada_cols, axis=2), ((0, 16 - N_DEV), (0, 0), (0, 0)))
    grads['w_ada'] = jnp.stack([_mm(c16, dm16[:, l], M=D, N=ada_cols, K=16, ta=True, a_tf=_silu, out_dtype=F32,
                                    name="ada_bwd_w") for l in range(L)])

    red = gr
    G, P = ssm_lam_re.shape[2], ssm_lam_re.shape[3]
    d_a = jnp.swapaxes(red['a'], 2, 3).reshape(L, 2, 2, G, P)
    (grads['ssm_lam_re'], grads['ssm_lam_im'], grads['ssm_log_step'], grads['ssm_b_re'],
     grads['ssm_b_im']) = disc_vjp((d_a[:, :, 0], d_a[:, :, 1], jnp.swapaxes(red['bb_re'], -1, -2),
                                    jnp.swapaxes(red['bb_im'], -1, -2)))
    grads['ssm_c_re'] = red['cc_re']
    grads['ssm_c_im'] = -red['cc_im']
    for n in ['ssm_d', 'b_glu', 'g_attn', 'g_ssm', 'ln1_g', 'ln1_b', 'ln2_g', 'ln2_b']:
        grads[n] = red[n].reshape(w[n].shape)

    delta, new_m, new_v = {}, {}, {}
    for n in ['w_ada'] + BIG:
        two_d = lambda t: t.reshape(-1, t.shape[-1])
        d_, m_, v_ = _adamw(two_d(w[n]), two_d(grads[n]), two_d(mom[n]), two_d(var[n]), "adamw_" + n)
        delta[n], new_m[n], new_v[n] = d_.reshape(w[n].shape), m_.reshape(w[n].shape), v_.reshape(w[n].shape)
    pk = lambda d: _pack([d[n] for n in SMALL], 1024)[0]
    d_, m_, v_ = _adamw(pk(w), pk(grads), pk(mom), pk(var), "adamw_small")
    like = [w[n] for n in SMALL]
    for n, dn, mn, vn in zip(SMALL, _unpack(d_, like), _unpack(m_, like), _unpack(v_, like)):
        delta[n], new_m[n], new_v[n] = dn, mn, vn

    return (loss, dx[None], *[grads[n] for n in WEIGHTS], *[delta[n] for n in WEIGHTS],
            *[new_m[n] for n in WEIGHTS], *[new_v[n] for n in WEIGHTS])
```

```python
import functools
import math

import jax
import jax.numpy as jnp
from jax import lax
from jax.experimental import pallas as pl
from jax.experimental.pallas import tpu as pltpu

F32 = jnp.float32
BF16 = jnp.bfloat16
MESH = pl.DeviceIdType.MESH
ANY = pl.BlockSpec(memory_space=pl.ANY)

HEAD_DIM = 64
SSM_GROUP = 16
SSM_STATE = 64
GROUPS_PER_BLOCK = 8
ATTN_PATTERNS = ((128, 1), (512, 4), (2048, 16))
LN_EPS = 1e-5
NEG_BIG = -1e30
ADAM_LR, ADAM_B1, ADAM_B2, ADAM_EPS, ADAM_WD, ADAM_STEP = 0.001, 0.9, 0.999, 1e-08, 0.01, 10
VMEM_LIMIT_V7X = 56 * 1024 * 1024
N_DEV = 8

WEIGHTS = ['w_ada', 'b_ada', 'w_in', 'ssm_lam_re', 'ssm_lam_im', 'ssm_log_step', 'ssm_b_re', 'ssm_b_im',
           'ssm_c_re', 'ssm_c_im', 'ssm_d', 'w_glu', 'b_glu', 'g_attn', 'g_ssm', 'w_out', 'ln1_g', 'ln1_b',
           'w_mlp1', 'w_mlp2', 'ln2_g', 'ln2_b']
BIG = ['w_in', 'w_glu', 'w_out', 'w_mlp1', 'w_mlp2']
SMALL = [n for n in WEIGHTS if n not in BIG and n != 'w_ada']


def _params(n_grid):
    return pltpu.CompilerParams(dimension_semantics=("arbitrary",) * n_grid, vmem_limit_bytes=VMEM_LIMIT_V7X)


def _tile(n, cap):
    t = 1 << (max(1, min(n, cap)).bit_length() - 1)
    while n % t:
        t //= 2
    return t


def _operand_spec(stack, transposed, tr, tc, nrb, ncb, pick):
    def index(i, j, k):
        r, c = pick(i, j, k)
        s = None
        if stack == 'r':
            s, r = r // nrb, r % nrb
        elif stack == 'c':
            s, c = c // ncb, c % ncb
        idx = (c, r) if transposed else (r, c)
        return idx if s is None else (s,) + idx
    blk = (tc, tr) if transposed else (tr, tc)
    return pl.BlockSpec(blk if stack is None else (None,) + blk, index)


def _mm(a, b, *, M, N, K, name, out_dtype, ta=False, tb=False, a_st=None, b_st=None, o_st=None, ns=1,
        a_tf=None, epi=None, extra=None, bias=None, cap_m=1024, cap_n=1024, cap_k=2048):
    pm = M // ns if 'm' in (a_st, o_st) else M
    pn = N // ns if 'n' in (b_st, o_st) else N
    pk = K // ns if 'k' in (a_st, b_st) else K
    tm, tn, tk = _tile(pm, cap_m), _tile(pn, cap_n), _tile(pk, cap_k)
    gm, gn, gk = M // tm, N // tn, K // tk
    nmb, nnb, nkb = pm // tm, pn // tn, pk // tk
    a_spec = _operand_spec({None: None, 'm': 'r', 'k': 'c'}[a_st], ta, tm, tk, nmb, nkb, lambda i, j, k: (i, k))
    b_spec = _operand_spec({None: None, 'k': 'r', 'n': 'c'}[b_st], tb, tk, tn, nkb, nnb, lambda i, j, k: (k, j))
    o_spec = _operand_spec({None: None, 'm': 'r', 'n': 'c'}[o_st], False, tm, tn, nmb, nnb, lambda i, j, k: (i, j))
    dn = (((0 if ta else 1,), (1 if tb else 0,)), ((), ()))
    ins, specs = [a, b], [a_spec, b_spec]
    if extra is not None:
        ins.append(extra)
        specs.append(o_spec)
    if bias is not None:
        ins.append(bias)
        specs.append(pl.BlockSpec((1, tn), lambda i, j, k: (0, j)))
    n_in = len(ins)
    if o_st is None:
        o_shape = (M, N)
    elif o_st == 'm':
        o_shape = (ns, pm, N)
    else:
        o_shape = (ns, M, pn)

    def body(*refs):
        a_ref, b_ref = refs[0], refs[1]
        o_ref, acc = refs[n_in], refs[n_in + 1]
        k = pl.program_id(2)

        @pl.when(k == 0)
        def _():
            acc[...] = jnp.zeros_like(acc)

        av = a_ref[...]
        if a_tf is not None:
            av = a_tf(av.astype(F32))
        acc[...] += lax.dot_general(av.astype(BF16), b_ref[...].astype(BF16), dn, preferred_element_type=F32)

        @pl.when(k == gk - 1)
        def _():
            r = acc[...]
            pos = 2
            if extra is not None:
                r = epi(r, refs[pos][...].astype(F32))
                pos += 1
            elif epi is not None:
                r = epi(r)
            if bias is not None:
                r = r + refs[pos][...]
            o_ref[...] = r.astype(o_ref.dtype)

    return pl.pallas_call(
        body, grid=(gm, gn, gk), in_specs=specs, out_specs=o_spec,
        out_shape=jax.ShapeDtypeStruct(o_shape, out_dtype),
        scratch_shapes=[pltpu.VMEM((tm, tn), F32)], compiler_params=_params(3), name=name)(*ins)


def _relu2(v):
    r = jnp.maximum(v, 0.0)
    return r * r


def _silu(v):
    return v / (1.0 + jnp.exp(-v))


def _rb(tr, w):
    return pl.BlockSpec((tr, w), lambda i: (i, 0))


def _pb(r, w):
    return pl.BlockSpec((r, w), lambda i: (0, 0))


def _row_call(body, n_rows, tr, ins, in_specs, outs, out_specs, name):
    return pl.pallas_call(body, grid=(n_rows // tr,), in_specs=in_specs, out_specs=out_specs,
                          out_shape=outs, compiler_params=_params(1), name=name)(*ins)


def _accumulate(ref, value):
    @pl.when(pl.program_id(0) == 0)
    def _():
        ref[...] = jnp.zeros_like(ref)
    ref[...] += jnp.sum(value, axis=0, keepdims=True)


def _modulate(x, m6, row_shift, row_scale, name):
    S, D = x.shape
    tr = _tile(S, 256)

    def body(x_ref, m_ref, h_ref):
        sh = m_ref[row_shift:row_shift + 1, :]
        sc = m_ref[row_scale:row_scale + 1, :]
        h_ref[...] = (x_ref[...] * (1.0 + sc) + sh).astype(BF16)

    return _row_call(body, S, tr, [x, m6], [_rb(tr, D), _pb(6, D)],
                     jax.ShapeDtypeStruct((S, D), BF16), _rb(tr, D), name)


def _gelu(v):
    c = math.sqrt(2.0 / math.pi)
    return 0.5 * v * (1.0 + jnp.tanh(c * (v + 0.044715 * v * v * v)))


def _gelu_grad(v):
    c = math.sqrt(2.0 / math.pi)
    t = jnp.tanh(c * (v + 0.044715 * v * v * v))
    return 0.5 * (1.0 + t) + 0.5 * v * (1.0 - t * t) * c * (1.0 + 3.0 * 0.044715 * v * v)


def _ssm_act(u, y0, y1, dskip):
    S, W = u.shape
    tr = _tile(S, 256)

    def body(u_ref, y0_ref, y1_ref, d_ref, pre_ref, g_ref):
        pre = u_ref[...] * d_ref[...] + y0_ref[...] + y1_ref[...]
        pre_ref[...] = pre
        g_ref[...] = _gelu(pre)

    return _row_call(body, S, tr, [u, y0, y1, dskip], [_rb(tr, W)] * 3 + [_pb(1, W)],
                     [jax.ShapeDtypeStruct((S, W), F32)] * 2, [_rb(tr, W)] * 2, "ssm_act")


def _merge(attn, g, z, b_glu, g_attn, g_ssm):
    S, W = attn.shape
    tr = _tile(S, 256)

    def body(a_ref, g_ref, z_ref, b_ref, ga_ref, gs_ref, o_ref):
        a = a_ref[...]
        ra = lax.rsqrt(jnp.mean(a * a, axis=-1, keepdims=True) + LN_EPS)
        o_ref[:, 0:W] = (a * ra * ga_ref[...]).astype(BF16)
        s = g_ref[...] / (1.0 + jnp.exp(-(z_ref[...] + b_ref[...])))
        rs = lax.rsqrt(jnp.mean(s * s, axis=-1, keepdims=True) + LN_EPS)
        o_ref[:, W:2 * W] = (s * rs * gs_ref[...]).astype(BF16)

    return _row_call(body, S, tr, [attn, g, z, b_glu, g_attn, g_ssm],
                     [_rb(tr, W)] * 3 + [_pb(1, W)] * 3,
                     jax.ShapeDtypeStruct((S, 2 * W), BF16), _rb(tr, 2 * W), "merge")


def _post_ln(x, branch, m6, row_gate, ln_g, ln_b, alpha, name):
    S, D = x.shape
    tr = _tile(S, 256)

    def body(x_ref, br_ref, m_ref, g_ref, b_ref, o_ref):
        gate = m_ref[row_gate:row_gate + 1, :]
        s = alpha * x_ref[...] + (1.0 + gate) * br_ref[...]
        mu = jnp.mean(s, axis=-1, keepdims=True)
        d = s - mu
        var = jnp.mean(d * d, axis=-1, keepdims=True)
        o_ref[...] = d * lax.rsqrt(var + LN_EPS) * g_ref[...] + b_ref[...]

    return _row_call(body, S, tr, [x, branch, m6, ln_g, ln_b],
                     [_rb(tr, D), _rb(tr, D), _pb(6, D), _pb(1, D), _pb(1, D)],
                     jax.ShapeDtypeStruct((S, D), F32), _rb(tr, D), name)


def _loss_head(x, target):
    S, D = x.shape
    tr = _tile(S, 256)

    def body(x_ref, t_ref, dx_ref, acc_ref):
        e = x_ref[...] - t_ref[...]
        dx_ref[...] = e * (1.0 / D)
        _accumulate(acc_ref, e * e)

    return _row_call(body, S, tr, [x, target], [_rb(tr, D)] * 2,
                     [jax.ShapeDtypeStruct((S, D), F32), jax.ShapeDtypeStruct((1, D), F32)],
                     [_rb(tr, D), _pb(1, D)], "loss_head")


def _ln_bwd(dxo, x_in, branch, m6, row_gate, ln_g, alpha, name):
    S, D = dxo.shape
    tr = _tile(S, 256)

    def body(dxo_ref, x_ref, br_ref, m_ref, g_ref, ds_ref, dbr_ref, dg_ref, db_ref, dgate_ref):
        gate = m_ref[row_gate:row_gate + 1, :]
        br = br_ref[...]
        s = alpha * x_ref[...] + (1.0 + gate) * br
        mu = jnp.mean(s, axis=-1, keepdims=True)
        d = s - mu
        var = jnp.mean(d * d, axis=-1, keepdims=True)
        rstd = lax.rsqrt(var + LN_EPS)
        xhat = d * rstd
        dxo = dxo_ref[...]
        dxh = dxo * g_ref[...]
        ds = rstd * (dxh - jnp.mean(dxh, axis=-1, keepdims=True)
                     - xhat * jnp.mean(dxh * xhat, axis=-1, keepdims=True))
        ds_ref[...] = ds
        dbr_ref[...] = ((1.0 + gate) * ds).astype(BF16)
        _accumulate(dg_ref, dxo * xhat)
        _accumulate(db_ref, dxo)
        _accumulate(dgate_ref, ds * br)

    vec = jax.ShapeDtypeStruct((1, D), F32)
    return _row_call(body, S, tr, [dxo, x_in, branch, m6, ln_g],
                     [_rb(tr, D)] * 3 + [_pb(6, D), _pb(1, D)],
                     [jax.ShapeDtypeStruct((S, D), F32), jax.ShapeDtypeStruct((S, D), BF16), vec, vec, vec],
                     [_rb(tr, D), _rb(tr, D), _pb(1, D), _pb(1, D), _pb(1, D)], name)


def _mod_bwd(ds, dh, x_in, m6, row_scale, alpha, name):
    S, D = ds.shape
    tr = _tile(S, 256)

    def body(ds_ref, dh_ref, x_ref, m_ref, dx_ref, dsc_ref, dsh_ref):
        sc = m_ref[row_scale:row_scale + 1, :]
        dh = dh_ref[...]
        dx_ref[...] = alpha * ds_ref[...] + dh * (1.0 + sc)
        _accumulate(dsc_ref, dh * x_ref[...])
        _accumulate(dsh_ref, dh)

    vec = jax.ShapeDtypeStruct((1, D), F32)
    return _row_call(body, S, tr, [ds, dh, x_in, m6], [_rb(tr, D)] * 3 + [_pb(6, D)],
                     [jax.ShapeDtypeStruct((S, D), F32), vec, vec],
                     [_rb(tr, D), _pb(1, D), _pb(1, D)], name)


def _merge_bwd(dmerged, attn, g, z, b_glu, g_attn, g_ssm):
    S, W = attn.shape
    tr = _tile(S, 256)

    def rms_bwd(x, gamma, dy):
        r = lax.rsqrt(jnp.mean(x * x, axis=-1, keepdims=True) + LN_EPS)
        gdy = gamma * dy
        dx = gdy * r - x * (r * r * r) * jnp.mean(gdy * x, axis=-1, keepdims=True)
        return dx, dy * x * r

    def body(dm_ref, a_ref, g_ref, z_ref, b_ref, ga_ref, gs_ref, da_ref, dgp_ref, dz_ref, dga_ref, dgs_ref, db_ref):
        da, dga = rms_bwd(a_ref[...], ga_ref[...], dm_ref[:, 0:W])
        da_ref[...] = da
        _accumulate(dga_ref, dga)
        gv = g_ref[...]
        sig = 1.0 / (1.0 + jnp.exp(-(z_ref[...] + b_ref[...])))
        dssm, dgs = rms_bwd(gv * sig, gs_ref[...], dm_ref[:, W:2 * W])
        _accumulate(dgs_ref, dgs)
        dgp_ref[...] = dssm * sig
        dz = dssm * gv * sig * (1.0 - sig)
        dz_ref[...] = dz.astype(BF16)
        _accumulate(db_ref, dz)

    vec = jax.ShapeDtypeStruct((1, W), F32)
    mat = jax.ShapeDtypeStruct((S, W), F32)
    return _row_call(body, S, tr, [dmerged, attn, g, z, b_glu, g_attn, g_ssm],
                     [_rb(tr, 2 * W)] + [_rb(tr, W)] * 3 + [_pb(1, W)] * 3,
                     [mat, mat, jax.ShapeDtypeStruct((S, W), BF16), vec, vec, vec],
                     [_rb(tr, W)] * 3 + [_pb(1, W)] * 3, "merge_bwd")


def _act_bwd(dgp, dgl, pre, u, dskip):
    S, W = pre.shape
    tr = _tile(S, 256)

    def body(a_ref, b_ref, pre_ref, u_ref, d_ref, dy_ref, du_ref, dd_ref):
        dy = (a_ref[...] + b_ref[...]) * _gelu_grad(pre_ref[...])
        dy_ref[...] = dy
        du_ref[...] = dy * d_ref[...]
        _accumulate(dd_ref, dy * u_ref[...])

    mat = jax.ShapeDtypeStruct((S, W), F32)
    return _row_call(body, S, tr, [dgp, dgl, pre, u, dskip], [_rb(tr, W)] * 4 + [_pb(1, W)],
                     [mat, mat, jax.ShapeDtypeStruct((1, W), F32)], [_rb(tr, W)] * 2 + [_pb(1, W)], "act_bwd")


def _dproj(dq, dkT, dvT, du_skip, du0, du1, reach):
    S, W = dq.shape
    tr = _tile(S, 256)
    lanes = 2 * HEAD_DIM
    n_pairs, nkb = W // lanes, tr // lanes
    assert tr % lanes == 0 and reach % tr == 0

    def body(dq_ref, dk_ref, dv_ref, a_ref, b_ref, c_ref, o_ref):
        o_ref[0] = dq_ref[...].astype(BF16)
        for h in range(n_pairs):
            for b in range(nkb):
                rows, cols = slice(b * lanes, (b + 1) * lanes), slice(h * lanes, (h + 1) * lanes)
                o_ref[1, rows, cols] = dk_ref[h, b].T.astype(BF16)
                o_ref[2, rows, cols] = dv_ref[h, b].T.astype(BF16)
        o_ref[3] = (a_ref[...] + b_ref[...] + c_ref[...]).astype(BF16)

    tsp = pl.BlockSpec((n_pairs, nkb, lanes, lanes), lambda i: (0, i + reach // tr, 0, 0))
    return _row_call(body, S, tr, [dq, dkT, dvT, du_skip, du0, du1],
                     [_rb(tr, W), tsp, tsp, _rb(tr, W), _rb(tr, W), _rb(tr, W)],
                     jax.ShapeDtypeStruct((4, S, W), BF16), pl.BlockSpec((4, tr, W), lambda i: (0, i, 0)), "dproj")


def _attn_reach():
    return max(w // 2 for w, _ in ATTN_PATTERNS)


def _bias_table(tq, width, reach, head):
    i = lax.broadcasted_iota(jnp.int32, (tq, width), 0)
    j = lax.broadcasted_iota(jnp.int32, (tq, width), 1)
    ad = jnp.abs(j - i - reach)
    mult = jnp.zeros((tq, width), jnp.int32)
    for window, dil in ATTN_PATTERNS:
        assert dil & (dil - 1) == 0
        mult += ((jnp.bitwise_and(ad, dil - 1) == 0) & (ad <= window // 2)).astype(jnp.int32)
    logm = jnp.zeros((tq, width), F32)
    for n in range(2, len(ATTN_PATTERNS) + 1):
        logm = jnp.where(mult == n, math.log(n), logm)
    return logm, ad.astype(F32), mult > 0


def _bias_tables(n_heads, tq):
    reach = _attn_reach()
    width = tq + 2 * reach

    def body(o_ref):
        logm, ad, ok = _bias_table(tq, width, reach, None)
        head = (pl.program_id(0) + 1).astype(F32)
        slope = jnp.exp(jnp.full((tq, width), -8.0 * math.log(2.0) / n_heads, F32) * head)
        o_ref[...] = jnp.where(ok, logm - slope * ad, NEG_BIG)

    return pl.pallas_call(body, grid=(n_heads,), out_specs=pl.BlockSpec((None, tq, width), lambda h: (h, 0, 0)),
                          out_shape=jax.ShapeDtypeStruct((n_heads, tq, width), F32), compiler_params=_params(1),
                          name="attn_bias")()


def _attn_setup(tables_ref, bias_ref, sem, tq):
    hp, qi = pl.program_id(0), pl.program_id(1)

    @pl.when(qi == 0)
    def _():
        cp = pltpu.make_async_copy(tables_ref.at[pl.ds(2 * hp, 2)], bias_ref, sem)
        cp.start()
        cp.wait()

    lane = lax.broadcasted_iota(jnp.int32, (1, 2 * HEAD_DIM), 1)
    return pl.multiple_of(qi * tq, tq), [lane < HEAD_DIM, lane >= HEAD_DIM]


def _key_validity(first_key, n, seq):
    kpos = first_key + lax.broadcasted_iota(jnp.int32, (1, n), 1)
    return jnp.where((kpos >= 0) & (kpos < seq), 0.0, NEG_BIG).astype(F32)


_NT = (((1,), (1,)), ((), ()))
assert math.log2(HEAD_DIM) % 2 == 0


def _carrier_call(body, grid, in_specs, out_specs, out_shape, scratch, name, ins, riders):
    if riders:
        first = lambda: (pl.program_id(0) == 0) & (pl.program_id(1) == 0)
        last = lambda: (pl.program_id(0) == grid[0] - 1) & (pl.program_id(1) == grid[1] - 1)
        body = _hosted(body, len(in_specs), len(out_specs), len(scratch), riders, first, last)
        name += "_carrier"
    rider_ins = [a for t in riders for a in t.ins]
    res = pl.pallas_call(
        body, grid=grid, in_specs=in_specs + [ANY] * len(rider_ins), out_specs=out_specs + [ANY] * len(riders),
        out_shape=out_shape + [t.out_shape for t in riders],
        scratch_shapes=scratch + [s for t in riders for s in t.semaphores()],
        compiler_params=_params(2), name=name)(*ins, *rider_ins)
    return res[:len(out_specs)], res[len(out_specs):]


def _attention_fwd(qkv, kp, vp, tables, tq, riders=()):
    _, S, W = qkv.shape
    reach = _attn_reach()
    width = tq + 2 * reach
    Sp = S + 2 * reach
    scale = HEAD_DIM ** -0.5
    lanes = 2 * HEAD_DIM

    kc = tq
    assert width % kc == 0

    def body(q_ref, k_ref, v_ref, tables_ref, o_ref, lse_ref, bias_ref, sem):
        start, masks = _attn_setup(tables_ref, bias_ref, sem, tq)
        qi = pl.program_id(1)
        q = q_ref[...] * scale
        out = jnp.zeros((tq, lanes), F32)
        lse = jnp.zeros((tq, lanes), F32)
        for a in range(2):
            qa = jnp.where(masks[a], q, jnp.zeros_like(q))
            m = jnp.full((tq, 1), 3.0 * NEG_BIG, F32)
            l = jnp.zeros((tq, 1), F32)
            acc = jnp.zeros((tq, lanes), F32)
            for c in range(width // kc):
                kw = k_ref[pl.ds(start + c * kc, kc), :]
                vw = v_ref[pl.ds(start + c * kc, kc), :]
                s = (lax.dot_general(qa, kw, _NT, preferred_element_type=F32) + bias_ref[a, :, c * kc:(c + 1) * kc]
                     + _key_validity(qi * tq - reach + c * kc, kc, S))
                m_new = jnp.maximum(m, jnp.max(s, axis=-1, keepdims=True))
                alpha = jnp.exp(m - m_new)
                p = jnp.exp(s - m_new)
                l = alpha * l + jnp.sum(p, axis=-1, keepdims=True)
                acc = alpha * acc + jnp.dot(p.astype(BF16), vw, preferred_element_type=F32)
                m = m_new
            out = jnp.where(masks[a], acc / l, out)
            lse = jnp.where(masks[a], m + jnp.log(l), lse)
        o_ref[...] = out
        lse_ref[...] = lse

    qsp = pl.BlockSpec((None, tq, lanes), lambda h, i: (0, i, h))
    ksp = pl.BlockSpec((Sp, lanes), lambda h, i: (0, h))
    osp = pl.BlockSpec((tq, lanes), lambda h, i: (i, h))
    mat = jax.ShapeDtypeStruct((S, W), F32)
    return _carrier_call(body, (W // lanes, S // tq), [qsp, ksp, ksp, ANY], [osp, osp], [mat, mat],
                           [pltpu.VMEM((2, tq, width), F32), pltpu.SemaphoreType.DMA], "attn_fwd",
                           (qkv, kp, vp, tables), list(riders))


def _attention_bwd(qkv, kp, vp, out, lse, dout, tables, tq, riders=()):
    _, S, W = qkv.shape
    reach = _attn_reach()
    width = tq + 2 * reach
    Sp = S + 2 * reach
    scale = HEAD_DIM ** -0.5
    lanes = 2 * HEAD_DIM
    kc = tq
    assert width % kc == 0 and kc % lanes == 0

    def body(q_ref, k_ref, v_ref, tables_ref, o_ref, lse_ref, do_ref, dq_ref, dk_ref, dv_ref, bias_ref, sem):
        start, masks = _attn_setup(tables_ref, bias_ref, sem, tq)
        qi = pl.program_id(1)

        @pl.when(qi == 0)
        def _():
            dk_ref[...] = jnp.zeros_like(dk_ref)
            dv_ref[...] = jnp.zeros_like(dv_ref)

        q = q_ref[...] * scale
        do = do_ref[...]
        prod = do * o_ref[...]
        lse_all = lse_ref[...]
        dq = jnp.zeros((tq, lanes), F32)
        for a in range(2):
            qa = jnp.where(masks[a], q, jnp.zeros_like(q))
            doa = jnp.where(masks[a], do, 0.0).astype(BF16)
            qat = qa.astype(F32).T.astype(BF16)
            doat = jnp.where(masks[a], do, 0.0).T.astype(BF16)
            lse_a = lse_all[:, a * HEAD_DIM:a * HEAD_DIM + 1]
            delta = jnp.sum(jnp.where(masks[a], prod, 0.0), axis=-1, keepdims=True)
            dq_a = jnp.zeros((tq, lanes), F32)
            for c in range(width // kc):
                kw = k_ref[pl.ds(start + c * kc, kc), :]
                vw = v_ref[pl.ds(start + c * kc, kc), :]
                s = (lax.dot_general(qa, kw, _NT, preferred_element_type=F32) + bias_ref[a, :, c * kc:(c + 1) * kc]
                     + _key_validity(qi * tq - reach + c * kc, kc, S))
                p = jnp.exp(s - lse_a)
                dp = lax.dot_general(doa, vw, _NT, preferred_element_type=F32)
                ds = (p * (dp - delta)).astype(BF16)
                dvt = jnp.dot(doat, p.astype(BF16), preferred_element_type=F32)
                dkt = jnp.dot(qat, ds, preferred_element_type=F32)
                dq_a += jnp.dot(ds, kw, preferred_element_type=F32)
                first = qi * (tq // lanes) + c * (kc // lanes)
                for j in range(kc // lanes):
                    dk_ref[first + j] += dkt[:, j * lanes:(j + 1) * lanes]
                    dv_ref[first + j] += dvt[:, j * lanes:(j + 1) * lanes]
            dq = jnp.where(masks[a], dq_a * scale, dq)
        dq_ref[...] = dq

    assert tq % lanes == 0 and reach % lanes == 0
    qsp = pl.BlockSpec((None, tq, lanes), lambda h, i: (0, i, h))
    ksp = pl.BlockSpec((Sp, lanes), lambda h, i: (0, h))
    osp = pl.BlockSpec((tq, lanes), lambda h, i: (i, h))
    tsp = pl.BlockSpec((None, Sp // lanes, lanes, lanes), lambda h, i: (h, 0, 0, 0))
    tiles = jax.ShapeDtypeStruct((W // lanes, Sp // lanes, lanes, lanes), F32)
    return _carrier_call(body, (W // lanes, S // tq), [qsp, ksp, ksp, ANY, osp, osp, osp], [osp, tsp, tsp],
                           [jax.ShapeDtypeStruct((S, W), F32), tiles, tiles],
                           [pltpu.VMEM((2, tq, width), F32), pltpu.SemaphoreType.DMA], "attn_bwd",
                           (qkv, kp, vp, tables, out, lse, dout), list(riders))


SB = GROUPS_PER_BLOCK * SSM_STATE
CB = GROUPS_PER_BLOCK * SSM_GROUP
RUNS = 8


def _powers(pw, a_r, a_i, n):
    p_r, p_i = a_r, a_i
    for j in range(n):
        pw[0, j:j + 1, :] = p_r
        pw[1, j:j + 1, :] = p_i
        p_r, p_i = p_r * a_r - p_i * a_i, p_r * a_i + p_i * a_r


def _rows8(j):
    return slice(j * RUNS, (j + 1) * RUNS)


def _run_scan(br, bi, T, a_r, a_i, pw, e_r, e_i, ent, down, conj):
    n = T // RUNS
    sg = -1.0 if conj else 1.0
    A_r = jnp.broadcast_to(a_r, (RUNS, SB))
    A_i = jnp.broadcast_to(sg * a_i, (RUNS, SB))
    x_r = x_i = None
    for j in (range(n - 1, -1, -1) if down else range(n)):
        b_r, b_i = br[_rows8(j), :], bi[_rows8(j), :]
        if x_r is None:
            x_r, x_i = b_r, b_i
        else:
            x_r, x_i = A_r * x_r - A_i * x_i + b_r, A_r * x_i + A_i * x_r + b_i
            br[_rows8(j), :] = x_r
            bi[_rows8(j), :] = x_i
    al_r, al_i = pw[0, n - 1:n, :], sg * pw[1, n - 1:n, :]
    s_r, s_i = e_r, e_i
    for c in (range(RUNS - 1, -1, -1) if down else range(RUNS)):
        ent[0, c:c + 1, :] = s_r
        ent[1, c:c + 1, :] = s_i
        s_r, s_i = (al_r * s_r - al_i * s_i + x_r[c:c + 1, :], al_r * s_i + al_i * s_r + x_i[c:c + 1, :])
    E_r, E_i = ent[0], ent[1]
    for j in range(n):
        p = n - 1 - j if down else j
        p_r, p_i = pw[0, p:p + 1, :], sg * pw[1, p:p + 1, :]
        v_r, v_i = br[_rows8(j), :], bi[_rows8(j), :]
        br[_rows8(j), :] = v_r + p_r * E_r - p_i * E_i
        bi[_rows8(j), :] = v_i + p_r * E_i + p_i * E_r
    return s_r, s_i


def _load_run_major(ref, T):
    n = T // RUNS
    return jnp.concatenate([ref[pl.ds(j, RUNS, stride=n), :] for j in range(n)], axis=0)


def _store_row_major(ref, value, T):
    n = T // RUNS
    for j in range(n):
        ref[pl.ds(j, RUNS, stride=n), :] = value[_rows8(j), :]


def _s5_specs(nb, T, rev, adjoint):
    flip = rev != adjoint
    blk = (lambda i: nb - 1 - i) if flip else (lambda i: i)
    usp = pl.BlockSpec((T, CB), lambda k, i: (blk(i), k))
    asp = pl.BlockSpec((None, 2, SB), lambda k, i: (k, 0, 0))
    wsp = pl.BlockSpec((None, CB, SB), lambda k, i: (k, 0, 0))
    csp = pl.BlockSpec((None, None, 2, SB), lambda k, i: (k, blk(i), 0, 0))
    return usp, asp, wsp, csp


def _s5_fwd(u, a, b_re, b_im, c_re, c_im, T, rev):
    S, W = u.shape
    NK = W // CB
    nb = S // T
    usp, asp, wsp, csp = _s5_specs(nb, T, rev, False)

    def body(u_ref, a_ref, bre_ref, bim_ref, cre_ref, cim_ref, y_ref, car_ref, wr, wi, st, pw, ent):
        a_r, a_i = a_ref[0:1, :], a_ref[1:2, :]

        @pl.when(pl.program_id(1) == 0)
        def _():
            st[...] = jnp.zeros_like(st)
            _powers(pw, a_r, a_i, T // RUNS)

        car_ref[...] = st[0:2, :]
        ub = _load_run_major(u_ref, T).astype(BF16)
        wr[...] = jnp.dot(ub, bre_ref[...], preferred_element_type=F32)
        wi[...] = jnp.dot(ub, bim_ref[...], preferred_element_type=F32)
        s_r, s_i = _run_scan(wr, wi, T, a_r, a_i, pw, st[0:1, :], st[1:2, :], ent, rev, False)
        st[0:1, :] = s_r
        st[1:2, :] = s_i
        y = (lax.dot_general(wr[...].astype(BF16), cre_ref[...], _NT, preferred_element_type=F32)
             - lax.dot_general(wi[...].astype(BF16), cim_ref[...], _NT, preferred_element_type=F32))
        _store_row_major(y_ref, y, T)

    return pl.pallas_call(
        body, grid=(NK, nb), in_specs=[usp, asp, wsp, wsp, wsp, wsp], out_specs=[usp, csp],
        out_shape=[jax.ShapeDtypeStruct((S, W), F32), jax.ShapeDtypeStruct((NK, nb, 2, SB), F32)],
        scratch_shapes=[pltpu.VMEM((T, SB), F32), pltpu.VMEM((T, SB), F32), pltpu.VMEM((8, SB), F32),
                        pltpu.VMEM((2, T // RUNS, SB), F32), pltpu.VMEM((2, RUNS, SB), F32)],
        compiler_params=_params(2), name="s5_fwd_rev" if rev else "s5_fwd")(u, a, b_re, b_im, c_re, c_im)


def _s5_bwd(u, dy, a, b_re, b_im, c_re, c_im, carries, T, rev, riders=()):
    S, W = u.shape
    NK = W // CB
    nb = S // T
    usp, asp, wsp, csp = _s5_specs(nb, T, rev, True)
    n = T // RUNS

    def body(u_ref, dy_ref, a_ref, bre_ref, bim_ref, cre_ref, cim_ref, car_ref,
             du_ref, dbre_ref, dbim_ref, dcre_ref, dcim_ref, da_ref, wr, wi, gr, gi, lam, pw, ent):
        a_r, a_i = a_ref[0:1, :], a_ref[1:2, :]

        @pl.when(pl.program_id(1) == 0)
        def _():
            lam[...] = jnp.zeros_like(lam)
            for r in (dbre_ref, dbim_ref, dcre_ref, dcim_ref, da_ref):
                r[...] = jnp.zeros_like(r)
            _powers(pw, a_r, a_i, n)

        u32 = _load_run_major(u_ref, T)
        ub = u32.astype(BF16)
        dyk = _load_run_major(dy_ref, T)
        dyb = dyk.astype(BF16)
        wr[...] = jnp.dot(ub, bre_ref[...], preferred_element_type=F32)
        wi[...] = jnp.dot(ub, bim_ref[...], preferred_element_type=F32)
        x0r, x0i = car_ref[0:1, :], car_ref[1:2, :]
        _run_scan(wr, wi, T, a_r, a_i, pw, x0r, x0i, ent, rev, False)

        gr[...] = jnp.dot(dyb, cre_ref[...], preferred_element_type=F32)
        gi[...] = -jnp.dot(dyb, cim_ref[...], preferred_element_type=F32)
        l_r, l_i = _run_scan(gr, gi, T, a_r, a_i, pw, lam[0:1, :], lam[1:2, :], ent, not rev, True)
        lam[0:1, :] = l_r
        lam[1:2, :] = l_i

        sub = lax.broadcasted_iota(jnp.int32, (RUNS, SB), 0)

        def before(buf, x0, j):
            if rev:
                if j < n - 1:
                    return buf[_rows8(j + 1), :]
                return jnp.where(sub == RUNS - 1, x0, pltpu.roll(buf[_rows8(0), :], RUNS - 1, 0))
            if j > 0:
                return buf[_rows8(j - 1), :]
            return jnp.where(sub == 0, x0, pltpu.roll(buf[_rows8(n - 1), :], 1, 0))

        acc_r = jnp.zeros((RUNS, SB), F32)
        acc_i = jnp.zeros((RUNS, SB), F32)
        for j in range(n):
            g_r, g_i = gr[_rows8(j), :], gi[_rows8(j), :]
            p_r, p_i = before(wr, x0r, j), before(wi, x0i, j)
            acc_r += g_r * p_r + g_i * p_i
            acc_i += g_i * p_r - g_r * p_i
        da_ref[0:1, :] += jnp.sum(acc_r, axis=0, keepdims=True)
        da_ref[1:2, :] += jnp.sum(acc_i, axis=0, keepdims=True)

        lrb, lib = gr[...].astype(BF16), gi[...].astype(BF16)
        du = (lax.dot_general(lrb, bre_ref[...], _NT, preferred_element_type=F32)
              + lax.dot_general(lib, bim_ref[...], _NT, preferred_element_type=F32))
        _store_row_major(du_ref, du, T)
        ut = u32.T.astype(BF16)
        dbre_ref[...] += jnp.dot(ut, lrb, preferred_element_type=F32)
        dbim_ref[...] += jnp.dot(ut, lib, preferred_element_type=F32)
        dyt = dyk.T.astype(BF16)
        dcre_ref[...] += jnp.dot(dyt, wr[...].astype(BF16), preferred_element_type=F32)
        dcim_ref[...] += jnp.dot(dyt, wi[...].astype(BF16), preferred_element_type=F32)

    blk = jax.ShapeDtypeStruct((NK, CB, SB), F32)
    return _carrier_call(
        body, (NK, nb), [usp, usp, asp, wsp, wsp, wsp, wsp, csp], [usp, wsp, wsp, wsp, wsp, asp],
        [jax.ShapeDtypeStruct((S, W), F32), blk, blk, blk, blk, jax.ShapeDtypeStruct((NK, 2, SB), F32)],
        [pltpu.VMEM((T, SB), F32), pltpu.VMEM((T, SB), F32), pltpu.VMEM((T, SB), F32), pltpu.VMEM((T, SB), F32),
         pltpu.VMEM((8, SB), F32), pltpu.VMEM((2, n, SB), F32), pltpu.VMEM((2, RUNS, SB), F32)],
        "s5_bwd_rev" if rev else "s5_bwd", (u, dy, a, b_re, b_im, c_re, c_im, carries), list(riders))


def _s5_discretize(lam_re, lam_im, log_step, b_re, b_im):
    step = jnp.exp(log_step)[..., None]
    mag = jnp.exp(lam_re * step)
    a_re, a_im = mag * jnp.cos(lam_im * step), mag * jnp.sin(lam_im * step)
    den = lam_re * lam_re + lam_im * lam_im
    coef_re = ((a_re - 1.0) * lam_re + a_im * lam_im) / den
    coef_im = (a_im * lam_re - (a_re - 1.0) * lam_im) / den
    bb_re = coef_re[..., None] * b_re - coef_im[..., None] * b_im
    bb_im = coef_re[..., None] * b_im + coef_im[..., None] * b_re
    return a_re, a_im, bb_re, bb_im


def _to_blocks(w_gcp):
    two, G, C, P = w_gcp.shape
    nk = G // GROUPS_PER_BLOCK
    x = w_gcp.reshape(two, nk, GROUPS_PER_BLOCK, C, P)
    eye = jnp.eye(GROUPS_PER_BLOCK, dtype=w_gcp.dtype)
    return jnp.einsum('dkgcp,gh->dkgchp', x, eye).reshape(two, nk, GROUPS_PER_BLOCK * C, GROUPS_PER_BLOCK * P)


def _from_blocks(blk):
    two, nk, cb, sb = blk.shape
    C, P = cb // GROUPS_PER_BLOCK, sb // GROUPS_PER_BLOCK
    x = blk.reshape(two, nk, GROUPS_PER_BLOCK, C, GROUPS_PER_BLOCK, P)
    eye = jnp.eye(GROUPS_PER_BLOCK, dtype=blk.dtype)
    return jnp.einsum('dkgchp,gh->dkgcp', x, eye).reshape(two, nk * GROUPS_PER_BLOCK, C, P)


def _me():
    return lax.axis_index("x"), lax.axis_index("y"), lax.axis_index("c")


def _peer(k):
    x, y, c = _me()
    return (1 - x if k & 4 else x, 1 - y if k & 2 else y, 1 - c if k & 1 else c)


def _logical(dev):
    return 4 * dev[0] + 2 * dev[1] + dev[2]


class _Transfer:
    def __init__(self, ks, src_of, dst_of, out_shape, ins, split=1):
        self.ks, self.src_of, self.dst_of, self.out_shape, self.ins, self.split = ks, src_of, dst_of, out_shape, ins, split

    def semaphores(self):
        n = len(self.ks) * self.split
        return [pltpu.SemaphoreType.DMA((n,)), pltpu.SemaphoreType.DMA((n,))]

    def copies(self, in_refs, out_ref, send, recv):
        out = []
        for k in self.ks:
            src, dst = self.src_of(k, in_refs, out_ref), self.dst_of(k, in_refs, out_ref)
            rows = src.shape[0] // self.split
            for q in range(self.split):
                part = pl.ds(q * rows, rows)
                j = len(out)
                out.append(pltpu.make_async_remote_copy(
                    src_ref=src if self.split == 1 else src.at[part],
                    dst_ref=dst if self.split == 1 else dst.at[part],
                    send_sem=send.at[j], recv_sem=recv.at[j], device_id=_peer(k), device_id_type=MESH))
        return out


def _exchange(name, t):
    n_in = len(t.ins)

    def body(*refs):
        copies = t.copies(refs[:n_in], refs[n_in], *refs[n_in + 1:])
        for cp in copies:
            cp.start()
        for cp in copies:
            cp.wait_recv()
        for cp in copies:
            cp.wait_send()

    return pl.pallas_call(body, in_specs=[ANY] * n_in, out_specs=ANY, out_shape=t.out_shape,
                          scratch_shapes=t.semaphores(), name=name)(*t.ins)


def _hosted(body, n_in, n_out, n_scratch, riders, is_first, is_last):
    n_rin = sum(len(t.ins) for t in riders)

    def wrapped(*refs):
        host_in, rider_in = refs[:n_in], refs[n_in:n_in + n_rin]
        pos = n_in + n_rin
        host_out, rider_out = refs[pos:pos + n_out], refs[pos + n_out:pos + n_out + len(riders)]
        pos += n_out + len(riders)
        host_scratch, sems = refs[pos:pos + n_scratch], refs[pos + n_scratch:]

        def copies():
            out, at = [], 0
            for i, t in enumerate(riders):
                out += t.copies(rider_in[at:at + len(t.ins)], rider_out[i], sems[2 * i], sems[2 * i + 1])
                at += len(t.ins)
            return out

        @pl.when(is_first())
        def _():
            for cp in copies():
                cp.start()

        body(*host_in, *host_out, *host_scratch)

        @pl.when(is_last())
        def _():
            for cp in copies():
                cp.wait_recv()
            for cp in copies():
                cp.wait_send()

    return wrapped


def _gather8_transfer(v):
    out = jax.ShapeDtypeStruct((N_DEV,) + v.shape, v.dtype)
    slot = lambda k, ins, o: o.at[_logical(_me())]
    return _Transfer(list(range(1, 8)), lambda k, ins, o: ins[0], slot, out, [v])


def _all_gather8(v, name):
    got = _exchange(name, _gather8_transfer(v))
    return lax.dynamic_update_index_in_dim(got, v, _logical(_me()), 0)


def _halves_transfer(v):
    R, C = v.shape
    half = lambda ins: ins[0].at[pl.ds(lax.axis_index("c") * (R // 2), R // 2)]
    chip = lambda: 2 * lax.axis_index("x") + lax.axis_index("y")
    return _Transfer([2, 4, 6], lambda k, ins, o: half(ins), lambda k, ins, o: o.at[chip()],
                     jax.ShapeDtypeStruct((4, R // 2, C), v.dtype), [v])


def _join_halves(v, halves, name):
    R, C = v.shape
    c = lax.axis_index("c")
    mine = lax.dynamic_slice_in_dim(v, c * (R // 2), R // 2, axis=0)
    halves = lax.dynamic_update_index_in_dim(halves, mine, 2 * lax.axis_index("x") + lax.axis_index("y"), 0)
    core = lambda o: o.at[:, lax.axis_index("c")]
    both = _exchange(name + "_join", _Transfer(
        [1], lambda k, ins, o: ins[0], lambda k, ins, o: core(o),
        jax.ShapeDtypeStruct((4, 2, R // 2, C), v.dtype), [halves], split=4))
    return lax.dynamic_update_index_in_dim(both, halves, c, 1).reshape(4, R, C)


def _scatter_transfer(g):
    four, two, R2, C = g.shape
    piece = lambda dev, ins: ins[0].at[2 * dev[0] + dev[1], dev[2]]
    slot = lambda k, ins, o: o.at[_logical(_me())]
    return _Transfer(list(range(1, 8)), lambda k, ins, o: piece(_peer(k), ins), slot,
                     jax.ShapeDtypeStruct((N_DEV, R2, C), g.dtype), [g])


def _reduce_scattered(g, landed, name):
    x, y, c = _me()
    mine = lax.dynamic_index_in_dim(lax.dynamic_index_in_dim(g, 2 * x + y, 0, False), c, 0, False)
    landed = lax.dynamic_update_index_in_dim(landed, mine, _logical(_me()), 0)
    part = _sum_leading(landed, "reduce_" + name)
    return _swap_c(part, "swap_" + name).reshape(2 * part.shape[0], part.shape[1])


def _swap_c(v, name):
    out = jax.ShapeDtypeStruct((2,) + v.shape, v.dtype)
    slot = lambda: lax.axis_index("c")
    got = _exchange(name, _Transfer([1], lambda k, ins, o: ins[0], lambda k, ins, o: o.at[slot()], out, [v],
                                    split=8))
    return lax.dynamic_update_index_in_dim(got, v, slot(), 0)


def _sum_leading(v, name):
    n, R, C = v.shape
    tr = _tile(R, max(8, (1 << 19) // (C * n)))

    def body(v_ref, o_ref):
        acc = v_ref[0].astype(F32)
        for s in range(1, n):
            acc = acc + v_ref[s].astype(F32)
        o_ref[...] = acc

    return pl.pallas_call(body, grid=(R // tr,), in_specs=[pl.BlockSpec((n, tr, C), lambda i: (0, i, 0))],
                          out_specs=pl.BlockSpec((tr, C), lambda i: (i, 0)),
                          out_shape=jax.ShapeDtypeStruct((R, C), F32), compiler_params=_params(1), name=name)(v)


def _adamw(w, g, m, v, name):
    R, C = w.shape
    tr = _tile(R, max(8, (1 << 18) // C))
    c1 = 1.0 - ADAM_B1 ** ADAM_STEP
    c2 = 1.0 - ADAM_B2 ** ADAM_STEP

    def body(w_ref, g_ref, m_ref, v_ref, d_ref, nm_ref, nv_ref):
        gv = g_ref[...]
        nm = ADAM_B1 * m_ref[...] + (1.0 - ADAM_B1) * gv
        nv = ADAM_B2 * v_ref[...] + (1.0 - ADAM_B2) * (gv * gv)
        nm_ref[...] = nm
        nv_ref[...] = nv
        d_ref[...] = -ADAM_LR * ((nm / c1) / (jnp.sqrt(nv / c2) + ADAM_EPS) + ADAM_WD * w_ref[...])

    sp = pl.BlockSpec((tr, C), lambda i: (i, 0))
    return pl.pallas_call(body, grid=(R // tr,), in_specs=[sp] * 4, out_specs=[sp] * 3,
                          out_shape=[jax.ShapeDtypeStruct((R, C), F32)] * 3, compiler_params=_params(1),
                          name=name)(w, g, m, v)


def _forward_layer(x, lw, w_in_full, shards, next_w_in, alpha, tables, tq, ts):
    S, D = x.shape
    W = D // 2
    reach = _attn_reach()
    m6 = lw['m6']
    gw = {'w_in': w_in_full}
    FF = shards['w_mlp1'].shape[1] * 4

    h = _modulate(x, m6, 0, 1, "modulate1")
    qkv = _mm(h, gw['w_in'], M=S, N=3 * W, K=D, b_st='n', o_st='n', ns=3, out_dtype=BF16, cap_k=2048, name="proj_qkv")
    u = _mm(h, gw['w_in'][3], M=S, N=W, K=D, out_dtype=F32, cap_k=2048, name="proj_u")
    pad = ((reach, reach), (0, 0))
    kp, vp = jnp.pad(qkv[1], pad), jnp.pad(qkv[2], pad)
    carried = [(n, shards[n]) for n in BIG[1:]] + ([('w_in', next_w_in)] if next_w_in is not None else [])
    (attn, lse), halves = _attention_fwd(qkv, kp, vp, tables, tq, [_halves_transfer(v) for _, v in carried])
    joined = [_join_halves(v, hv, "gather_" + n) for (n, v), hv in zip(carried, halves)]
    gw.update({n: j for (n, _), j in zip(carried[:len(BIG) - 1], joined)})
    w_in_next = joined[-1] if next_w_in is not None else None
    y, carries = zip(*[_s5_fwd(u, lw['a'][d], lw['b_re'][d], lw['b_im'][d], lw['c_re'][d], lw['c_im'][d], ts,
                               rev=bool(d)) for d in range(2)])
    pre, g = _ssm_act(u, y[0], y[1], lw['ssm_d'])
    z = _mm(g, gw['w_glu'].reshape(W, W), M=S, N=W, K=W, out_dtype=F32, name="glu")
    merged = _merge(attn, g, z, lw['b_glu'], lw['g_attn'], lw['g_ssm'])
    mix = _mm(merged, gw['w_out'].reshape(D, D), M=S, N=D, K=D, out_dtype=F32, cap_k=2048, name="out_proj")
    x1 = _post_ln(x, mix, m6, 2, lw['ln1_g'], lw['ln1_b'], alpha, "post_ln1")
    h2 = _modulate(x1, m6, 3, 4, "modulate2")
    a4 = _mm(h2, gw['w_mlp1'], M=S, N=FF, K=D, b_st='n', o_st='n', ns=4, out_dtype=BF16, cap_k=2048, name="mlp1")
    ff = _mm(a4, gw['w_mlp2'], M=S, N=D, K=FF, a_st='k', b_st='k', ns=4, a_tf=_relu2, out_dtype=F32, name="mlp2")
    x2 = _post_ln(x1, ff, m6, 5, lw['ln2_g'], lw['ln2_b'], alpha, "post_ln2")
    saved = dict(x=x, h=h, qkv=qkv, u=u, attn=attn, lse=lse, carries=carries, pre=pre, g=g, z=z, merged=merged,
                 mix=mix, x1=x1, h2=h2, a4=a4, ff=ff, gw=gw)
    return x2, saved, w_in_next


SMALL_GRADS = ['a', 'bb_re', 'bb_im', 'cc_re', 'cc_im', 'ssm_d', 'b_glu', 'g_attn', 'g_ssm',
               'ln1_g', 'ln1_b', 'ln2_g', 'ln2_b']


def _as_halves(g):
    C = g.shape[-1]
    return g.reshape(4, 2, g.size // (8 * C), C)


def _backward_layer(dx2, lw, sv, pending_w_in, alpha, tables, tq, ts):
    S, D = dx2.shape
    W = D // 2
    reach = _attn_reach()
    m6, gw = lw['m6'], sv['gw']
    FF = gw['w_mlp1'].shape[2] * 4
    gr = {}

    ds2, dff, gr['ln2_g'], gr['ln2_b'], d_g2 = _ln_bwd(dx2, sv['x1'], sv['ff'], m6, 5, lw['ln2_g'], alpha, "ln2_bwd")
    da4 = _mm(dff, gw['w_mlp2'], M=S, N=FF, K=D, tb=True, b_st='n', o_st='n', ns=4, extra=sv['a4'],
              epi=lambda acc, a: 2.0 * jnp.maximum(a, 0.0) * acc, out_dtype=BF16, cap_k=2048, name="mlp2_bwd_x")
    gr['w_mlp2'] = _mm(sv['a4'], dff, M=FF, N=D, K=S, ta=True, a_st='m', o_st='m', ns=4, a_tf=_relu2,
                       out_dtype=BF16, name="mlp2_bwd_w")
    dh2 = _mm(da4, gw['w_mlp1'], M=S, N=D, K=FF, tb=True, a_st='k', b_st='k', ns=4, out_dtype=F32, name="mlp1_bwd_x")
    gr['w_mlp1'] = _mm(sv['h2'], da4, M=D, N=FF, K=S, ta=True, b_st='n', o_st='n', ns=4, out_dtype=BF16,
                       name="mlp1_bwd_w")
    dx1, d_sc2, d_sh2 = _mod_bwd(ds2, dh2, sv['x1'], m6, 4, alpha, "mod2_bwd")

    ds1, dmix, gr['ln1_g'], gr['ln1_b'], d_g1 = _ln_bwd(dx1, sv['x'], sv['mix'], m6, 2, lw['ln1_g'], alpha, "ln1_bwd")
    dmerged = _mm(dmix, gw['w_out'].reshape(D, D), M=S, N=D, K=D, tb=True, out_dtype=F32, cap_k=2048, name="out_proj_bwd_x")
    gr['w_out'] = _mm(sv['merged'], dmix, M=D, N=D, K=S, ta=True, out_dtype=BF16, name="out_proj_bwd_w")
    dattn, dgp, dz, gr['g_attn'], gr['g_ssm'], gr['b_glu'] = _merge_bwd(
        dmerged, sv['attn'], sv['g'], sv['z'], lw['b_glu'], lw['g_attn'], lw['g_ssm'])
    dgl = _mm(dz, gw['w_glu'].reshape(W, W), M=S, N=W, K=W, tb=True, out_dtype=F32, name="glu_bwd_x")
    gr['w_glu'] = _mm(sv['g'], dz, M=W, N=W, K=S, ta=True, out_dtype=BF16, name="glu_bwd_w")
    dy, du_skip, gr['ssm_d'] = _act_bwd(dgp, dgl, sv['pre'], sv['u'], lw['ssm_d'])
    mlp = [_as_halves(gr['w_mlp2']), _as_halves(gr['w_mlp1'])]
    s5 = [_s5_bwd(sv['u'], dy, lw['a'][d], lw['b_re'][d], lw['b_im'][d], lw['c_re'][d], lw['c_im'][d],
                  sv['carries'][d], ts, rev=bool(d), riders=[_scatter_transfer(mlp[d])]) for d in range(2)]
    gr['w_mlp2'] = _reduce_scattered(mlp[0], s5[0][1][0], 'w_mlp2')
    gr['w_mlp1'] = _reduce_scattered(mlp[1], s5[1][1][0], 'w_mlp1')
    du = [s5[d][0][0] for d in range(2)]
    for pos, n in enumerate(['bb_re', 'bb_im', 'cc_re', 'cc_im', 'a']):
        gr[n] = jnp.stack([s5[d][0][pos + 1] for d in range(2)])
    pad = ((reach, reach), (0, 0))
    kp, vp = jnp.pad(sv['qkv'][1], pad), jnp.pad(sv['qkv'][2], pad)
    for n in ['bb_re', 'bb_im', 'cc_re', 'cc_im']:
        gr[n] = _from_blocks(gr[n])
    small, _ = _pack([gr[n] for n in SMALL_GRADS], 1024)
    scattered = [(n, _as_halves(gr[n])) for n in ('w_glu', 'w_out')]
    if pending_w_in is not None:
        scattered.append(('w_in', pending_w_in))
    riders = [_scatter_transfer(g) for _, g in scattered] + [_gather8_transfer(small)]
    (dq, dkT, dvT), landed = _attention_bwd(sv['qkv'], kp, vp, sv['attn'], sv['lse'], dattn, tables, tq, riders)
    reduced = [_reduce_scattered(g, lv, n) for (n, g), lv in zip(scattered, landed)]
    gr.update({n: r for (n, _), r in zip(scattered[:2], reduced)})
    w_in_above = reduced[-1] if pending_w_in is not None else None
    own = lax.dynamic_update_index_in_dim(landed[-1], small, _logical(_me()), 0)
    summed = _unpack(_sum_leading(own, "reduce_small"), [gr[n] for n in SMALL_GRADS])
    gr.update(dict(zip(SMALL_GRADS, summed)))
    dproj = _dproj(dq, dkT, dvT, du_skip, du[0], du[1], reach)
    dh = _mm(dproj, gw['w_in'], M=S, N=D, K=4 * W, tb=True, a_st='k', b_st='k', ns=4, out_dtype=F32, cap_n=2048, name="proj_bwd_x")
    gr['w_in'] = _mm(sv['h'], dproj, M=D, N=4 * W, K=S, ta=True, b_st='n', o_st='n', ns=4, out_dtype=BF16,
                     name="proj_bwd_w")
    dx, d_sc1, d_sh1 = _mod_bwd(ds1, dh, sv['x'], m6, 1, alpha, "mod1_bwd")
    gr['m6'] = jnp.concatenate([d_sh1, d_sc1, d_g1, d_sh2, d_sc2, d_g2], axis=0)
    return dx, gr, w_in_above


def _pack(arrays, width):
    flat = jnp.concatenate([a.reshape(-1) for a in arrays])
    n = flat.shape[0]
    rows = -(-n // (8 * width)) * 8
    return jnp.pad(flat, (0, rows * width - n)).reshape(rows, width), n


def _unpack(packed, like):
    flat = packed.reshape(-1)
    out, pos = [], 0
    for a in like:
        out.append(flat[pos:pos + a.size].reshape(a.shape))
        pos += a.size
    return out


def kernel(x, c, w_ada, b_ada, w_in, ssm_lam_re, ssm_lam_im, ssm_log_step, ssm_b_re, ssm_b_im, ssm_c_re, ssm_c_im, ssm_d, w_glu, b_glu, g_attn, g_ssm, w_out, ln1_g, ln1_b, w_mlp1, w_mlp2, ln2_g, ln2_b, loss_target, m_w_ada, m_b_ada, m_w_in, m_ssm_lam_re, m_ssm_lam_im, m_ssm_log_step, m_ssm_b_re, m_ssm_b_im, m_ssm_c_re, m_ssm_c_im, m_ssm_d, m_w_glu, m_b_glu, m_g_attn, m_g_ssm, m_w_out, m_ln1_g, m_ln1_b, m_w_mlp1, m_w_mlp2, m_ln2_g, m_ln2_b, v_w_ada, v_b_ada, v_w_in, v_ssm_lam_re, v_ssm_lam_im, v_ssm_log_step, v_ssm_b_re, v_ssm_b_im, v_ssm_c_re, v_ssm_c_im, v_ssm_d, v_w_glu, v_b_glu, v_g_attn, v_g_ssm, v_w_out, v_ln1_g, v_ln1_b, v_w_mlp1, v_w_mlp2, v_ln2_g, v_ln2_b):
    args = locals()
    w = {n: args[n] for n in WEIGHTS}
    mom = {n: args['m_' + n] for n in WEIGHTS}
    var = {n: args['v_' + n] for n in WEIGHTS}
    L, D, ada_cols = w_ada.shape
    S = x.shape[1]
    W = D // 2
    n_heads = W // HEAD_DIM
    alpha = (2 * L) ** 0.25
    tq = _tile(S, 512)
    ts = _tile(S, 512)
    tables = _bias_tables(n_heads, tq)
    xi, yi, ci = _me()
    shard = 2 * xi + yi
    me = 4 * xi + 2 * yi + ci

    c_all = _all_gather8(c, "gather_c").reshape(N_DEV, D)
    c16 = jnp.pad(c_all, ((0, 16 - N_DEV), (0, 0)))
    b_sh = lax.dynamic_slice_in_dim(b_ada, shard * ada_cols, ada_cols, axis=1)
    mods_sh = jnp.stack([_mm(c16, w_ada[l], M=16, N=ada_cols, K=D, a_tf=_silu, bias=b_sh[l:l + 1], out_dtype=F32,
                             name="ada") for l in range(L)])
    mods_all = _all_gather8(mods_sh, "gather_mods")
    mods = jnp.concatenate([mods_all[2 * j] for j in range(4)], axis=-1)
    m6_all = lax.dynamic_index_in_dim(mods, me, axis=1, keepdims=False).reshape(L, 6, D)

    (a_re, a_im, bb_re, bb_im), disc_vjp = jax.vjp(
        _s5_discretize, ssm_lam_re, ssm_lam_im, ssm_log_step, ssm_b_re, ssm_b_im)
    N = a_re.shape[2] * a_re.shape[3]
    per_layer = dict(
        m6=m6_all,
        a=jnp.stack([a_re.reshape(L, 2, N // SB, SB), a_im.reshape(L, 2, N // SB, SB)], axis=3),
        b_re=jax.vmap(_to_blocks)(jnp.swapaxes(bb_re, -1, -2)).astype(BF16),
        b_im=jax.vmap(_to_blocks)(jnp.swapaxes(bb_im, -1, -2)).astype(BF16),
        c_re=jax.vmap(_to_blocks)(ssm_c_re).astype(BF16),
        c_im=jax.vmap(_to_blocks)(ssm_c_im).astype(BF16),
        ssm_d=ssm_d[:, None], b_glu=b_glu[:, None], g_attn=g_attn[:, None], g_ssm=g_ssm[:, None],
        ln1_g=ln1_g[:, None], ln1_b=ln1_b[:, None], ln2_g=ln2_g[:, None], ln2_b=ln2_b[:, None],
        w_in=w_in, w_glu=w_glu, w_out=w_out, w_mlp1=w_mlp1, w_mlp2=w_mlp2)

    layers = [{n: v[l] for n, v in per_layer.items()} for l in range(L)]
    shards = [{n: layers[l][n].astype(BF16) for n in BIG} for l in range(L)]
    first = shards[0]['w_in']
    w_in_full = _join_halves(first, _exchange("gather_w_in", _halves_transfer(first)), "gather_w_in")
    xc, saved = x[0], []
    for l in range(L):
        xc, sv, w_in_full = _forward_layer(xc, layers[l], w_in_full, shards[l],
                                           shards[l + 1]['w_in'] if l + 1 < L else None, alpha, tables, tq, ts)
        saved.append(sv)
    dx, sq = _loss_head(xc, loss_target[0])
    loss = lax.psum(0.5 * jnp.sum(sq) / D, ("x", "y", "c"))

    per_layer_grads = [None] * L
    pending = None
    for l in reversed(range(L)):
        dx, per_layer_grads[l], above = _backward_layer(dx, layers[l], saved[l], pending, alpha, tables, tq, ts)
        if pending is not None:
            per_layer_grads[l + 1]['w_in'] = above
        pending = _as_halves(per_layer_grads[l]['w_in'])
    per_layer_grads[0]['w_in'] = _reduce_scattered(
        pending, _exchange("scatter_w_in", _scatter_transfer(pending)), 'w_in')
    gr = {n: jnp.stack([g[n] for g in per_layer_grads]) for n in per_layer_grads[0]}
    grads = {n: gr[n].reshape(w[n].shape) for n in BIG}

    dm_all = _all_gather8(gr['m6'].reshape(L, 6 * D), "gather_dmods")
    grads['b_ada'] = _sum_leading(jnp.pad(dm_all, ((0, 0), (0, 8 - L), (0, 0))), "reduce_b_ada")[:L]
    dm16 = jnp.pad(lax.dynamic_slice_in_dim(dm_all, shard * ada_cols, ada_cols, axis=2), ((0, 16 - N_DEV), (0, 0), (0, 0)))
    grads['w_ada'] = jnp.stack([_mm(c16, dm16[:, l], M=D, N=ada_cols, K=16, ta=True, a_tf=_silu, out_dtype=F32,
                                    name="ada_bwd_w") for l in range(L)])

    red = gr
    G, P = ssm_lam_re.shape[2], ssm_lam_re.shape[3]
    d_a = jnp.swapaxes(red['a'], 2, 3).reshape(L, 2, 2, G, P)
    (grads['ssm_lam_re'], grads['ssm_lam_im'], grads['ssm_log_step'], grads['ssm_b_re'],
     grads['ssm_b_im']) = disc_vjp((d_a[:, :, 0], d_a[:, :, 1], jnp.swapaxes(red['bb_re'], -1, -2),
                                    jnp.swapaxes(red['bb_im'], -1, -2)))
    grads['ssm_c_re'] = red['cc_re']
    grads['ssm_c_im'] = -red['cc_im']
    for n in ['ssm_d', 'b_glu', 'g_attn', 'g_ssm', 'ln1_g', 'ln1_b', 'ln2_g', 'ln2_b']:
        grads[n] = red[n].reshape(w[n].shape)

    delta, new_m, new_v = {}, {}, {}
    for n in ['w_ada'] + BIG:
        two_d = lambda t: t.reshape(-1, t.shape[-1])
        d_, m_, v_ = _adamw(two_d(w[n]), two_d(grads[n]), two_d(mom[n]), two_d(var[n]), "adamw_" + n)
        delta[n], new_m[n], new_v[n] = d_.reshape(w[n].shape), m_.reshape(w[n].shape), v_.reshape(w[n].shape)
    pk = lambda d: _pack([d[n] for n in SMALL], 1024)[0]
    d_, m_, v_ = _adamw(pk(w), pk(grads), pk(mom), pk(var), "adamw_small")
    like = [w[n] for n in SMALL]
    for n, dn, mn, vn in zip(SMALL, _unpack(d_, like), _unpack(m_, like), _unpack(v_, like)):
        delta[n], new_m[n], new_v[n] = dn, mn, vn

    return (loss, dx[None], *[grads[n] for n in WEIGHTS], *[delta[n] for n in WEIGHTS],
            *[new_m[n] for n in WEIGHTS], *[new_v[n] for n in WEIGHTS])
```
